```python
import math
import jax, jax.numpy as jnp
from jax import lax
import numpy as np

D_MODEL = 1024
BATCH = 8
SEQ = 4096
DEPTH = 4

N_MIXERS = 4
HEAD_DIM = 64
Q_BLOCK = 128
ROPE_THETA = 500000.0
ROT_DIM = HEAD_DIM // 4
NORM_EPS = 1e-6
D_FF = 128 * ((8 * D_MODEL // 3 + 127) // 128)
PLE_DIM = 256
FOX_HEADS = D_MODEL // HEAD_DIM
DIL_HEADS = D_MODEL // HEAD_DIM
DIL_CONFIGS = ((128, 1), (512, 4), (2048, 16))
DIL_GROUPS = len(DIL_CONFIGS)
DIL_BAND = 128
DIFF_HEADS = D_MODEL // (2 * HEAD_DIM)
DIFF_SUBLN_EPS = 1e-5
SGU_HALF = 2 * D_MODEL
SGU_GROUPS = 8
SGU_GROUP_DIM = SGU_HALF // SGU_GROUPS
SGU_CHUNK = 128
N_FOX = (DEPTH + 3) // 4
N_DIL = (DEPTH + 2) // 4
N_DIFF = (DEPTH + 1) // 4
N_SGU = DEPTH // 4

kernel_name = "hybrid_interleaved_fox_dilated_diff_sgu"


def rms_norm(x, g, eps=NORM_EPS):
    xf = x.astype(jnp.float32)
    y = xf * lax.rsqrt(jnp.mean(xf * xf, axis=-1, keepdims=True) + eps)
    return (y * g.astype(jnp.float32)).astype(x.dtype)


def swiglu(h, w_in, w_out):
    gate, up = jnp.split(h @ w_in, 2, axis=-1)
    return (jax.nn.silu(gate) * up) @ w_out


def partial_rope(t, pos):
    half = ROT_DIM // 2
    inv_freq = ROPE_THETA ** (-jnp.arange(0, ROT_DIM, 2, dtype=jnp.float32) / ROT_DIM)
    ang = pos.astype(jnp.float32)[:, None] * inv_freq[None, :]
    bshape = (pos.shape[0],) + (1,) * (t.ndim - 3) + (half,)
    cos = jnp.cos(ang).reshape(bshape)
    sin = jnp.sin(ang).reshape(bshape)
    tf = t[..., :ROT_DIM].astype(jnp.float32)
    x1, x2 = tf[..., :half], tf[..., half:]
    rot = jnp.concatenate([x1 * cos - x2 * sin, x2 * cos + x1 * sin], axis=-1)
    return jnp.concatenate([rot.astype(t.dtype), t[..., ROT_DIM:]], axis=-1)


def to_blocks(t):
    b, s = t.shape[:2]
    return jnp.swapaxes(t.reshape((b, s // Q_BLOCK, Q_BLOCK) + t.shape[2:]), 0, 1)


def from_blocks(t):
    nb, b = t.shape[:2]
    return jnp.swapaxes(t, 0, 1).reshape((b, nb * Q_BLOCK) + t.shape[3:])


def causal_probs(q_blk, k, start, bias=None):
    s = jnp.einsum('bqhd,bkhd->bhqk', q_blk, k, preferred_element_type=jnp.float32)
    s = s * (q_blk.shape[-1] ** -0.5)
    if bias is not None:
        s = s + bias
    qpos = start + jnp.arange(q_blk.shape[1])
    kpos = jnp.arange(k.shape[1])
    s = jnp.where(kpos[None, :] <= qpos[:, None], s, -jnp.inf)
    return jax.nn.softmax(s, axis=-1)


def fox_attention(h, w_in, b_f, w_out):
    b, s, _ = h.shape
    proj = h @ w_in
    q = proj[..., :D_MODEL].reshape(b, s, FOX_HEADS, HEAD_DIM)
    k = proj[..., D_MODEL:2 * D_MODEL].reshape(b, s, FOX_HEADS, HEAD_DIM)
    v = proj[..., 2 * D_MODEL:3 * D_MODEL].reshape(b, s, FOX_HEADS, HEAD_DIM)
    log_f = jax.nn.log_sigmoid((proj[..., 3 * D_MODEL:] + b_f).astype(jnp.float32))
    c = jnp.cumsum(log_f, axis=1)
    c_k = jnp.transpose(c, (0, 2, 1))

    def block(args):
        q_blk, c_blk, start = args
        bias = jnp.transpose(c_blk, (0, 2, 1))[..., None] - c_k[:, :, None, :]
        probs = causal_probs(q_blk, k, start, bias)
        return jnp.einsum('bhqk,bkhd->bqhd', probs.astype(v.dtype), v)

    starts = jnp.arange(s // Q_BLOCK, dtype=jnp.int32) * Q_BLOCK
    o = from_blocks(lax.map(block, (to_blocks(q), to_blocks(c), starts)))
    return o.reshape(b, s, D_MODEL) @ w_out


def dilated_group(q, k, v, dilation, n_steps):
    b, s, nh, dh = q.shape
    span = dilation * DIL_BAND
    s_pad = -(-s // span) * span
    n_sub = s_pad // dilation
    nb = n_sub // DIL_BAND

    def to_res(t):
        t = jnp.pad(t, ((0, 0), (0, s_pad - s), (0, 0), (0, 0)))
        t = jnp.transpose(t.reshape(b, n_sub, dilation, nh, dh), (0, 2, 1, 3, 4))
        return t.reshape(b * dilation, nb, DIL_BAND, nh, dh)

    def with_prev(t):
        prev = jnp.pad(t, ((0, 0), (1, 0), (0, 0), (0, 0), (0, 0)))[:, :-1]
        return jnp.concatenate([prev, t], axis=2)

    qr = to_res(q)
    kk = with_prev(to_res(k))
    vv = with_prev(to_res(v))
    sc = jnp.einsum('xnqhd,xnkhd->xnhqk', qr, kk, preferred_element_type=jnp.float32) * (dh ** -0.5)
    q_idx = jnp.arange(DIL_BAND)
    k_idx = jnp.arange(2 * DIL_BAND) - DIL_BAND
    steps = q_idx[:, None] - k_idx[None, :]
    in_band = (steps >= 0) & (steps <= n_steps)
    has_prev = (jnp.arange(nb) > 0)[:, None, None] | (k_idx >= 0)[None, None, :]
    mask = in_band[None] & has_prev
    sc = jnp.where(mask[None, :, None], sc, -jnp.inf)
    lse = jax.nn.logsumexp(sc, axis=-1)
    probs = jnp.exp(sc - lse[..., None])
    o = jnp.einsum('xnhqk,xnkhd->xnqhd', probs.astype(v.dtype), vv)
    o = jnp.transpose(o.reshape(b, dilation, n_sub, nh, dh), (0, 2, 1, 3, 4)).reshape(b, s_pad, nh, dh)[:, :s]
    lse = jnp.transpose(lse, (0, 1, 3, 2)).reshape(b, dilation, n_sub, nh)
    lse = jnp.transpose(lse, (0, 2, 1, 3)).reshape(b, s_pad, nh)[:, :s]
    return o, lse


def dilated_attention(h, w_in, w_out, pos):
    b, s, _ = h.shape
    proj = (h @ w_in).reshape(b, s, DIL_GROUPS + 2, DIL_HEADS, HEAD_DIM)
    q = partial_rope(proj[:, :, :DIL_GROUPS], pos)
    k = partial_rope(proj[:, :, DIL_GROUPS], pos)
    v = proj[:, :, DIL_GROUPS + 1]
    outs, lses = [], []
    for g, (window, dilation) in enumerate(DIL_CONFIGS):
        o_g, l_g = dilated_group(q[:, :, g], k, v, dilation, window // dilation)
        outs.append(o_g)
        lses.append(l_g)
    w = jax.nn.softmax(jnp.stack(lses), axis=0)
    o = jnp.einsum('gbsh,gbshd->bshd', w.astype(v.dtype), jnp.stack(outs))
    return o.reshape(b, s, D_MODEL) @ w_out


def diff_attention(h, w_in, lam_params, subln_g, w_out, pos, layer_idx):
    b, s, _ = h.shape
    proj = h @ w_in
    q = partial_rope(proj[..., :D_MODEL].reshape(b, s, DIFF_HEADS, 2, HEAD_DIM), pos)
    k = partial_rope(proj[..., D_MODEL:2 * D_MODEL].reshape(b, s, DIFF_HEADS, 2, HEAD_DIM), pos)
    v = proj[..., 2 * D_MODEL:].reshape(b, s, DIFF_HEADS, 2 * HEAD_DIM)
    lam_init = 0.8 - 0.6 * math.exp(-0.3 * layer_idx)
    lp = lam_params.astype(jnp.float32)
    lam = jnp.exp(jnp.sum(lp[0] * lp[1])) - jnp.exp(jnp.sum(lp[2] * lp[3])) + lam_init
    k1, k2 = k[:, :, :, 0], k[:, :, :, 1]

    def block(args):
        q_blk, start = args
        p1 = causal_probs(q_blk[:, :, :, 0], k1, start)
        p2 = causal_probs(q_blk[:, :, :, 1], k2, start)
        return jnp.einsum('bhqk,bkhd->bqhd', (p1 - lam * p2).astype(v.dtype), v)

    starts = jnp.arange(s // Q_BLOCK, dtype=jnp.int32) * Q_BLOCK
    o = from_blocks(lax.map(block, (to_blocks(q), starts)))
    o = rms_norm(o, subln_g, DIFF_SUBLN_EPS) * (1.0 - lam_init)
    return o.reshape(b, s, D_MODEL) @ w_out


def chunked_sgu(h, w_in, norm_v, w_s, b_s, w_out):
    b, s, _ = h.shape
    u, v = jnp.split(jax.nn.gelu(h @ w_in, approximate=False), 2, axis=-1)
    v = rms_norm(v, norm_v).reshape(b, s // SGU_CHUNK, SGU_CHUNK, SGU_GROUPS, SGU_GROUP_DIM)
    causal = jnp.tril(jnp.ones((SGU_CHUNK, SGU_CHUNK), dtype=bool))
    w = jnp.where(causal[None], w_s, 0)
    mixed = jnp.einsum('gts,bnsgc->bntgc', w.astype(v.dtype), v) + jnp.transpose(b_s)[None, None, :, :, None]
    return (u * mixed.reshape(b, s, SGU_HALF)) @ w_out


def setup_inputs(seed: int = 0) -> dict:
    key = jax.random.key(seed)
    ks = iter(jax.random.split(key, 40))
    D = D_MODEL

    def nrm(shape, scale):
        return scale * jax.random.normal(next(ks), shape, jnp.float32)

    def gain(shape):
        return 1.0 + nrm(shape, 0.05)

    return {
        "x": nrm((BATCH, SEQ, D), 1.0),
        "p": nrm((DEPTH, BATCH, SEQ, PLE_DIM), 1.0),
        "norm_ffn1": gain((DEPTH, D)),
        "w_ffn1_in": nrm((DEPTH, D, 2 * D_FF), D ** -0.5),
        "w_ffn1_out": nrm((DEPTH, D_FF, D), D_FF ** -0.5),
        "norm_mix": gain((DEPTH, D)),
        "norm_ffn2": gain((DEPTH, D)),
        "w_ffn2_in": nrm((DEPTH, D, 2 * D_FF), D ** -0.5),
        "w_ffn2_out": nrm((DEPTH, D_FF, D), D_FF ** -0.5),
        "norm_ple": gain((DEPTH, D)),
        "w_ple_gate": nrm((DEPTH, D, D), D ** -0.5),
        "b_ple_gate": nrm((DEPTH, D), 0.02),
        "w_ple_proj": nrm((DEPTH, PLE_DIM, D), PLE_DIM ** -0.5),
        "fox_w_in": nrm((N_FOX, D, 3 * D + FOX_HEADS), D ** -0.5),
        "fox_b_f": jax.random.uniform(next(ks), (N_FOX, FOX_HEADS), jnp.float32, 1.0, 5.0),
        "fox_w_out": nrm((N_FOX, D, D), D ** -0.5),
        "dil_w_in": nrm((N_DIL, D, (DIL_GROUPS + 2) * DIL_HEADS * HEAD_DIM), D ** -0.5),
        "dil_w_out": nrm((N_DIL, DIL_HEADS * HEAD_DIM, D), D ** -0.5),
        "diff_w_in": nrm((N_DIFF, D, 3 * D), D ** -0.5),
        "diff_lambda": nrm((N_DIFF, 4, HEAD_DIM), 0.1),
        "diff_subln": gain((N_DIFF, 2 * HEAD_DIM)),
        "diff_w_out": nrm((N_DIFF, D, D), D ** -0.5),
        "sgu_w_in": nrm((N_SGU, D, 2 * SGU_HALF), D ** -0.5),
        "sgu_norm_v": gain((N_SGU, SGU_HALF)),
        "sgu_w_s": nrm((N_SGU, SGU_GROUPS, SGU_CHUNK, SGU_CHUNK), SGU_CHUNK ** -0.5),
        "sgu_b_s": 1.0 + nrm((N_SGU, SGU_GROUPS, SGU_CHUNK), 0.1),
        "sgu_w_out": nrm((N_SGU, SGU_HALF, D), SGU_HALF ** -0.5),
        "norm_final": gain((D,)),
    }


def reference(x, p, norm_ffn1, w_ffn1_in, w_ffn1_out, norm_mix, norm_ffn2, w_ffn2_in, w_ffn2_out,
              norm_ple, w_ple_gate, b_ple_gate, w_ple_proj, fox_w_in, fox_b_f, fox_w_out,
              dil_w_in, dil_w_out, diff_w_in, diff_lambda, diff_subln, diff_w_out,
              sgu_w_in, sgu_norm_v, sgu_w_s, sgu_b_s, sgu_w_out, norm_final):
    pos = jnp.arange(x.shape[1], dtype=jnp.int32)
    h = x
    for i in range(DEPTH):
        kind, j = i % N_MIXERS, i // N_MIXERS
        h = h + 0.5 * swiglu(rms_norm(h, norm_ffn1[i]), w_ffn1_in[i], w_ffn1_out[i])
        hn = rms_norm(h, norm_mix[i])
        if kind == 0:
            y = fox_attention(hn, fox_w_in[j], fox_b_f[j], fox_w_out[j])
        elif kind == 1:
            y = dilated_attention(hn, dil_w_in[j], dil_w_out[j], pos)
        elif kind == 2:
            y = diff_attention(hn, diff_w_in[j], diff_lambda[j], diff_subln[j], diff_w_out[j], pos, i)
        else:
            y = chunked_sgu(hn, sgu_w_in[j], sgu_norm_v[j], sgu_w_s[j], sgu_b_s[j], sgu_w_out[j])
        h = h + y
        h = h + 0.5 * swiglu(rms_norm(h, norm_ffn2[i]), w_ffn2_in[i], w_ffn2_out[i])
        gate = jax.nn.sigmoid(rms_norm(h, norm_ple[i]) @ w_ple_gate[i] + b_ple_gate[i])
        h = h + gate * (p[i] @ w_ple_proj[i])
    return rms_norm(h, norm_final)
```

```python
import functools
import math

import jax
import jax.numpy as jnp
from jax import lax
from jax.experimental import pallas as pl
from jax.experimental.pallas import tpu as pltpu

_F32 = jnp.float32
_BF16 = jnp.bfloat16

HEAD_DIM = 64
ROPE_THETA = 500000.0
ROT_DIM = HEAD_DIM // 4
NORM_EPS = 1e-6
DIL_CONFIGS = ((128, 1), (512, 4), (2048, 16))
DIL_BAND = 128
DIFF_SUBLN_EPS = 1e-5
SGU_GROUPS = 8
SGU_CHUNK = 128

LANES = 128
VMEM_LIMIT_BYTES = 56 * 1024 * 1024
ROW_TILE = 512
ATTN_BLOCK = 512
COL_CHUNK = 256


def _params(n_axes):
    return pltpu.CompilerParams(dimension_semantics=("arbitrary",) * n_axes,
                                vmem_limit_bytes=VMEM_LIMIT_BYTES)


def _col_chunk(n, target=COL_CHUNK):
    best = LANES
    for c in range(LANES, target + 1, LANES):
        if n % c == 0:
            best = c
    assert n % best == 0, (n, best)
    return best


def _row_spec(tm, n):
    return pl.BlockSpec((tm, n), lambda i: (i, 0))


def _const_spec(shape):
    return pl.BlockSpec(shape, lambda i: (0,) * len(shape))


def _rms_norm(x, g, eps):
    return x * lax.rsqrt(jnp.mean(x * x, axis=-1, keepdims=True) + eps) * g


def _dot(a, b):
    return jnp.dot(a, b, preferred_element_type=_F32)


def _dot_nt(a, b):
    return lax.dot_general(a, b, (((1,), (1,)), ((), ())), preferred_element_type=_F32)


def _split3(x):
    hi = x.astype(_BF16)
    r1 = x - hi.astype(_F32)
    mid = r1.astype(_BF16)
    lo = (r1 - mid.astype(_F32)).astype(_BF16)
    return hi, mid, lo


def _ffn_kernel(h_ref, g_ref, win_ref, wout_ref, o_ref, hid_ref, *, d_ff, chunk):
    x = h_ref[...]
    xn = _rms_norm(x, g_ref[...], NORM_EPS).astype(_BF16)
    for c in range(0, d_ff, chunk):
        gate = _dot(xn, win_ref[:, c:c + chunk])
        up = _dot(xn, win_ref[:, d_ff + c:d_ff + c + chunk])
        hid_ref[:, c:c + chunk] = (gate * jax.nn.sigmoid(gate) * up).astype(_BF16)
    o_ref[...] = x + 0.5 * _dot(hid_ref[...], wout_ref[...])


def _ffn(h, g, w_in, w_out):
    t, d = h.shape
    d_ff = w_out.shape[0]
    tm = min(ROW_TILE, t)
    return pl.pallas_call(
        functools.partial(_ffn_kernel, d_ff=d_ff, chunk=_col_chunk(d_ff)),
        out_shape=jax.ShapeDtypeStruct((t, d), _F32),
        grid=(t // tm,),
        in_specs=[_row_spec(tm, d), _const_spec((1, d)), _const_spec(w_in.shape), _const_spec(w_out.shape)],
        out_specs=_row_spec(tm, d),
        scratch_shapes=[pltpu.VMEM((tm, d_ff), _BF16)],
        compiler_params=_params(1),
        name="ffn",
    )(h, g.reshape(1, d), w_in, w_out)


def _ple_kernel(h_ref, p_ref, g_ref, wg_ref, b_ref, wp_ref, gf_ref, o_ref, *, final_norm):
    x = h_ref[...]
    xn = _rms_norm(x, g_ref[...], NORM_EPS).astype(_BF16)
    gate = jax.nn.sigmoid(_dot(xn, wg_ref[...]) + b_ref[...])
    y = x + gate * _dot(p_ref[...].astype(_BF16), wp_ref[...])
    if final_norm:
        y = _rms_norm(y, gf_ref[...], NORM_EPS)
    o_ref[...] = y


def _ple(h, p, g, w_gate, b_gate, w_proj, g_final, final_norm):
    t, d = h.shape
    dp = p.shape[1]
    tm = min(ROW_TILE, t)
    return pl.pallas_call(
        functools.partial(_ple_kernel, final_norm=final_norm),
        out_shape=jax.ShapeDtypeStruct((t, d), _F32),
        grid=(t // tm,),
        in_specs=[_row_spec(tm, d), _row_spec(tm, dp), _const_spec((1, d)), _const_spec(w_gate.shape),
                  _const_spec((1, d)), _const_spec(w_proj.shape), _const_spec((1, d))],
        out_specs=_row_spec(tm, d),
        compiler_params=_params(1),
        name="ple",
    )(h, p, g.reshape(1, d), w_gate, b_gate.reshape(1, d), w_proj, g_final.reshape(1, d))


def _rope_tables(s):
    half = ROT_DIM // 2
    inv_freq = ROPE_THETA ** (-jnp.arange(0, ROT_DIM, 2, dtype=_F32) / ROT_DIM)
    ang = jnp.arange(s, dtype=jnp.int32).astype(_F32)[:, None] * inv_freq[None, :]
    cos, sin = jnp.cos(ang), jnp.sin(ang)
    m = jnp.arange(LANES) % HEAD_DIM
    idx = m % half
    first, second = m < half, (m >= half) & (m < ROT_DIM)
    a = jnp.where((first | second)[None, :], cos[:, idx], 1.0)
    b = jnp.where(first[None, :], -sin[:, idx], 0.0)
    c = jnp.where(second[None, :], sin[:, idx], 0.0)
    return a.astype(_F32), b.astype(_F32), c.astype(_F32)


def _proj_kernel(h_ref, g_ref, w_ref, ra_ref, rb_ref, rc_ref, o_ref, *, n_rope, chunk):
    xn = _rms_norm(h_ref[...], g_ref[...], NORM_EPS).astype(_BF16)
    half = ROT_DIM // 2
    for c in range(0, w_ref.shape[1], chunk):
        y = _dot(xn, w_ref[:, c:c + chunk])
        if c < n_rope:
            for s in range(0, chunk, LANES):
                ys = y[:, s:s + LANES]
                ys = (ys * ra_ref[...] + pltpu.roll(ys, LANES - half, axis=1) * rb_ref[...]
                      + pltpu.roll(ys, half, axis=1) * rc_ref[...])
                o_ref[:, c + s:c + s + LANES] = ys.astype(o_ref.dtype)
        else:
            o_ref[:, c:c + chunk] = y.astype(o_ref.dtype)


def _proj(h, g, w, rope, n_rope, seq):
    t, d = h.shape
    n = w.shape[1]
    tm = min(ROW_TILE, seq)
    chunk = _col_chunk(n)
    assert n_rope % chunk == 0
    nblk = seq // tm
    rope_spec = pl.BlockSpec((tm, LANES), lambda i: (i % nblk, 0))
    return pl.pallas_call(
        functools.partial(_proj_kernel, n_rope=n_rope, chunk=chunk),
        out_shape=jax.ShapeDtypeStruct((t, n), _BF16),
        grid=(t // tm,),
        in_specs=[_row_spec(tm, d), _const_spec((1, d)), _const_spec(w.shape), rope_spec, rope_spec, rope_spec],
        out_specs=_row_spec(tm, n),
        compiler_params=_params(1),
        name="proj",
    )(h, g.reshape(1, d), w, *rope)


def _out_proj_kernel(h_ref, o_ref, w_ref, out_ref):
    out_ref[...] = h_ref[...] + _dot(o_ref[...], w_ref[...])


def _out_proj(h, o, w):
    t, d = h.shape
    tm = min(ROW_TILE, t)
    return pl.pallas_call(
        _out_proj_kernel,
        out_shape=jax.ShapeDtypeStruct((t, d), _F32),
        grid=(t // tm,),
        in_specs=[_row_spec(tm, d), _row_spec(tm, o.shape[1]), _const_spec(w.shape)],
        out_specs=_row_spec(tm, d),
        compiler_params=_params(1),
        name="out_proj",
    )(h, o, w)


def _fox_proj_kernel(h_ref, g_ref, w_ref, wf_ref, bf_ref, qkv_ref, lf_ref, *, chunk):
    xn = _rms_norm(h_ref[...], g_ref[...], NORM_EPS).astype(_BF16)
    for c in range(0, w_ref.shape[1], chunk):
        qkv_ref[:, c:c + chunk] = _dot(xn, w_ref[:, c:c + chunk]).astype(qkv_ref.dtype)
    z = _dot(xn, wf_ref[...]) + bf_ref[...]
    lf_ref[...] = jnp.minimum(z, 0.0) - jnp.log1p(jnp.exp(-jnp.abs(z)))


def _fox_proj(h, g, w_qkv, w_f, b_f):
    t, d = h.shape
    n = w_qkv.shape[1]
    tm = min(ROW_TILE, t)
    return pl.pallas_call(
        functools.partial(_fox_proj_kernel, chunk=_col_chunk(n)),
        out_shape=(jax.ShapeDtypeStruct((t, n), _BF16), jax.ShapeDtypeStruct((t, LANES), _F32)),
        grid=(t // tm,),
        in_specs=[_row_spec(tm, d), _const_spec((1, d)), _const_spec(w_qkv.shape), _const_spec(w_f.shape),
                  _const_spec((1, LANES))],
        out_specs=(_row_spec(tm, n), _row_spec(tm, LANES)),
        compiler_params=_params(1),
        name="fox_proj",
    )(h, g.reshape(1, d), w_qkv, w_f, b_f)


def _cumsum_kernel(x_ref, c_ref, ct_ref, carry_ref):
    @pl.when(pl.program_id(1) == 0)
    def _():
        carry_ref[...] = jnp.zeros_like(carry_ref)

    x = x_ref[...]
    ts = x.shape[0]
    row = lax.broadcasted_iota(jnp.int32, (ts, ts), 0)
    col = lax.broadcasted_iota(jnp.int32, (ts, ts), 1)
    tri = jnp.where(col <= row, 1.0, 0.0).astype(_BF16)
    hi, mid, lo = _split3(x)
    c = _dot(tri, hi) + _dot(tri, mid) + _dot(tri, lo) + carry_ref[0:1, :]
    c_ref[...] = c
    ct_ref[...] = c.T
    carry_ref[0:1, :] = c[ts - 1:ts, :]


def _cumsum(x, batch, seq, ts):
    ns = seq // ts
    return pl.pallas_call(
        _cumsum_kernel,
        out_shape=(jax.ShapeDtypeStruct((batch * seq, LANES), _F32),
                   jax.ShapeDtypeStruct((batch, ns, LANES, ts), _F32)),
        grid=(batch, ns),
        in_specs=[pl.BlockSpec((ts, LANES), lambda b, s: (b * ns + s, 0))],
        out_specs=(pl.BlockSpec((ts, LANES), lambda b, s: (b * ns + s, 0)),
                   pl.BlockSpec((None, None, LANES, ts), lambda b, s: (b, s, 0, 0))),
        scratch_shapes=[pltpu.VMEM((8, LANES), _F32)],
        compiler_params=_params(2),
        name="cumsum",
    )(x)


def _softmax_step(s, m, l):
    m_new = jnp.maximum(m, jnp.max(s, axis=1, keepdims=True))
    alpha = jnp.exp(m - m_new)
    p = jnp.exp(s - m_new)
    return p, m_new, alpha * l + jnp.sum(p, axis=1, keepdims=True), alpha


def _fox_attn_kernel(q_ref, k_ref, v_ref, c_ref, ct_ref, o_ref, *, blk):
    hp = pl.program_id(1)
    i = pl.program_id(2)
    lane = lax.broadcasted_iota(jnp.int32, (blk, LANES), 1)
    lo = lane < HEAD_DIM
    q = q_ref[...] * jnp.asarray(HEAD_DIM ** -0.5, _BF16)
    zero = jnp.zeros_like(q)
    q_heads = (jnp.where(lo, q, zero), jnp.where(lo, zero, q))
    heads = (2 * hp, 2 * hp + 1)
    c_blk = c_ref[...]
    ci = tuple(jnp.sum(jnp.where(lane == hd, c_blk, 0.0), axis=1, keepdims=True) for hd in heads)

    def step(j, carry, diagonal):
        start = pl.multiple_of(j * blk, blk)
        kb = k_ref[pl.ds(start, blk), :]
        vb = v_ref[pl.ds(start, blk), :]
        ms, ls, acc = carry
        new_m, new_l, alphas, pvs = [], [], [], []
        for e in range(2):
            cj = ct_ref[j, pl.ds(heads[e], 1), :]
            s = _dot_nt(q_heads[e], kb) + (ci[e] - cj)
            if diagonal:
                row = lax.broadcasted_iota(jnp.int32, (blk, blk), 0)
                col = lax.broadcasted_iota(jnp.int32, (blk, blk), 1)
                s = jnp.where(col <= row, s, -jnp.inf)
            p, m_e, l_e, alpha = _softmax_step(s, ms[e], ls[e])
            new_m.append(m_e)
            new_l.append(l_e)
            alphas.append(alpha)
            pvs.append(_dot(p.astype(_BF16), vb))
        acc = acc * jnp.where(lo, alphas[0], alphas[1]) + jnp.where(lo, pvs[0], pvs[1])
        return tuple(new_m), tuple(new_l), acc

    neg = jnp.full((blk, 1), -jnp.inf, _F32)
    zl = jnp.zeros((blk, 1), _F32)
    init = ((neg, neg), (zl, zl), jnp.zeros((blk, LANES), _F32))
    carry = lax.fori_loop(0, i, lambda j, c: step(j, c, False), init)
    _, ls, acc = step(i, carry, True)
    o_ref[...] = (acc / jnp.where(lo, ls[0], ls[1])).astype(o_ref.dtype)


def _fox_attn(qkv, c, ct, batch, seq, d):
    blk = min(ATTN_BLOCK, seq)
    nq = seq // blk
    pairs = d // LANES
    return pl.pallas_call(
        functools.partial(_fox_attn_kernel, blk=blk),
        out_shape=jax.ShapeDtypeStruct((batch * seq, d), _BF16),
        grid=(batch, pairs, nq),
        in_specs=[pl.BlockSpec((blk, LANES), lambda b, h, i: (b * nq + i, h)),
                  pl.BlockSpec((seq, LANES), lambda b, h, i: (b, pairs + h)),
                  pl.BlockSpec((seq, LANES), lambda b, h, i: (b, 2 * pairs + h)),
                  pl.BlockSpec((blk, LANES), lambda b, h, i: (b * nq + i, 0)),
                  pl.BlockSpec((None, nq, 2 * pairs, blk), lambda b, h, i: (b, 0, 0, 0))],
        out_specs=pl.BlockSpec((blk, LANES), lambda b, h, i: (b * nq + i, h)),
        compiler_params=_params(3),
        name="fox_attn",
    )(qkv, qkv, qkv, c, ct)


def _fox_mixer(h, g, w_in, b_f, w_out, batch, seq):
    t, d = h.shape
    n_heads = d // HEAD_DIM
    w_qkv = w_in[:, :3 * d].astype(_BF16)
    w_f = jnp.pad(w_in[:, 3 * d:], ((0, 0), (0, LANES - n_heads))).astype(_BF16)
    b_pad = jnp.pad(b_f, (0, LANES - n_heads)).reshape(1, LANES)
    qkv, log_f = _fox_proj(h, g, w_qkv, w_f, b_pad)
    c, ct = _cumsum(log_f, batch, seq, min(ATTN_BLOCK, seq))
    o = _fox_attn(qkv, c, ct, batch, seq, d)
    return _out_proj(h, o, w_out.astype(_BF16))


def _diff_attn_kernel(q_ref, k_ref, v_ref, lam_ref, g_ref, o_ref, *, blk, lam_init):
    i = pl.program_id(2)
    lane = lax.broadcasted_iota(jnp.int32, (blk, LANES), 1)
    lo = lane < HEAD_DIM
    q = q_ref[...] * jnp.asarray(HEAD_DIM ** -0.5, _BF16)
    zero = jnp.zeros_like(q)
    q_maps = (jnp.where(lo, q, zero), jnp.where(lo, zero, q))

    def step(j, carry, diagonal):
        start = pl.multiple_of(j * blk, blk)
        kb = k_ref[pl.ds(start, blk), :]
        vb = v_ref[pl.ds(start, blk), :]
        ms, ls, accs = carry
        new_m, new_l, new_acc = [], [], []
        for e in range(2):
            s = _dot_nt(q_maps[e], kb)
            if diagonal:
                row = lax.broadcasted_iota(jnp.int32, (blk, blk), 0)
                col = lax.broadcasted_iota(jnp.int32, (blk, blk), 1)
                s = jnp.where(col <= row, s, -jnp.inf)
            p, m_e, l_e, alpha = _softmax_step(s, ms[e], ls[e])
            new_m.append(m_e)
            new_l.append(l_e)
            new_acc.append(accs[e] * alpha + _dot(p.astype(_BF16), vb))
        return tuple(new_m), tuple(new_l), tuple(new_acc)

    neg = jnp.full((blk, 1), -jnp.inf, _F32)
    zl = jnp.zeros((blk, 1), _F32)
    za = jnp.zeros((blk, LANES), _F32)
    carry = lax.fori_loop(0, i, lambda j, c: step(j, c, False), ((neg, neg), (zl, zl), (za, za)))
    _, ls, accs = step(i, carry, True)

    lp = lam_ref[...]
    lam = (jnp.exp(jnp.sum(lp[0:1] * lp[1:2], axis=1, keepdims=True))
           - jnp.exp(jnp.sum(lp[2:3] * lp[3:4], axis=1, keepdims=True)) + lam_init)
    o = accs[0] / ls[0] - lam * (accs[1] / ls[1])
    o_ref[...] = (_rms_norm(o, g_ref[...], DIFF_SUBLN_EPS) * (1.0 - lam_init)).astype(o_ref.dtype)


def _diff_attn(proj, lam_params, subln_g, batch, seq, d, lam_init):
    blk = min(ATTN_BLOCK, seq)
    nq = seq // blk
    heads = d // LANES
    return pl.pallas_call(
        functools.partial(_diff_attn_kernel, blk=blk, lam_init=lam_init),
        out_shape=jax.ShapeDtypeStruct((batch * seq, d), _BF16),
        grid=(batch, heads, nq),
        in_specs=[pl.BlockSpec((blk, LANES), lambda b, h, i: (b * nq + i, h)),
                  pl.BlockSpec((seq, LANES), lambda b, h, i: (b, heads + h)),
                  pl.BlockSpec((seq, LANES), lambda b, h, i: (b, 2 * heads + h)),
                  pl.BlockSpec(lam_params.shape, lambda b, h, i: (0, 0)),
                  pl.BlockSpec((1, LANES), lambda b, h, i: (0, 0))],
        out_specs=pl.BlockSpec((blk, LANES), lambda b, h, i: (b * nq + i, h)),
        compiler_params=_params(3),
        name="diff_attn",
    )(proj, proj, proj, lam_params, subln_g.reshape(1, LANES))


def _diff_mixer(h, g, w_in, lam_params, subln_g, w_out, rope, batch, seq, layer_idx):
    t, d = h.shape
    lam_init = 0.8 - 0.6 * math.exp(-0.3 * layer_idx)
    proj = _proj(h, g, w_in.astype(_BF16), rope, 2 * d, seq)
    o = _diff_attn(proj, lam_params, subln_g, batch, seq, d, lam_init)
    return _out_proj(h, o, w_out.astype(_BF16))


def _dil_attn_kernel(q_ref, kp_ref, kc_ref, vp_ref, vc_ref, o_ref, lse_ref, *, n_steps, n_pairs):
    band = q_ref.shape[0]
    has_prev = pl.program_id(1) > 0
    row = lax.broadcasted_iota(jnp.int32, (band, band), 0)
    col = lax.broadcasted_iota(jnp.int32, (band, band), 1)
    valid_prev = jnp.logical_and(col - band >= row - n_steps, has_prev)
    valid_cur = jnp.logical_and(col <= row, col >= row - n_steps)
    lane = lax.broadcasted_iota(jnp.int32, (band, LANES), 1)
    lo = lane < HEAD_DIM
    lse_tile = jnp.zeros((band, LANES), _F32)
    for hp in range(n_pairs):
        sl = slice(hp * LANES, (hp + 1) * LANES)
        q = q_ref[:, sl] * jnp.asarray(HEAD_DIM ** -0.5, _BF16)
        zero = jnp.zeros_like(q)
        kp, kc, vp, vc = kp_ref[:, sl], kc_ref[:, sl], vp_ref[:, sl], vc_ref[:, sl]
        outs = []
        for e, qm in enumerate((jnp.where(lo, q, zero), jnp.where(lo, zero, q))):
            sp = jnp.where(valid_prev, _dot_nt(qm, kp), -jnp.inf)
            sc = jnp.where(valid_cur, _dot_nt(qm, kc), -jnp.inf)
            m = jnp.maximum(jnp.max(sp, axis=1, keepdims=True), jnp.max(sc, axis=1, keepdims=True))
            ep = jnp.exp(sp - m)
            ec = jnp.exp(sc - m)
            l = jnp.sum(ep, axis=1, keepdims=True) + jnp.sum(ec, axis=1, keepdims=True)
            outs.append(_dot((ep / l).astype(_BF16), vp) + _dot((ec / l).astype(_BF16), vc))
            lse_tile = jnp.where(lane == 2 * hp + e, m + jnp.log(l), lse_tile)
        o_ref[:, sl] = jnp.where(lo, outs[0], outs[1]).astype(o_ref.dtype)
    lse_ref[...] = lse_tile


def _dil_attn(q, q_col, k, k_col, v, v_col, n_seq, n_sub, d, n_steps):
    nb = n_sub // DIL_BAND
    blk = (DIL_BAND, d)
    cur = lambda col: (lambda x, n: (x * nb + n, col))
    prev = lambda col: (lambda x, n: (x * nb + jnp.maximum(n - 1, 0), col))
    return pl.pallas_call(
        functools.partial(_dil_attn_kernel, n_steps=n_steps, n_pairs=d // LANES),
        out_shape=(jax.ShapeDtypeStruct((n_seq * n_sub, d), _BF16),
                   jax.ShapeDtypeStruct((n_seq * n_sub, LANES), _F32)),
        grid=(n_seq, nb),
        in_specs=[pl.BlockSpec(blk, cur(q_col)), pl.BlockSpec(blk, prev(k_col)), pl.BlockSpec(blk, cur(k_col)),
                  pl.BlockSpec(blk, prev(v_col)), pl.BlockSpec(blk, cur(v_col))],
        out_specs=(pl.BlockSpec(blk, cur(0)), pl.BlockSpec((DIL_BAND, LANES), cur(0))),
        compiler_params=_params(2),
        name="dil_attn",
    )(q, k, k, v, v)


def _dil_merge_kernel(h_ref, o1_ref, o2_ref, o3_ref, l1_ref, l2_ref, l3_ref, e_ref, w_ref, out_ref):
    lses = (l1_ref[...], l2_ref[...], l3_ref[...])
    m = jnp.maximum(jnp.maximum(lses[0], lses[1]), lses[2])
    es = tuple(jnp.exp(l - m) for l in lses)
    den = es[0] + es[1] + es[2]
    acc = jnp.zeros(h_ref.shape, _F32)
    for e, o_ref in zip(es, (o1_ref, o2_ref, o3_ref)):
        hi, mid, lo = _split3(e / den)
        wide = _dot(hi, e_ref[...]) + _dot(mid, e_ref[...]) + _dot(lo, e_ref[...])
        acc = acc + wide * o_ref[...].astype(_F32)
    out_ref[...] = h_ref[...] + _dot(acc.astype(_BF16), w_ref[...])


def _dil_merge(h, outs, lses, w_out):
    t, d = h.shape
    tm = min(ROW_TILE, t)
    n_heads = d // HEAD_DIM
    expand = (jnp.arange(LANES)[:, None] == (jnp.arange(d) // HEAD_DIM)[None, :]) & (jnp.arange(LANES) < n_heads)[:, None]
    return pl.pallas_call(
        _dil_merge_kernel,
        out_shape=jax.ShapeDtypeStruct((t, d), _F32),
        grid=(t // tm,),
        in_specs=[_row_spec(tm, d)] * 4 + [_row_spec(tm, LANES)] * 3 + [_const_spec((LANES, d)), _const_spec(w_out.shape)],
        out_specs=_row_spec(tm, d),
        compiler_params=_params(1),
        name="dil_merge",
    )(h, *outs, *lses, expand.astype(_BF16), w_out)


def _dil_mixer(h, g, w_in, w_out, rope, batch, seq):
    t, d = h.shape
    n_groups = len(DIL_CONFIGS)
    proj = _proj(h, g, w_in.astype(_BF16), rope, (n_groups + 1) * d, seq)

    def to_res(col, dil):
        x = proj[:, col * d:(col + 1) * d].reshape(batch, seq // dil, dil, d)
        return jnp.transpose(x, (0, 2, 1, 3)).reshape(t, d)

    def from_res(x, dil):
        x = x.reshape(batch, dil, seq // dil, x.shape[-1])
        return jnp.transpose(x, (0, 2, 1, 3)).reshape(t, x.shape[-1])

    outs, lses = [], []
    for gi, (window, dil) in enumerate(DIL_CONFIGS):
        assert seq % (dil * DIL_BAND) == 0
        n_steps = window // dil
        if dil == 1:
            o, lse = _dil_attn(proj, gi, proj, n_groups, proj, n_groups + 1, batch, seq, d, n_steps)
        else:
            o, lse = _dil_attn(to_res(gi, dil), 0, to_res(n_groups, dil), 0, to_res(n_groups + 1, dil), 0,
                               batch * dil, seq // dil, d, n_steps)
            o, lse = from_res(o, dil), from_res(lse, dil)
        outs.append(o)
        lses.append(lse)
    return _dil_merge(h, outs, lses, w_out.astype(_BF16))


def _gelu(x):
    return 0.5 * x * (1.0 + lax.erf(x * math.sqrt(0.5)))


def _sgu_kernel(h_ref, g_ref, win_ref, nv_ref, ws_ref, bs_ref, wout_ref, o_ref, u_ref, v_ref, y_ref, *, chunk):
    x = h_ref[...]
    tm = x.shape[0]
    half = wout_ref.shape[0]
    gdim = half // SGU_GROUPS
    xn = _rms_norm(x, g_ref[...], NORM_EPS).astype(_BF16)
    ssq = jnp.zeros((tm, 1), _F32)
    for c in range(0, half, chunk):
        u_ref[:, c:c + chunk] = _gelu(_dot(xn, win_ref[:, c:c + chunk]))
        vv = _gelu(_dot(xn, win_ref[:, half + c:half + c + chunk]))
        v_ref[:, c:c + chunk] = vv
        ssq = ssq + jnp.sum(vv * vv, axis=1, keepdims=True)
    rstd = lax.rsqrt(ssq / half + NORM_EPS)
    row = lax.broadcasted_iota(jnp.int32, (SGU_CHUNK, SGU_CHUNK), 0)
    col = lax.broadcasted_iota(jnp.int32, (SGU_CHUNK, SGU_CHUNK), 1)
    for gi in range(SGU_GROUPS):
        cols = slice(gi * gdim, (gi + 1) * gdim)
        w = jnp.where(col <= row, ws_ref[gi], jnp.zeros((), _BF16))
        bias = bs_ref[:, gi:gi + 1]
        for r in range(0, tm, SGU_CHUNK):
            rows = slice(r, r + SGU_CHUNK)
            vn = (v_ref[rows, cols] * rstd[rows] * nv_ref[:, cols]).astype(_BF16)
            y_ref[rows, cols] = (u_ref[rows, cols] * (_dot(w, vn) + bias)).astype(_BF16)
    o_ref[...] = x + _dot(y_ref[...], wout_ref[...])


def _sgu_mixer(h, g, w_in, norm_v, w_s, b_s, w_out):
    t, d = h.shape
    half = w_out.shape[0]
    tm = min(ROW_TILE, t)
    assert tm % SGU_CHUNK == 0 and (half // SGU_GROUPS) % LANES == 0
    return pl.pallas_call(
        functools.partial(_sgu_kernel, chunk=_col_chunk(half)),
        out_shape=jax.ShapeDtypeStruct((t, d), _F32),
        grid=(t // tm,),
        in_specs=[_row_spec(tm, d), _const_spec((1, d)), _const_spec(w_in.shape), _const_spec((1, half)),
                  _const_spec(w_s.shape), _const_spec((SGU_CHUNK, SGU_GROUPS)), _const_spec(w_out.shape)],
        out_specs=_row_spec(tm, d),
        scratch_shapes=[pltpu.VMEM((tm, half), _F32), pltpu.VMEM((tm, half), _F32), pltpu.VMEM((tm, half), _BF16)],
        compiler_params=_params(1),
        name="sgu",
    )(h, g.reshape(1, d), w_in.astype(_BF16), norm_v.reshape(1, half), w_s.astype(_BF16), jnp.transpose(b_s),
      w_out.astype(_BF16))


def kernel(x, p, norm_ffn1, w_ffn1_in, w_ffn1_out, norm_mix, norm_ffn2, w_ffn2_in, w_ffn2_out, norm_ple, w_ple_gate, b_ple_gate, w_ple_proj, fox_w_in, fox_b_f, fox_w_out, dil_w_in, dil_w_out, diff_w_in, diff_lambda, diff_subln, diff_w_out, sgu_w_in, sgu_norm_v, sgu_w_s, sgu_b_s, sgu_w_out, norm_final):
    batch, seq, d = x.shape
    depth = p.shape[0]
    t = batch * seq
    h = x.reshape(t, d)
    rope = _rope_tables(seq)
    n_mixers = 4
    for i in range(depth):
        kind, j = i % n_mixers, i // n_mixers
        h = _ffn(h, norm_ffn1[i], w_ffn1_in[i].astype(_BF16), w_ffn1_out[i].astype(_BF16))
        if kind == 0:
            h = _fox_mixer(h, norm_mix[i], fox_w_in[j], fox_b_f[j], fox_w_out[j], batch, seq)
        elif kind == 1:
            h = _dil_mixer(h, norm_mix[i], dil_w_in[j], dil_w_out[j], rope, batch, seq)
        elif kind == 2:
            h = _diff_mixer(h, norm_mix[i], diff_w_in[j], diff_lambda[j], diff_subln[j], diff_w_out[j], rope,
                            batch, seq, i)
        else:
            h = _sgu_mixer(h, norm_mix[i], sgu_w_in[j], sgu_norm_v[j], sgu_w_s[j], sgu_b_s[j], sgu_w_out[j])
        h = _ffn(h, norm_ffn2[i], w_ffn2_in[i].astype(_BF16), w_ffn2_out[i].astype(_BF16))
        h = _ple(h, p[i].reshape(t, p.shape[-1]), norm_ple[i], w_ple_gate[i].astype(_BF16), b_ple_gate[i],
                 w_ple_proj[i].astype(_BF16), norm_final, final_norm=(i == depth - 1))
    return h.reshape(batch, seq, d)
```

```python
import functools
import math

import jax
import jax.numpy as jnp
from jax import lax
from jax.experimental import pallas as pl
from jax.experimental.pallas import tpu as pltpu

_F32 = jnp.float32
_BF16 = jnp.bfloat16

HEAD_DIM = 64
ROPE_THETA = 500000.0
ROT_DIM = HEAD_DIM // 4
NORM_EPS = 1e-6
DIL_CONFIGS = ((128, 1), (512, 4), (2048, 16))
DIL_BAND = 128
DIFF_SUBLN_EPS = 1e-5
SGU_GROUPS = 8
SGU_CHUNK = 128

LANES = 128
VMEM_LIMIT_BYTES = 56 * 1024 * 1024
ROW_TILE = 512
ATTN_BLOCK = 512
COL_CHUNK = 256


def _params(n_axes):
    return pltpu.CompilerParams(dimension_semantics=("arbitrary",) * n_axes,
                                vmem_limit_bytes=VMEM_LIMIT_BYTES)


def _col_chunk(n, target=COL_CHUNK):
    best = LANES
    for c in range(LANES, target + 1, LANES):
        if n % c == 0:
            best = c
    assert n % best == 0, (n, best)
    return best


def _row_spec(tm, n):
    return pl.BlockSpec((tm, n), lambda i: (i, 0))


def _const_spec(shape):
    return pl.BlockSpec(shape, lambda i: (0,) * len(shape))


def _rms_norm(x, g, eps):
    return x * lax.rsqrt(jnp.mean(x * x, axis=-1, keepdims=True) + eps) * g


def _dot(a, b):
    return jnp.dot(a, b, preferred_element_type=_F32)


def _dot_nt(a, b):
    return lax.dot_general(a, b, (((1,), (1,)), ((), ())), preferred_element_type=_F32)


def _split3(x):
    hi = x.astype(_BF16)
    r1 = x - hi.astype(_F32)
    mid = r1.astype(_BF16)
    lo = (r1 - mid.astype(_F32)).astype(_BF16)
    return hi, mid, lo


def _ffn_kernel(h_ref, g_ref, win_ref, wout_ref, o_ref, hid_ref, *, d_ff, chunk):
    x = h_ref[...]
    xn = _rms_norm(x, g_ref[...], NORM_EPS).astype(_BF16)
    for c in range(0, d_ff, chunk):
        gate = _dot(xn, win_ref[:, c:c + chunk])
        up = _dot(xn, win_ref[:, d_ff + c:d_ff + c + chunk])
        hid_ref[:, c:c + chunk] = (gate * jax.nn.sigmoid(gate) * up).astype(_BF16)
    o_ref[...] = x + 0.5 * _dot(hid_ref[...], wout_ref[...])


def _ffn(h, g, w_in, w_out):
    t, d = h.shape
    d_ff = w_out.shape[0]
    tm = min(ROW_TILE, t)
    return pl.pallas_call(
        functools.partial(_ffn_kernel, d_ff=d_ff, chunk=_col_chunk(d_ff)),
        out_shape=jax.ShapeDtypeStruct((t, d), _F32),
        grid=(t // tm,),
        in_specs=[_row_spec(tm, d), _const_spec((1, d)), _const_spec(w_in.shape), _const_spec(w_out.shape)],
        out_specs=_row_spec(tm, d),
        scratch_shapes=[pltpu.VMEM((tm, d_ff), _BF16)],
        compiler_params=_params(1),
        name="ffn",
    )(h, g.reshape(1, d), w_in, w_out)


def _ple_kernel(h_ref, p_ref, g_ref, wg_ref, b_ref, wp_ref, gf_ref, o_ref, *, final_norm):
    x = h_ref[...]
    xn = _rms_norm(x, g_ref[...], NORM_EPS).astype(_BF16)
    gate = jax.nn.sigmoid(_dot(xn, wg_ref[...]) + b_ref[...])
    y = x + gate * _dot(p_ref[...].astype(_BF16), wp_ref[...])
    if final_norm:
        y = _rms_norm(y, gf_ref[...], NORM_EPS)
    o_ref[...] = y


def _ple(h, p, g, w_gate, b_gate, w_proj, g_final, final_norm):
    t, d = h.shape
    dp = p.shape[1]
    tm = min(ROW_TILE, t)
    return pl.pallas_call(
        functools.partial(_ple_kernel, final_norm=final_norm),
        out_shape=jax.ShapeDtypeStruct((t, d), _F32),
        grid=(t // tm,),
        in_specs=[_row_spec(tm, d), _row_spec(tm, dp), _const_spec((1, d)), _const_spec(w_gate.shape),
                  _const_spec((1, d)), _const_spec(w_proj.shape), _const_spec((1, d))],
        out_specs=_row_spec(tm, d),
        compiler_params=_params(1),
        name="ple",
    )(h, p, g.reshape(1, d), w_gate, b_gate.reshape(1, d), w_proj, g_final.reshape(1, d))


def _rope_tables(s):
    half = ROT_DIM // 2
    inv_freq = ROPE_THETA ** (-jnp.arange(0, ROT_DIM, 2, dtype=_F32) / ROT_DIM)
    ang = jnp.arange(s, dtype=jnp.int32).astype(_F32)[:, None] * inv_freq[None, :]
    cos, sin = jnp.cos(ang), jnp.sin(ang)
    m = jnp.arange(LANES) % HEAD_DIM
    idx = m % half
    first, second = m < half, (m >= half) & (m < ROT_DIM)
    a = jnp.where((first | second)[None, :], cos[:, idx], 1.0)
    b = jnp.where(first[None, :], -sin[:, idx], 0.0)
    c = jnp.where(second[None, :], sin[:, idx], 0.0)
    return a.astype(_F32), b.astype(_F32), c.astype(_F32)


def _proj_kernel(h_ref, g_ref, w_ref, ra_ref, rb_ref, rc_ref, o_ref, *, n_rope, chunk):
    xn = _rms_norm(h_ref[...], g_ref[...], NORM_EPS).astype(_BF16)
    half = ROT_DIM // 2
    for c in range(0, w_ref.shape[1], chunk):
        y = _dot(xn, w_ref[:, c:c + chunk])
        if c < n_rope:
            for s in range(0, chunk, LANES):
                ys = y[:, s:s + LANES]
                ys = (ys * ra_ref[...] + pltpu.roll(ys, LANES - half, axis=1) * rb_ref[...]
                      + pltpu.roll(ys, half, axis=1) * rc_ref[...])
                o_ref[:, c + s:c + s + LANES] = ys.astype(o_ref.dtype)
        else:
            o_ref[:, c:c + chunk] = y.astype(o_ref.dtype)


def _proj(h, g, w, rope, n_rope, seq):
    t, d = h.shape
    n = w.shape[1]
    tm = min(ROW_TILE, seq)
    chunk = _col_chunk(n)
    assert n_rope % chunk == 0
    nblk = seq // tm
    rope_spec = pl.BlockSpec((tm, LANES), lambda i: (i % nblk, 0))
    return pl.pallas_call(
        functools.partial(_proj_kernel, n_rope=n_rope, chunk=chunk),
        out_shape=jax.ShapeDtypeStruct((t, n), _BF16),
        grid=(t // tm,),
        in_specs=[_row_spec(tm, d), _const_spec((1, d)), _const_spec(w.shape), rope_spec, rope_spec, rope_spec],
        out_specs=_row_spec(tm, n),
        compiler_params=_params(1),
        name="proj",
    )(h, g.reshape(1, d), w, *rope)


def _out_proj_kernel(h_ref, o_ref, w_ref, out_ref):
    out_ref[...] = h_ref[...] + _dot(o_ref[...], w_ref[...])


def _out_proj(h, o, w):
    t, d = h.shape
    tm = min(ROW_TILE, t)
    return pl.pallas_call(
        _out_proj_kernel,
        out_shape=jax.ShapeDtypeStruct((t, d), _F32),
        grid=(t // tm,),
        in_specs=[_row_spec(tm, d), _row_spec(tm, o.shape[1]), _const_spec(w.shape)],
        out_specs=_row_spec(tm, d),
        compiler_params=_params(1),
        name="out_proj",
    )(h, o, w)


def _fox_proj_kernel(h_ref, g_ref, w_ref, wf_ref, bf_ref, qkv_ref, lf_ref, *, chunk):
    xn = _rms_norm(h_ref[...], g_ref[...], NORM_EPS).astype(_BF16)
    for c in range(0, w_ref.shape[1], chunk):
        qkv_ref[:, c:c + chunk] = _dot(xn, w_ref[:, c:c + chunk]).astype(qkv_ref.dtype)
    z = _dot(xn, wf_ref[...]) + bf_ref[...]
    lf_ref[...] = jnp.minimum(z, 0.0) - jnp.log1p(jnp.exp(-jnp.abs(z)))


def _fox_proj(h, g, w_qkv, w_f, b_f):
    t, d = h.shape
    n = w_qkv.shape[1]
    tm = min(ROW_TILE, t)
    return pl.pallas_call(
        functools.partial(_fox_proj_kernel, chunk=_col_chunk(n)),
        out_shape=(jax.ShapeDtypeStruct((t, n), _BF16), jax.ShapeDtypeStruct((t, LANES), _F32)),
        grid=(t // tm,),
        in_specs=[_row_spec(tm, d), _const_spec((1, d)), _const_spec(w_qkv.shape), _const_spec(w_f.shape),
                  _const_spec((1, LANES))],
        out_specs=(_row_spec(tm, n), _row_spec(tm, LANES)),
        compiler_params=_params(1),
        name="fox_proj",
    )(h, g.reshape(1, d), w_qkv, w_f, b_f)


def _cumsum_kernel(x_ref, c_ref, ct_ref, carry_ref):
    @pl.when(pl.program_id(1) == 0)
    def _():
        carry_ref[...] = jnp.zeros_like(carry_ref)

    x = x_ref[...]
    ts = x.shape[0]
    row = lax.broadcasted_iota(jnp.int32, (ts, ts), 0)
    col = lax.broadcasted_iota(jnp.int32, (ts, ts), 1)
    tri = jnp.where(col <= row, 1.0, 0.0).astype(_BF16)
    hi, mid, lo = _split3(x)
    c = _dot(tri, hi) + _dot(tri, mid) + _dot(tri, lo) + carry_ref[0:1, :]
    c_ref[...] = c
    ct_ref[...] = c.T
    carry_ref[0:1, :] = c[ts - 1:ts, :]


def _cumsum(x, batch, seq, ts):
    ns = seq // ts
    return pl.pallas_call(
        _cumsum_kernel,
        out_shape=(jax.ShapeDtypeStruct((batch * seq, LANES), _F32),
                   jax.ShapeDtypeStruct((batch, ns, LANES, ts), _F32)),
        grid=(batch, ns),
        in_specs=[pl.BlockSpec((ts, LANES), lambda b, s: (b * ns + s, 0))],
        out_specs=(pl.BlockSpec((ts, LANES), lambda b, s: (b * ns + s, 0)),
                   pl.BlockSpec((None, None, LANES, ts), lambda b, s: (b, s, 0, 0))),
        scratch_shapes=[pltpu.VMEM((8, LANES), _F32)],
        compiler_params=_params(2),
        name="cumsum",
    )(x)


def _softmax_step(s, m, l):
    m_new = jnp.maximum(m, jnp.max(s, axis=1, keepdims=True))
    alpha = jnp.exp(m - m_new)
    p = jnp.exp(s - m_new)
    return p, m_new, alpha * l + jnp.sum(p, axis=1, keepdims=True), alpha


def _fox_attn_kernel(q_ref, k_ref, v_ref, c_ref, ct_ref, o_ref, *, blk):
    hp = pl.program_id(1)
    i = pl.program_id(2)
    lane = lax.broadcasted_iota(jnp.int32, (blk, LANES), 1)
    lo = lane < HEAD_DIM
    q = q_ref[...] * jnp.asarray(HEAD_DIM ** -0.5, _BF16)
    zero = jnp.zeros_like(q)
    q_heads = (jnp.where(lo, q, zero), jnp.where(lo, zero, q))
    heads = (2 * hp, 2 * hp + 1)
    c_blk = c_ref[...]
    ci = tuple(jnp.sum(jnp.where(lane == hd, c_blk, 0.0), axis=1, keepdims=True) for hd in heads)

    def step(j, carry, diagonal):
        start = pl.multiple_of(j * blk, blk)
        kb = k_ref[pl.ds(start, blk), :]
        vb = v_ref[pl.ds(start, blk), :]
        ms, ls, acc = carry
        new_m, new_l, alphas, pvs = [], [], [], []
        for e in range(2):
            cj = ct_ref[j, pl.ds(heads[e], 1), :]
            s = _dot_nt(q_heads[e], kb) + (ci[e] - cj)
            if diagonal:
                row = lax.broadcasted_iota(jnp.int32, (blk, blk), 0)
                col = lax.broadcasted_iota(jnp.int32, (blk, blk), 1)
                s = jnp.where(col <= row, s, -jnp.inf)
            p, m_e, l_e, alpha = _softmax_step(s, ms[e], ls[e])
            new_m.append(m_e)
            new_l.append(l_e)
            alphas.append(alpha)
            pvs.append(_dot(p.astype(_BF16), vb))
        acc = acc * jnp.where(lo, alphas[0], alphas[1]) + jnp.where(lo, pvs[0], pvs[1])
        return tuple(new_m), tuple(new_l), acc

    neg = jnp.full((blk, 1), -jnp.inf, _F32)
    zl = jnp.zeros((blk, 1), _F32)
    init = ((neg, neg), (zl, zl), jnp.zeros((blk, LANES), _F32))
    carry = lax.fori_loop(0, i, lambda j, c: step(j, c, False), init)
    _, ls, acc = step(i, carry, True)
    o_ref[...] = (acc / jnp.where(lo, ls[0], ls[1])).astype(o_ref.dtype)


def _fox_attn(qkv, c, ct, batch, seq, d):
    blk = min(ATTN_BLOCK, seq)
    nq = seq // blk
    pairs = d // LANES
    return pl.pallas_call(
        functools.partial(_fox_attn_kernel, blk=blk),
        out_shape=jax.ShapeDtypeStruct((batch * seq, d), _BF16),
        grid=(batch, pairs, nq),
        in_specs=[pl.BlockSpec((blk, LANES), lambda b, h, i: (b * nq + i, h)),
                  pl.BlockSpec((seq, LANES), lambda b, h, i: (b, pairs + h)),
                  pl.BlockSpec((seq, LANES), lambda b, h, i: (b, 2 * pairs + h)),
                  pl.BlockSpec((blk, LANES), lambda b, h, i: (b * nq + i, 0)),
                  pl.BlockSpec((None, nq, 2 * pairs, blk), lambda b, h, i: (b, 0, 0, 0))],
        out_specs=pl.BlockSpec((blk, LANES), lambda b, h, i: (b * nq + i, h)),
        compiler_params=_params(3),
        name="fox_attn",
    )(qkv, qkv, qkv, c, ct)


def _fox_mixer(h, g, w_in, b_f, w_out, batch, seq):
    t, d = h.shape
    n_heads = d // HEAD_DIM
    w_qkv = w_in[:, :3 * d].astype(_BF16)
    w_f = jnp.pad(w_in[:, 3 * d:], ((0, 0), (0, LANES - n_heads))).astype(_BF16)
    b_pad = jnp.pad(b_f, (0, LANES - n_heads)).reshape(1, LANES)
    qkv, log_f = _fox_proj(h, g, w_qkv, w_f, b_pad)
    c, ct = _cumsum(log_f, batch, seq, min(ATTN_BLOCK, seq))
    o = _fox_attn(qkv, c, ct, batch, seq, d)
    return _out_proj(h, o, w_out.astype(_BF16))


def _diff_attn_kernel(q_ref, k_ref, v_ref, lam_ref, g_ref, o_ref, *, blk, lam_init):
    i = pl.program_id(2)
    lane = lax.broadcasted_iota(jnp.int32, (blk, LANES), 1)
    lo = lane < HEAD_DIM
    q = q_ref[...] * jnp.asarray(HEAD_DIM ** -0.5, _BF16)
    zero = jnp.zeros_like(q)
    q_maps = (jnp.where(lo, q, zero), jnp.where(lo, zero, q))

    def step(j, carry, diagonal):
        start = pl.multiple_of(j * blk, blk)
        kb = k_ref[pl.ds(start, blk), :]
        vb = v_ref[pl.ds(start, blk), :]
        ms, ls, accs = carry
        new_m, new_l, new_acc = [], [], []
        for e in range(2):
            s = _dot_nt(q_maps[e], kb)
            if diagonal:
                row = lax.broadcasted_iota(jnp.int32, (blk, blk), 0)
                col = lax.broadcasted_iota(jnp.int32, (blk, blk), 1)
                s = jnp.where(col <= row, s, -jnp.inf)
            p, m_e, l_e, alpha = _softmax_step(s, ms[e], ls[e])
            new_m.append(m_e)
            new_l.append(l_e)
            new_acc.append(accs[e] * alpha + _dot(p.astype(_BF16), vb))
        return tuple(new_m), tuple(new_l), tuple(new_acc)

    neg = jnp.full((blk, 1), -jnp.inf, _F32)
    zl = jnp.zeros((blk, 1), _F32)
    za = jnp.zeros((blk, LANES), _F32)
    carry = lax.fori_loop(0, i, lambda j, c: step(j, c, False), ((neg, neg), (zl, zl), (za, za)))
    _, ls, accs = step(i, carry, True)

    lp = lam_ref[...]
    lam = (jnp.exp(jnp.sum(lp[0:1] * lp[1:2], axis=1, keepdims=True))
           - jnp.exp(jnp.sum(lp[2:3] * lp[3:4], axis=1, keepdims=True)) + lam_init)
    o = accs[0] / ls[0] - lam * (accs[1] / ls[1])
    o_ref[...] = (_rms_norm(o, g_ref[...], DIFF_SUBLN_EPS) * (1.0 - lam_init)).astype(o_ref.dtype)


def _diff_attn(proj, lam_params, subln_g, batch, seq, d, lam_init):
    blk = min(ATTN_BLOCK, seq)
    nq = seq // blk
    heads = d // LANES
    return pl.pallas_call(
        functools.partial(_diff_attn_kernel, blk=blk, lam_init=lam_init),
        out_shape=jax.ShapeDtypeStruct((batch * seq, d), _BF16),
        grid=(batch, heads, nq),
        in_specs=[pl.BlockSpec((blk, LANES), lambda b, h, i: (b * nq + i, h)),
                  pl.BlockSpec((seq, LANES), lambda b, h, i: (b, heads + h)),
                  pl.BlockSpec((seq, LANES), lambda b, h, i: (b, 2 * heads + h)),
                  pl.BlockSpec(lam_params.shape, lambda b, h, i: (0, 0)),
                  pl.BlockSpec((1, LANES), lambda b, h, i: (0, 0))],
        out_specs=pl.BlockSpec((blk, LANES), lambda b, h, i: (b * nq + i, h)),
        compiler_params=_params(3),
        name="diff_attn",
    )(proj, proj, proj, lam_params, subln_g.reshape(1, LANES))


def _diff_mixer(h, g, w_in, lam_params, subln_g, w_out, rope, batch, seq, layer_idx):
    t, d = h.shape
    lam_init = 0.8 - 0.6 * math.exp(-0.3 * layer_idx)
    proj = _proj(h, g, w_in.astype(_BF16), rope, 2 * d, seq)
    o = _diff_attn(proj, lam_params, subln_g, batch, seq, d, lam_init)
    return _out_proj(h, o, w_out.astype(_BF16))


def _dil_attn_kernel(q_ref, kp_ref, kc_ref, vp_ref, vc_ref, o_ref, lse_ref, kcat_ref, vcat_ref, s_ref, p_ref, *,
                     n_steps, n_pairs):
    band = DIL_BAND
    n_heads = 2 * n_pairs
    kcat_ref[0:band, :] = kp_ref[...]
    kcat_ref[band:, :] = kc_ref[...]
    vcat_ref[0:band, :] = vp_ref[...]
    vcat_ref[band:, :] = vc_ref[...]
    row = lax.broadcasted_iota(jnp.int32, (band, 2 * band), 0)
    col = lax.broadcasted_iota(jnp.int32, (band, 2 * band), 1)
    in_band = jnp.logical_and(col <= row + band, col >= row + band - n_steps)
    first_valid = jnp.logical_and(in_band, jnp.logical_or(col >= band, pl.program_id(1) > 0))
    lane = lax.broadcasted_iota(jnp.int32, (band, LANES), 1)
    lo = lane < HEAD_DIM
    for r in range(q_ref.shape[0] // band):
        rows = slice(r * band, (r + 1) * band)
        win = slice(r * band, (r + 2) * band)
        valid = first_valid if r == 0 else in_band
        for hp in range(n_pairs):
            sl = slice(hp * LANES, (hp + 1) * LANES)
            q = q_ref[rows, sl] * jnp.asarray(HEAD_DIM ** -0.5, _BF16)
            zero = jnp.zeros_like(q)
            kw = kcat_ref[win, sl]
            for e, qm in enumerate((jnp.where(lo, q, zero), jnp.where(lo, zero, q))):
                s_ref[(2 * hp + e) * band:(2 * hp + e + 1) * band, :] = jnp.where(valid, _dot_nt(qm, kw), -jnp.inf)
        s = s_ref[...]
        m = jnp.max(s, axis=1, keepdims=True)
        e = jnp.exp(s - m)
        l = jnp.sum(e, axis=1, keepdims=True)
        p_ref[...] = e.astype(_BF16)
        inv_l = 1.0 / l
        lse = m + jnp.log(l)
        lse_tile = jnp.zeros((band, LANES), _F32)
        for hp in range(n_pairs):
            sl = slice(hp * LANES, (hp + 1) * LANES)
            vw = vcat_ref[win, sl]
            ha, hb = slice((2 * hp) * band, (2 * hp + 1) * band), slice((2 * hp + 1) * band, (2 * hp + 2) * band)
            o = jnp.where(lo, _dot(p_ref[ha, :], vw) * inv_l[ha], _dot(p_ref[hb, :], vw) * inv_l[hb])
            o_ref[rows, sl] = o.astype(o_ref.dtype)
            lse_tile = jnp.where(lane == 2 * hp, lse[ha], jnp.where(lane == 2 * hp + 1, lse[hb], lse_tile))
        lse_ref[rows, :] = lse_tile


def _dil_attn(q, q_col, k, k_col, v, v_col, n_seq, n_sub, d, n_steps):
    rb = min(ROW_TILE, n_sub)
    nb = n_sub // rb
    bands = rb // DIL_BAND
    n_heads = d // HEAD_DIM
    cur = lambda col: (lambda x, n: (x * nb + n, col))
    prev = lambda col: (lambda x, n: ((x * nb + n) * bands - jnp.where(n > 0, 1, 0), col))
    return pl.pallas_call(
        functools.partial(_dil_attn_kernel, n_steps=n_steps, n_pairs=d // LANES),
        out_shape=(jax.ShapeDtypeStruct((n_seq * n_sub, d), _BF16),
                   jax.ShapeDtypeStruct((n_seq * n_sub, LANES), _F32)),
        grid=(n_seq, nb),
        in_specs=[pl.BlockSpec((rb, d), cur(q_col)),
                  pl.BlockSpec((DIL_BAND, d), prev(k_col)), pl.BlockSpec((rb, d), cur(k_col)),
                  pl.BlockSpec((DIL_BAND, d), prev(v_col)), pl.BlockSpec((rb, d), cur(v_col))],
        out_specs=(pl.BlockSpec((rb, d), cur(0)), pl.BlockSpec((rb, LANES), cur(0))),
        scratch_shapes=[pltpu.VMEM((rb + DIL_BAND, d), _BF16), pltpu.VMEM((rb + DIL_BAND, d), _BF16),
                        pltpu.VMEM((n_heads * DIL_BAND, 2 * DIL_BAND), _F32),
                        pltpu.VMEM((n_heads * DIL_BAND, 2 * DIL_BAND), _BF16)],
        compiler_params=_params(2),
        name="dil_attn",
    )(q, k, k, v, v)


def _dil_merge_kernel(h_ref, o1_ref, o2_ref, o3_ref, l1_ref, l2_ref, l3_ref, e_ref, w_ref, out_ref):
    lses = (l1_ref[...], l2_ref[...], l3_ref[...])
    m = jnp.maximum(jnp.maximum(lses[0], lses[1]), lses[2])
    es = tuple(jnp.exp(l - m) for l in lses)
    den = es[0] + es[1] + es[2]
    acc = jnp.zeros(h_ref.shape, _F32)
    for e, o_ref in zip(es, (o1_ref, o2_ref, o3_ref)):
        hi, mid, lo = _split3(e / den)
        wide = _dot(hi, e_ref[...]) + _dot(mid, e_ref[...]) + _dot(lo, e_ref[...])
        acc = acc + wide * o_ref[...].astype(_F32)
    out_ref[...] = h_ref[...] + _dot(acc.astype(_BF16), w_ref[...])


def _dil_merge(h, outs, lses, w_out):
    t, d = h.shape
    tm = min(ROW_TILE, t)
    n_heads = d // HEAD_DIM
    expand = (jnp.arange(LANES)[:, None] == (jnp.arange(d) // HEAD_DIM)[None, :]) & (jnp.arange(LANES) < n_heads)[:, None]
    return pl.pallas_call(
        _dil_merge_kernel,
        out_shape=jax.ShapeDtypeStruct((t, d), _F32),
        grid=(t // tm,),
        in_specs=[_row_spec(tm, d)] * 4 + [_row_spec(tm, LANES)] * 3 + [_const_spec((LANES, d)), _const_spec(w_out.shape)],
        out_specs=_row_spec(tm, d),
        compiler_params=_params(1),
        name="dil_merge",
    )(h, *outs, *lses, expand.astype(_BF16), w_out)


def _dil_mixer(h, g, w_in, w_out, rope, batch, seq):
    t, d = h.shape
    n_groups = len(DIL_CONFIGS)
    proj = _proj(h, g, w_in.astype(_BF16), rope, (n_groups + 1) * d, seq)

    def to_res(col, dil):
        x = proj[:, col * d:(col + 1) * d].reshape(batch, seq // dil, dil, d)
        return jnp.transpose(x, (0, 2, 1, 3)).reshape(t, d)

    def from_res(x, dil):
        x = x.reshape(batch, dil, seq // dil, x.shape[-1])
        return jnp.transpose(x, (0, 2, 1, 3)).reshape(t, x.shape[-1])

    outs, lses = [], []
    for gi, (window, dil) in enumerate(DIL_CONFIGS):
        assert seq % (dil * DIL_BAND) == 0
        n_steps = window // dil
        if dil == 1:
            o, lse = _dil_attn(proj, gi, proj, n_groups, proj, n_groups + 1, batch, seq, d, n_steps)
        else:
            o, lse = _dil_attn(to_res(gi, dil), 0, to_res(n_groups, dil), 0, to_res(n_groups + 1, dil), 0,
                               batch * dil, seq // dil, d, n_steps)
            o, lse = from_res(o, dil), from_res(lse, dil)
        outs.append(o)
        lses.append(lse)
    return _dil_merge(h, outs, lses, w_out.astype(_BF16))


def _gelu(x):
    return 0.5 * x * (1.0 + lax.erf(x * math.sqrt(0.5)))


def _sgu_kernel(h_ref, g_ref, win_ref, nv_ref, ws_ref, bs_ref, wout_ref, o_ref, u_ref, v_ref, y_ref, *, chunk):
    x = h_ref[...]
    tm = x.shape[0]
    half = wout_ref.shape[0]
    gdim = half // SGU_GROUPS
    xn = _rms_norm(x, g_ref[...], NORM_EPS).astype(_BF16)
    ssq = jnp.zeros((tm, 1), _F32)
    for c in range(0, half, chunk):
        u_ref[:, c:c + chunk] = _gelu(_dot(xn, win_ref[:, c:c + chunk]))
        vv = _gelu(_dot(xn, win_ref[:, half + c:half + c + chunk]))
        v_ref[:, c:c + chunk] = vv
        ssq = ssq + jnp.sum(vv * vv, axis=1, keepdims=True)
    rstd = lax.rsqrt(ssq / half + NORM_EPS)
    row = lax.broadcasted_iota(jnp.int32, (SGU_CHUNK, SGU_CHUNK), 0)
    col = lax.broadcasted_iota(jnp.int32, (SGU_CHUNK, SGU_CHUNK), 1)
    for gi in range(SGU_GROUPS):
        cols = slice(gi * gdim, (gi + 1) * gdim)
        w = jnp.where(col <= row, ws_ref[gi], jnp.zeros((), _BF16))
        bias = bs_ref[:, gi:gi + 1]
        for r in range(0, tm, SGU_CHUNK):
            rows = slice(r, r + SGU_CHUNK)
            vn = (v_ref[rows, cols] * rstd[rows] * nv_ref[:, cols]).astype(_BF16)
            y_ref[rows, cols] = (u_ref[rows, cols] * (_dot(w, vn) + bias)).astype(_BF16)
    o_ref[...] = x + _dot(y_ref[...], wout_ref[...])


def _sgu_mixer(h, g, w_in, norm_v, w_s, b_s, w_out):
    t, d = h.shape
    half = w_out.shape[0]
    tm = min(ROW_TILE, t)
    assert tm % SGU_CHUNK == 0 and (half // SGU_GROUPS) % LANES == 0
    return pl.pallas_call(
        functools.partial(_sgu_kernel, chunk=_col_chunk(half)),
        out_shape=jax.ShapeDtypeStruct((t, d), _F32),
        grid=(t // tm,),
        in_specs=[_row_spec(tm, d), _const_spec((1, d)), _const_spec(w_in.shape), _const_spec((1, half)),
                  _const_spec(w_s.shape), _const_spec((SGU_CHUNK, SGU_GROUPS)), _const_spec(w_out.shape)],
        out_specs=_row_spec(tm, d),
        scratch_shapes=[pltpu.VMEM((tm, half), _F32), pltpu.VMEM((tm, half), _F32), pltpu.VMEM((tm, half), _BF16)],
        compiler_params=_params(1),
        name="sgu",
    )(h, g.reshape(1, d), w_in.astype(_BF16), norm_v.reshape(1, half), w_s.astype(_BF16), jnp.transpose(b_s),
      w_out.astype(_BF16))


def kernel(x, p, norm_ffn1, w_ffn1_in, w_ffn1_out, norm_mix, norm_ffn2, w_ffn2_in, w_ffn2_out, norm_ple, w_ple_gate, b_ple_gate, w_ple_proj, fox_w_in, fox_b_f, fox_w_out, dil_w_in, dil_w_out, diff_w_in, diff_lambda, diff_subln, diff_w_out, sgu_w_in, sgu_norm_v, sgu_w_s, sgu_b_s, sgu_w_out, norm_final):
    batch, seq, d = x.shape
    depth = p.shape[0]
    t = batch * seq
    h = x.reshape(t, d)
    rope = _rope_tables(seq)
    n_mixers = 4
    for i in range(depth):
        kind, j = i % n_mixers, i // n_mixers
        h = _ffn(h, norm_ffn1[i], w_ffn1_in[i].astype(_BF16), w_ffn1_out[i].astype(_BF16))
        if kind == 0:
            h = _fox_mixer(h, norm_mix[i], fox_w_in[j], fox_b_f[j], fox_w_out[j], batch, seq)
        elif kind == 1:
            h = _dil_mixer(h, norm_mix[i], dil_w_in[j], dil_w_out[j], rope, batch, seq)
        elif kind == 2:
            h = _diff_mixer(h, norm_mix[i], diff_w_in[j], diff_lambda[j], diff_subln[j], diff_w_out[j], rope,
                            batch, seq, i)
        else:
            h = _sgu_mixer(h, norm_mix[i], sgu_w_in[j], sgu_norm_v[j], sgu_w_s[j], sgu_b_s[j], sgu_w_out[j])
        h = _ffn(h, norm_ffn2[i], w_ffn2_in[i].astype(_BF16), w_ffn2_out[i].astype(_BF16))
        h = _ple(h, p[i].reshape(t, p.shape[-1]), norm_ple[i], w_ple_gate[i].astype(_BF16), b_ple_gate[i],
                 w_ple_proj[i].astype(_BF16), norm_final, final_norm=(i == depth - 1))
    return h.reshape(batch, seq, d)
```

```python
import functools
import math

import jax
import jax.numpy as jnp
import numpy as np
from jax import lax
from jax.experimental import pallas as pl
from jax.experimental.pallas import tpu as pltpu

_F32 = jnp.float32
_BF16 = jnp.bfloat16

HEAD_DIM = 64
ROPE_THETA = 500000.0
ROT_DIM = HEAD_DIM // 4
NORM_EPS = 1e-6
DIL_CONFIGS = ((128, 1), (512, 4), (2048, 16))
DIL_BAND = 128
DIFF_SUBLN_EPS = 1e-5
SGU_GROUPS = 8
SGU_CHUNK = 128

LANES = 128
VMEM_LIMIT_BYTES = 56 * 1024 * 1024
ROW_TILE = 512
ATTN_BLOCK = 512
COL_CHUNK = 256


def _params(n_axes):
    return pltpu.CompilerParams(dimension_semantics=("arbitrary",) * n_axes,
                                vmem_limit_bytes=VMEM_LIMIT_BYTES)


def _col_chunk(n, target=COL_CHUNK):
    best = LANES
    for c in range(LANES, target + 1, LANES):
        if n % c == 0:
            best = c
    assert n % best == 0, (n, best)
    return best


def _row_spec(tm, n):
    return pl.BlockSpec((tm, n), lambda i: (i, 0))


def _const_spec(shape):
    return pl.BlockSpec(shape, lambda i: (0,) * len(shape))


def _rms_norm(x, g, eps):
    return x * lax.rsqrt(jnp.mean(x * x, axis=-1, keepdims=True) + eps) * g


def _dot(a, b):
    return jnp.dot(a, b, preferred_element_type=_F32)


def _dot_nt(a, b):
    return lax.dot_general(a, b, (((1,), (1,)), ((), ())), preferred_element_type=_F32)


def _split3(x):
    hi = x.astype(_BF16)
    r1 = x - hi.astype(_F32)
    mid = r1.astype(_BF16)
    lo = (r1 - mid.astype(_F32)).astype(_BF16)
    return hi, mid, lo


def _ffn_kernel(h_ref, g_ref, win_ref, wout_ref, o_ref, hid_ref, *, d_ff, chunk):
    x = h_ref[...]
    xn = _rms_norm(x, g_ref[...], NORM_EPS).astype(_BF16)
    for c in range(0, d_ff, chunk):
        gate = _dot(xn, win_ref[:, c:c + chunk])
        up = _dot(xn, win_ref[:, d_ff + c:d_ff + c + chunk])
        hid_ref[:, c:c + chunk] = (gate * jax.nn.sigmoid(gate) * up).astype(_BF16)
    o_ref[...] = x + 0.5 * _dot(hid_ref[...], wout_ref[...])


def _ffn(h, g, w_in, w_out):
    t, d = h.shape
    d_ff = w_out.shape[0]
    tm = min(ROW_TILE, t)
    return pl.pallas_call(
        functools.partial(_ffn_kernel, d_ff=d_ff, chunk=_col_chunk(d_ff)),
        out_shape=jax.ShapeDtypeStruct((t, d), _F32),
        grid=(t // tm,),
        in_specs=[_row_spec(tm, d), _const_spec((1, d)), _const_spec(w_in.shape), _const_spec(w_out.shape)],
        out_specs=_row_spec(tm, d),
        scratch_shapes=[pltpu.VMEM((tm, d_ff), _BF16)],
        compiler_params=_params(1),
        name="ffn",
    )(h, g.reshape(1, d), w_in, w_out)


def _ple_kernel(h_ref, p_ref, g_ref, wg_ref, b_ref, wp_ref, gf_ref, o_ref, *, final_norm):
    x = h_ref[...]
    xn = _rms_norm(x, g_ref[...], NORM_EPS).astype(_BF16)
    gate = jax.nn.sigmoid(_dot(xn, wg_ref[...]) + b_ref[...])
    y = x + gate * _dot(p_ref[...].astype(_BF16), wp_ref[...])
    if final_norm:
        y = _rms_norm(y, gf_ref[...], NORM_EPS)
    o_ref[...] = y


def _ple(h, p, g, w_gate, b_gate, w_proj, g_final, final_norm):
    t, d = h.shape
    dp = p.shape[1]
    tm = min(ROW_TILE, t)
    return pl.pallas_call(
        functools.partial(_ple_kernel, final_norm=final_norm),
        out_shape=jax.ShapeDtypeStruct((t, d), _F32),
        grid=(t // tm,),
        in_specs=[_row_spec(tm, d), _row_spec(tm, dp), _const_spec((1, d)), _const_spec(w_gate.shape),
                  _const_spec((1, d)), _const_spec(w_proj.shape), _const_spec((1, d))],
        out_specs=_row_spec(tm, d),
        compiler_params=_params(1),
        name="ple",
    )(h, p, g.reshape(1, d), w_gate, b_gate.reshape(1, d), w_proj, g_final.reshape(1, d))


def _rope_tables(s):
    half = ROT_DIM // 2
    inv_freq = ROPE_THETA ** (-jnp.arange(0, ROT_DIM, 2, dtype=_F32) / ROT_DIM)
    ang = jnp.arange(s, dtype=jnp.int32).astype(_F32)[:, None] * inv_freq[None, :]
    cos, sin = jnp.cos(ang), jnp.sin(ang)
    m = jnp.arange(LANES) % HEAD_DIM
    idx = m % half
    first, second = m < half, (m >= half) & (m < ROT_DIM)
    a = jnp.where((first | second)[None, :], cos[:, idx], 1.0)
    b = jnp.where(first[None, :], -sin[:, idx], 0.0)
    c = jnp.where(second[None, :], sin[:, idx], 0.0)
    return a.astype(_F32), b.astype(_F32), c.astype(_F32)


def _proj_kernel(h_ref, g_ref, w_ref, ra_ref, rb_ref, rc_ref, o_ref, *, n_rope, chunk):
    xn = _rms_norm(h_ref[...], g_ref[...], NORM_EPS).astype(_BF16)
    half = ROT_DIM // 2
    for c in range(0, w_ref.shape[1], chunk):
        y = _dot(xn, w_ref[:, c:c + chunk])
        if c < n_rope:
            for s in range(0, chunk, LANES):
                ys = y[:, s:s + LANES]
                ys = (ys * ra_ref[...] + pltpu.roll(ys, LANES - half, axis=1) * rb_ref[...]
                      + pltpu.roll(ys, half, axis=1) * rc_ref[...])
                o_ref[:, c + s:c + s + LANES] = ys.astype(o_ref.dtype)
        else:
            o_ref[:, c:c + chunk] = y.astype(o_ref.dtype)


def _proj(h, g, w, rope, n_rope, seq):
    t, d = h.shape
    n = w.shape[1]
    tm = min(ROW_TILE, seq)
    chunk = _col_chunk(n)
    assert n_rope % chunk == 0
    nblk = seq // tm
    rope_spec = pl.BlockSpec((tm, LANES), lambda i: (i % nblk, 0))
    return pl.pallas_call(
        functools.partial(_proj_kernel, n_rope=n_rope, chunk=chunk),
        out_shape=jax.ShapeDtypeStruct((t, n), _BF16),
        grid=(t // tm,),
        in_specs=[_row_spec(tm, d), _const_spec((1, d)), _const_spec(w.shape), rope_spec, rope_spec, rope_spec],
        out_specs=_row_spec(tm, n),
        compiler_params=_params(1),
        name="proj",
    )(h, g.reshape(1, d), w, *rope)


def _out_proj_kernel(h_ref, o_ref, w_ref, out_ref):
    out_ref[...] = h_ref[...] + _dot(o_ref[...], w_ref[...])


def _out_proj(h, o, w):
    t, d = h.shape
    tm = min(ROW_TILE, t)
    return pl.pallas_call(
        _out_proj_kernel,
        out_shape=jax.ShapeDtypeStruct((t, d), _F32),
        grid=(t // tm,),
        in_specs=[_row_spec(tm, d), _row_spec(tm, o.shape[1]), _const_spec(w.shape)],
        out_specs=_row_spec(tm, d),
        compiler_params=_params(1),
        name="out_proj",
    )(h, o, w)


def _fox_proj_kernel(h_ref, g_ref, w_ref, wf_ref, bf_ref, qkv_ref, lf_ref, *, chunk):
    xn = _rms_norm(h_ref[...], g_ref[...], NORM_EPS).astype(_BF16)
    for c in range(0, w_ref.shape[1], chunk):
        qkv_ref[:, c:c + chunk] = _dot(xn, w_ref[:, c:c + chunk]).astype(qkv_ref.dtype)
    z = _dot(xn, wf_ref[...]) + bf_ref[...]
    lf_ref[...] = jnp.minimum(z, 0.0) - jnp.log1p(jnp.exp(-jnp.abs(z)))


def _fox_proj(h, g, w_qkv, w_f, b_f):
    t, d = h.shape
    n = w_qkv.shape[1]
    tm = min(ROW_TILE, t)
    return pl.pallas_call(
        functools.partial(_fox_proj_kernel, chunk=_col_chunk(n)),
        out_shape=(jax.ShapeDtypeStruct((t, n), _BF16), jax.ShapeDtypeStruct((t, LANES), _F32)),
        grid=(t // tm,),
        in_specs=[_row_spec(tm, d), _const_spec((1, d)), _const_spec(w_qkv.shape), _const_spec(w_f.shape),
                  _const_spec((1, LANES))],
        out_specs=(_row_spec(tm, n), _row_spec(tm, LANES)),
        compiler_params=_params(1),
        name="fox_proj",
    )(h, g.reshape(1, d), w_qkv, w_f, b_f)


def _bias_placement(d):
    pairs = d // LANES
    w = np.zeros((3, 3 * LANES, d), np.float32)
    ones = np.zeros((3, 1, d), np.float32)
    for hp in range(pairs):
        base = hp * LANES
        for p in range(3):
            w[0, p * LANES + 2 * hp, base + p] = 1.0
            w[1, p * LANES + 2 * hp + 1, base + 6 + p] = 1.0
            w[2, p * LANES + 2 * hp, base + 3 + p] = -1.0
            w[2, p * LANES + 2 * hp + 1, base + 9 + p] = -1.0
            ones[0, 0, base + 3 + p] = 1.0
            ones[1, 0, base + 9 + p] = 1.0
            ones[2, 0, base + p] = 1.0
            ones[2, 0, base + 6 + p] = 1.0
    return jnp.asarray(w, _BF16), jnp.asarray(ones, _F32)


def _cumsum_kernel(x_ref, w_ref, ones_ref, qea_ref, qeb_ref, ke_ref, carry_ref):
    @pl.when(pl.program_id(1) == 0)
    def _():
        carry_ref[...] = jnp.zeros_like(carry_ref)

    x = x_ref[...]
    ts = x.shape[0]
    row = lax.broadcasted_iota(jnp.int32, (ts, ts), 0)
    col = lax.broadcasted_iota(jnp.int32, (ts, ts), 1)
    tri = jnp.where(col <= row, 1.0, 0.0).astype(_BF16)
    hi, mid, lo = _split3(x)
    c = _dot(tri, hi) + _dot(tri, mid) + _dot(tri, lo) + carry_ref[0:1, :]
    carry_ref[0:1, :] = c[ts - 1:ts, :]
    pieces = jnp.concatenate(_split3(c), axis=1)
    for n, out_ref in enumerate((qea_ref, qeb_ref, ke_ref)):
        out_ref[...] = (_dot(pieces, w_ref[n]) + ones_ref[n]).astype(out_ref.dtype)


def _cumsum_bias(x, batch, seq, d, ts):
    ns = seq // ts
    w, ones = _bias_placement(d)
    row_spec = pl.BlockSpec((ts, d), lambda b, s: (b * ns + s, 0))
    out = jax.ShapeDtypeStruct((batch * seq, d), _BF16)
    return pl.pallas_call(
        _cumsum_kernel,
        out_shape=(out, out, out),
        grid=(batch, ns),
        in_specs=[pl.BlockSpec((ts, LANES), lambda b, s: (b * ns + s, 0)),
                  pl.BlockSpec(w.shape, lambda b, s: (0, 0, 0)), pl.BlockSpec(ones.shape, lambda b, s: (0, 0, 0))],
        out_specs=(row_spec, row_spec, row_spec),
        scratch_shapes=[pltpu.VMEM((8, LANES), _F32)],
        compiler_params=_params(2),
        name="cumsum",
    )(x, w, ones)


def _causal_flash(q_maps, k_refs, v_ref, i, blk):
    def step(start, width, masked, carry):
        parts = [r[pl.ds(start, width), :] for r in k_refs]
        kb = parts[0] if len(parts) == 1 else jnp.concatenate(parts, axis=1)
        vb = v_ref[pl.ds(start, width), :]
        ms, ls, accs = carry
        new_m, new_l, new_acc = [], [], []
        for e in range(2):
            s = _dot_nt(q_maps[e], kb)
            if masked:
                row = lax.broadcasted_iota(jnp.int32, (blk, width), 0)
                col = lax.broadcasted_iota(jnp.int32, (blk, width), 1)
                s = jnp.where(col - row <= i * blk - start, s, -jnp.inf)
            m_new = jnp.maximum(ms[e], jnp.max(s, axis=1, keepdims=True))
            alpha = jnp.exp(ms[e] - m_new)
            p = jnp.exp(s - m_new)
            new_m.append(m_new)
            new_l.append(alpha * ls[e] + jnp.sum(p, axis=1, keepdims=True))
            new_acc.append(accs[e] * alpha + _dot(p.astype(_BF16), vb))
        return tuple(new_m), tuple(new_l), tuple(new_acc)

    neg = jnp.full((blk, 1), -jnp.inf, _F32)
    zl = jnp.zeros((blk, 1), _F32)
    za = jnp.zeros((blk, LANES), _F32)
    carry = lax.fori_loop(
        0, lax.shift_right_logical(i, 1),
        lambda j, c: step(pl.multiple_of(j * 2 * blk, 2 * blk), 2 * blk, False, c),
        ((neg, neg), (zl, zl), (za, za)))
    _, ls, accs = lax.cond(
        (i & 1) == 1,
        lambda c: step(pl.multiple_of((i - 1) * blk, blk), 2 * blk, True, c),
        lambda c: step(pl.multiple_of(i * blk, blk), blk, True, c),
        carry)
    return ls, accs


def _split_maps(q_ref, blk):
    lo = lax.broadcasted_iota(jnp.int32, (blk, LANES), 1) < HEAD_DIM
    q = q_ref[...] * jnp.asarray(HEAD_DIM ** -0.5, _BF16)
    zero = jnp.zeros_like(q)
    return lo, jnp.where(lo, q, zero), jnp.where(lo, zero, q)


def _fox_attn_kernel(q_ref, qea_ref, qeb_ref, k_ref, ke_ref, v_ref, o_ref, *, blk):
    lo, qa, qb = _split_maps(q_ref, blk)
    q_maps = (jnp.concatenate([qa, qea_ref[...]], axis=1), jnp.concatenate([qb, qeb_ref[...]], axis=1))
    ls, accs = _causal_flash(q_maps, (k_ref, ke_ref), v_ref, pl.program_id(2), blk)
    o_ref[...] = jnp.where(lo, accs[0] / ls[0], accs[1] / ls[1]).astype(o_ref.dtype)


def _fox_attn(qkv, qea, qeb, ke, batch, seq, d):
    blk = min(ATTN_BLOCK, seq)
    nq = seq // blk
    pairs = d // LANES
    q_spec = lambda col0: pl.BlockSpec((blk, LANES), lambda b, h, i: (b * nq + i, col0 + h))
    kv_spec = lambda col0: pl.BlockSpec((seq, LANES), lambda b, h, i: (b, col0 + h))
    return pl.pallas_call(
        functools.partial(_fox_attn_kernel, blk=blk),
        out_shape=jax.ShapeDtypeStruct((batch * seq, d), _BF16),
        grid=(batch, pairs, nq),
        in_specs=[q_spec(0), q_spec(0), q_spec(0), kv_spec(pairs), kv_spec(0), kv_spec(2 * pairs)],
        out_specs=q_spec(0),
        compiler_params=_params(3),
        name="fox_attn",
    )(qkv, qea, qeb, qkv, ke, qkv)


def _fox_mixer(h, g, w_in, b_f, w_out, batch, seq):
    t, d = h.shape
    n_heads = d // HEAD_DIM
    w_qkv = w_in[:, :3 * d].astype(_BF16)
    w_f = jnp.pad(w_in[:, 3 * d:], ((0, 0), (0, LANES - n_heads))).astype(_BF16)
    b_pad = jnp.pad(b_f, (0, LANES - n_heads)).reshape(1, LANES)
    qkv, log_f = _fox_proj(h, g, w_qkv, w_f, b_pad)
    qea, qeb, ke = _cumsum_bias(log_f, batch, seq, d, min(ATTN_BLOCK, seq))
    o = _fox_attn(qkv, qea, qeb, ke, batch, seq, d)
    return _out_proj(h, o, w_out.astype(_BF16))


def _diff_attn_kernel(q_ref, k_ref, v_ref, lam_ref, g_ref, o_ref, *, blk, lam_init):
    _, qa, qb = _split_maps(q_ref, blk)
    ls, accs = _causal_flash((qa, qb), (k_ref,), v_ref, pl.program_id(2), blk)
    lp = lam_ref[...]
    lam = (jnp.exp(jnp.sum(lp[0:1] * lp[1:2], axis=1, keepdims=True))
           - jnp.exp(jnp.sum(lp[2:3] * lp[3:4], axis=1, keepdims=True)) + lam_init)
    o = accs[0] / ls[0] - lam * (accs[1] / ls[1])
    o_ref[...] = (_rms_norm(o, g_ref[...], DIFF_SUBLN_EPS) * (1.0 - lam_init)).astype(o_ref.dtype)


def _diff_attn(proj, lam_params, subln_g, batch, seq, d, lam_init):
    blk = min(ATTN_BLOCK, seq)
    nq = seq // blk
    heads = d // LANES
    return pl.pallas_call(
        functools.partial(_diff_attn_kernel, blk=blk, lam_init=lam_init),
        out_shape=jax.ShapeDtypeStruct((batch * seq, d), _BF16),
        grid=(batch, heads, nq),
        in_specs=[pl.BlockSpec((blk, LANES), lambda b, h, i: (b * nq + i, h)),
                  pl.BlockSpec((seq, LANES), lambda b, h, i: (b, heads + h)),
                  pl.BlockSpec((seq, LANES), lambda b, h, i: (b, 2 * heads + h)),
                  pl.BlockSpec(lam_params.shape, lambda b, h, i: (0, 0)),
                  pl.BlockSpec((1, LANES), lambda b, h, i: (0, 0))],
        out_specs=pl.BlockSpec((blk, LANES), lambda b, h, i: (b * nq + i, h)),
        compiler_params=_params(3),
        name="diff_attn",
    )(proj, proj, proj, lam_params, subln_g.reshape(1, LANES))


def _diff_mixer(h, g, w_in, lam_params, subln_g, w_out, rope, batch, seq, layer_idx):
    t, d = h.shape
    lam_init = 0.8 - 0.6 * math.exp(-0.3 * layer_idx)
    proj = _proj(h, g, w_in.astype(_BF16), rope, 2 * d, seq)
    o = _diff_attn(proj, lam_params, subln_g, batch, seq, d, lam_init)
    return _out_proj(h, o, w_out.astype(_BF16))


def _dil_attn_kernel(q_ref, kp_ref, kc_ref, vp_ref, vc_ref, o_ref, lse_ref, kcat_ref, vcat_ref, s_ref, p_ref, *,
                     n_steps, n_pairs):
    band = DIL_BAND
    n_heads = 2 * n_pairs
    kcat_ref[0:band, :] = kp_ref[...]
    kcat_ref[band:, :] = kc_ref[...]
    vcat_ref[0:band, :] = vp_ref[...]
    vcat_ref[band:, :] = vc_ref[...]
    row = lax.broadcasted_iota(jnp.int32, (band, 2 * band), 0)
    col = lax.broadcasted_iota(jnp.int32, (band, 2 * band), 1)
    in_band = jnp.logical_and(col <= row + band, col >= row + band - n_steps)
    first_valid = jnp.logical_and(in_band, jnp.logical_or(col >= band, pl.program_id(1) > 0))
    lane = lax.broadcasted_iota(jnp.int32, (band, LANES), 1)
    lo = lane < HEAD_DIM
    for r in range(q_ref.shape[0] // band):
        rows = slice(r * band, (r + 1) * band)
        win = slice(r * band, (r + 2) * band)
        valid = first_valid if r == 0 else in_band
        for hp in range(n_pairs):
            sl = slice(hp * LANES, (hp + 1) * LANES)
            q = q_ref[rows, sl] * jnp.asarray(HEAD_DIM ** -0.5, _BF16)
            zero = jnp.zeros_like(q)
            kw = kcat_ref[win, sl]
            for e, qm in enumerate((jnp.where(lo, q, zero), jnp.where(lo, zero, q))):
                s_ref[(2 * hp + e) * band:(2 * hp + e + 1) * band, :] = jnp.where(valid, _dot_nt(qm, kw), -jnp.inf)
        s = s_ref[...]
        m = jnp.max(s, axis=1, keepdims=True)
        e = jnp.exp(s - m)
        l = jnp.sum(e, axis=1, keepdims=True)
        p_ref[...] = e.astype(_BF16)
        inv_l = 1.0 / l
        lse = m + jnp.log(l)
        lse_tile = jnp.zeros((band, LANES), _F32)
        for hp in range(n_pairs):
            sl = slice(hp * LANES, (hp + 1) * LANES)
            vw = vcat_ref[win, sl]
            ha, hb = slice((2 * hp) * band, (2 * hp + 1) * band), slice((2 * hp + 1) * band, (2 * hp + 2) * band)
            o = jnp.where(lo, _dot(p_ref[ha, :], vw) * inv_l[ha], _dot(p_ref[hb, :], vw) * inv_l[hb])
            o_ref[rows, sl] = o.astype(o_ref.dtype)
            lse_tile = jnp.where(lane == 2 * hp, lse[ha], lse_tile)
            lse_tile = jnp.where(lane == 2 * hp + 1, lse[hb], lse_tile)
        lse_ref[rows, :] = lse_tile


def _dil_attn(q, q_col, k, k_col, v, v_col, n_seq, n_sub, d, n_steps):
    rb = min(ROW_TILE, n_sub)
    nb = n_sub // rb
    bands = rb // DIL_BAND
    n_heads = d // HEAD_DIM
    cur = lambda col: (lambda x, n: (x * nb + n, col))
    prev = lambda col: (lambda x, n: ((x * nb + n) * bands - jnp.where(n > 0, 1, 0), col))
    return pl.pallas_call(
        functools.partial(_dil_attn_kernel, n_steps=n_steps, n_pairs=d // LANES),
        out_shape=(jax.ShapeDtypeStruct((n_seq * n_sub, d), _BF16),
                   jax.ShapeDtypeStruct((n_seq * n_sub, LANES), _F32)),
        grid=(n_seq, nb),
        in_specs=[pl.BlockSpec((rb, d), cur(q_col)),
                  pl.BlockSpec((DIL_BAND, d), prev(k_col)), pl.BlockSpec((rb, d), cur(k_col)),
                  pl.BlockSpec((DIL_BAND, d), prev(v_col)), pl.BlockSpec((rb, d), cur(v_col))],
        out_specs=(pl.BlockSpec((rb, d), cur(0)), pl.BlockSpec((rb, LANES), cur(0))),
        scratch_shapes=[pltpu.VMEM((rb + DIL_BAND, d), _BF16), pltpu.VMEM((rb + DIL_BAND, d), _BF16),
                        pltpu.VMEM((n_heads * DIL_BAND, 2 * DIL_BAND), _F32),
                        pltpu.VMEM((n_heads * DIL_BAND, 2 * DIL_BAND), _BF16)],
        compiler_params=_params(2),
        name="dil_attn",
    )(q, k, k, v, v)


def _dil_merge_kernel(h_ref, o1_ref, o2_ref, o3_ref, l1_ref, l2_ref, l3_ref, e_ref, w_ref, out_ref):
    lses = (l1_ref[...], l2_ref[...], l3_ref[...])
    m = jnp.maximum(jnp.maximum(lses[0], lses[1]), lses[2])
    es = tuple(jnp.exp(l - m) for l in lses)
    den = es[0] + es[1] + es[2]
    acc = jnp.zeros(h_ref.shape, _F32)
    for e, o_ref in zip(es, (o1_ref, o2_ref, o3_ref)):
        hi, mid, lo = _split3(e / den)
        wide = _dot(hi, e_ref[...]) + _dot(mid, e_ref[...]) + _dot(lo, e_ref[...])
        acc = acc + wide * o_ref[...].astype(_F32)
    out_ref[...] = h_ref[...] + _dot(acc.astype(_BF16), w_ref[...])


def _dil_merge(h, outs, lses, w_out):
    t, d = h.shape
    tm = min(ROW_TILE, t)
    n_heads = d // HEAD_DIM
    expand = (jnp.arange(LANES)[:, None] == (jnp.arange(d) // HEAD_DIM)[None, :]) & (jnp.arange(LANES) < n_heads)[:, None]
    return pl.pallas_call(
        _dil_merge_kernel,
        out_shape=jax.ShapeDtypeStruct((t, d), _F32),
        grid=(t // tm,),
        in_specs=[_row_spec(tm, d)] * 4 + [_row_spec(tm, LANES)] * 3 + [_const_spec((LANES, d)), _const_spec(w_out.shape)],
        out_specs=_row_spec(tm, d),
        compiler_params=_params(1),
        name="dil_merge",
    )(h, *outs, *lses, expand.astype(_BF16), w_out)


def _dil_mixer(h, g, w_in, w_out, rope, batch, seq):
    t, d = h.shape
    n_groups = len(DIL_CONFIGS)
    proj = _proj(h, g, w_in.astype(_BF16), rope, (n_groups + 1) * d, seq)

    def to_res(col, dil):
        x = proj[:, col * d:(col + 1) * d].reshape(batch, seq // dil, dil, d)
        return jnp.transpose(x, (0, 2, 1, 3)).reshape(t, d)

    def from_res(x, dil):
        x = x.reshape(batch, dil, seq // dil, x.shape[-1])
        return jnp.transpose(x, (0, 2, 1, 3)).reshape(t, x.shape[-1])

    outs, lses = [], []
    for gi, (window, dil) in enumerate(DIL_CONFIGS):
        assert seq % (dil * DIL_BAND) == 0
        n_steps = window // dil
        if dil == 1:
            o, lse = _dil_attn(proj, gi, proj, n_groups, proj, n_groups + 1, batch, seq, d, n_steps)
        else:
            o, lse = _dil_attn(to_res(gi, dil), 0, to_res(n_groups, dil), 0, to_res(n_groups + 1, dil), 0,
                               batch * dil, seq // dil, d, n_steps)
            o, lse = from_res(o, dil), from_res(lse, dil)
        outs.append(o)
        lses.append(lse)
    return _dil_merge(h, outs, lses, w_out.astype(_BF16))


def _gelu(x):
    return 0.5 * x * (1.0 + lax.erf(x * math.sqrt(0.5)))


def _sgu_kernel(h_ref, g_ref, win_ref, nv_ref, ws_ref, bs_ref, wout_ref, o_ref, u_ref, v_ref, y_ref, *, chunk):
    x = h_ref[...]
    tm = x.shape[0]
    half = wout_ref.shape[0]
    gdim = half // SGU_GROUPS
    xn = _rms_norm(x, g_ref[...], NORM_EPS).astype(_BF16)
    ssq = jnp.zeros((tm, 1), _F32)
    for c in range(0, half, chunk):
        u_ref[:, c:c + chunk] = _gelu(_dot(xn, win_ref[:, c:c + chunk]))
        vv = _gelu(_dot(xn, win_ref[:, half + c:half + c + chunk]))
        v_ref[:, c:c + chunk] = vv
        ssq = ssq + jnp.sum(vv * vv, axis=1, keepdims=True)
    rstd = lax.rsqrt(ssq / half + NORM_EPS)
    row = lax.broadcasted_iota(jnp.int32, (SGU_CHUNK, SGU_CHUNK), 0)
    col = lax.broadcasted_iota(jnp.int32, (SGU_CHUNK, SGU_CHUNK), 1)
    for gi in range(SGU_GROUPS):
        cols = slice(gi * gdim, (gi + 1) * gdim)
        w = jnp.where(col <= row, ws_ref[gi], jnp.zeros((), _BF16))
        bias = bs_ref[:, gi:gi + 1]
        for r in range(0, tm, SGU_CHUNK):
            rows = slice(r, r + SGU_CHUNK)
            vn = (v_ref[rows, cols] * rstd[rows] * nv_ref[:, cols]).astype(_BF16)
            y_ref[rows, cols] = (u_ref[rows, cols] * (_dot(w, vn) + bias)).astype(_BF16)
    o_ref[...] = x + _dot(y_ref[...], wout_ref[...])


def _sgu_mixer(h, g, w_in, norm_v, w_s, b_s, w_out):
    t, d = h.shape
    half = w_out.shape[0]
    tm = min(ROW_TILE, t)
    assert tm % SGU_CHUNK == 0 and (half // SGU_GROUPS) % LANES == 0
    return pl.pallas_call(
        functools.partial(_sgu_kernel, chunk=_col_chunk(half)),
        out_shape=jax.ShapeDtypeStruct((t, d), _F32),
        grid=(t // tm,),
        in_specs=[_row_spec(tm, d), _const_spec((1, d)), _const_spec(w_in.shape), _const_spec((1, half)),
                  _const_spec(w_s.shape), _const_spec((SGU_CHUNK, SGU_GROUPS)), _const_spec(w_out.shape)],
        out_specs=_row_spec(tm, d),
        scratch_shapes=[pltpu.VMEM((tm, half), _F32), pltpu.VMEM((tm, half), _F32), pltpu.VMEM((tm, half), _BF16)],
        compiler_params=_params(1),
        name="sgu",
    )(h, g.reshape(1, d), w_in.astype(_BF16), norm_v.reshape(1, half), w_s.astype(_BF16), jnp.transpose(b_s),
      w_out.astype(_BF16))


def kernel(x, p, norm_ffn1, w_ffn1_in, w_ffn1_out, norm_mix, norm_ffn2, w_ffn2_in, w_ffn2_out, norm_ple, w_ple_gate, b_ple_gate, w_ple_proj, fox_w_in, fox_b_f, fox_w_out, dil_w_in, dil_w_out, diff_w_in, diff_lambda, diff_subln, diff_w_out, sgu_w_in, sgu_norm_v, sgu_w_s, sgu_b_s, sgu_w_out, norm_final):
    batch, seq, d = x.shape
    depth = p.shape[0]
    t = batch * seq
    h = x.reshape(t, d)
    rope = _rope_tables(seq)
    n_mixers = 4
    for i in range(depth):
        kind, j = i % n_mixers, i // n_mixers
        h = _ffn(h, norm_ffn1[i], w_ffn1_in[i].astype(_BF16), w_ffn1_out[i].astype(_BF16))
        if kind == 0:
            h = _fox_mixer(h, norm_mix[i], fox_w_in[j], fox_b_f[j], fox_w_out[j], batch, seq)
        elif kind == 1:
            h = _dil_mixer(h, norm_mix[i], dil_w_in[j], dil_w_out[j], rope, batch, seq)
        elif kind == 2:
            h = _diff_mixer(h, norm_mix[i], diff_w_in[j], diff_lambda[j], diff_subln[j], diff_w_out[j], rope,
                            batch, seq, i)
        else:
            h = _sgu_mixer(h, norm_mix[i], sgu_w_in[j], sgu_norm_v[j], sgu_w_s[j], sgu_b_s[j], sgu_w_out[j])
        h = _ffn(h, norm_ffn2[i], w_ffn2_in[i].astype(_BF16), w_ffn2_out[i].astype(_BF16))
        h = _ple(h, p[i].reshape(t, p.shape[-1]), norm_ple[i], w_ple_gate[i].astype(_BF16), b_ple_gate[i],
                 w_ple_proj[i].astype(_BF16), norm_final, final_norm=(i == depth - 1))
    return h.reshape(batch, seq, d)
```

```python
import functools
import math

import jax
import jax.numpy as jnp
import numpy as np
from jax import lax
from jax.experimental import pallas as pl
from jax.experimental.pallas import tpu as pltpu

_F32 = jnp.float32
_BF16 = jnp.bfloat16

HEAD_DIM = 64
ROPE_THETA = 500000.0
ROT_DIM = HEAD_DIM // 4
NORM_EPS = 1e-6
DIL_CONFIGS = ((128, 1), (512, 4), (2048, 16))
DIL_BAND = 128
DIFF_SUBLN_EPS = 1e-5
SGU_GROUPS = 8
SGU_CHUNK = 128

LANES = 128
VMEM_LIMIT_BYTES = 56 * 1024 * 1024
ROW_TILE = 512
ATTN_BLOCK = 512
COL_CHUNK = 256
LOG2E = math.log2(math.e)
QK_SCALE = HEAD_DIM ** -0.5 * LOG2E


def _params(n_axes):
    return pltpu.CompilerParams(dimension_semantics=("arbitrary",) * n_axes,
                                vmem_limit_bytes=VMEM_LIMIT_BYTES)


def _col_chunk(n, target=COL_CHUNK):
    best = LANES
    for c in range(LANES, target + 1, LANES):
        if n % c == 0:
            best = c
    assert n % best == 0, (n, best)
    return best


def _row_spec(tm, n):
    return pl.BlockSpec((tm, n), lambda i: (i, 0))


def _const_spec(shape):
    return pl.BlockSpec(shape, lambda i: (0,) * len(shape))


def _rms_norm(x, g, eps):
    return x * lax.rsqrt(jnp.mean(x * x, axis=-1, keepdims=True) + eps) * g


def _dot(a, b):
    return jnp.dot(a, b, preferred_element_type=_F32)


def _dot_nt(a, b):
    return lax.dot_general(a, b, (((1,), (1,)), ((), ())), preferred_element_type=_F32)


def _split3(x):
    hi = x.astype(_BF16)
    r1 = x - hi.astype(_F32)
    mid = r1.astype(_BF16)
    lo = (r1 - mid.astype(_F32)).astype(_BF16)
    return hi, mid, lo


def _ffn_kernel(h_ref, g_ref, win_ref, wout_ref, o_ref, hid_ref, *, d_ff, chunk):
    x = h_ref[...]
    xn = _rms_norm(x, g_ref[...], NORM_EPS).astype(_BF16)
    for c in range(0, d_ff, chunk):
        gate = _dot(xn, win_ref[:, c:c + chunk])
        up = _dot(xn, win_ref[:, d_ff + c:d_ff + c + chunk])
        hid_ref[:, c:c + chunk] = (gate * jax.nn.sigmoid(gate) * up).astype(_BF16)
    o_ref[...] = x + 0.5 * _dot(hid_ref[...], wout_ref[...])


def _ffn(h, g, w_in, w_out):
    t, d = h.shape
    d_ff = w_out.shape[0]
    tm = min(ROW_TILE, t)
    return pl.pallas_call(
        functools.partial(_ffn_kernel, d_ff=d_ff, chunk=_col_chunk(d_ff)),
        out_shape=jax.ShapeDtypeStruct((t, d), _F32),
        grid=(t // tm,),
        in_specs=[_row_spec(tm, d), _const_spec((1, d)), _const_spec(w_in.shape), _const_spec(w_out.shape)],
        out_specs=_row_spec(tm, d),
        scratch_shapes=[pltpu.VMEM((tm, d_ff), _BF16)],
        compiler_params=_params(1),
        name="ffn",
    )(h, g.reshape(1, d), w_in, w_out)


def _ple_kernel(h_ref, p_ref, g_ref, wg_ref, b_ref, wp_ref, gf_ref, o_ref, *, final_norm):
    x = h_ref[...]
    xn = _rms_norm(x, g_ref[...], NORM_EPS).astype(_BF16)
    gate = jax.nn.sigmoid(_dot(xn, wg_ref[...]) + b_ref[...])
    y = x + gate * _dot(p_ref[...].astype(_BF16), wp_ref[...])
    if final_norm:
        y = _rms_norm(y, gf_ref[...], NORM_EPS)
    o_ref[...] = y


def _ple(h, p, g, w_gate, b_gate, w_proj, g_final, final_norm):
    t, d = h.shape
    dp = p.shape[1]
    tm = min(ROW_TILE, t)
    return pl.pallas_call(
        functools.partial(_ple_kernel, final_norm=final_norm),
        out_shape=jax.ShapeDtypeStruct((t, d), _F32),
        grid=(t // tm,),
        in_specs=[_row_spec(tm, d), _row_spec(tm, dp), _const_spec((1, d)), _const_spec(w_gate.shape),
                  _const_spec((1, d)), _const_spec(w_proj.shape), _const_spec((1, d))],
        out_specs=_row_spec(tm, d),
        compiler_params=_params(1),
        name="ple",
    )(h, p, g.reshape(1, d), w_gate, b_gate.reshape(1, d), w_proj, g_final.reshape(1, d))


def _rope_tables(s):
    half = ROT_DIM // 2
    inv_freq = ROPE_THETA ** (-jnp.arange(0, ROT_DIM, 2, dtype=_F32) / ROT_DIM)
    ang = jnp.arange(s, dtype=jnp.int32).astype(_F32)[:, None] * inv_freq[None, :]
    cos, sin = jnp.cos(ang), jnp.sin(ang)
    m = jnp.arange(LANES) % HEAD_DIM
    idx = m % half
    first, second = m < half, (m >= half) & (m < ROT_DIM)
    a = jnp.where((first | second)[None, :], cos[:, idx], 1.0)
    b = jnp.where(first[None, :], -sin[:, idx], 0.0)
    c = jnp.where(second[None, :], sin[:, idx], 0.0)
    return a.astype(_F32), b.astype(_F32), c.astype(_F32)


def _proj_kernel(h_ref, g_ref, w_ref, ra_ref, rb_ref, rc_ref, o_ref, *, n_query, n_rope, chunk):
    xn = _rms_norm(h_ref[...], g_ref[...], NORM_EPS).astype(_BF16)
    half = ROT_DIM // 2
    for c in range(0, w_ref.shape[1], chunk):
        y = _dot(xn, w_ref[:, c:c + chunk])
        if c < n_rope:
            for s in range(0, chunk, LANES):
                ys = y[:, s:s + LANES]
                ys = (ys * ra_ref[...] + pltpu.roll(ys, LANES - half, axis=1) * rb_ref[...]
                      + pltpu.roll(ys, half, axis=1) * rc_ref[...])
                if c < n_query:
                    ys = ys * QK_SCALE
                o_ref[:, c + s:c + s + LANES] = ys.astype(o_ref.dtype)
        else:
            o_ref[:, c:c + chunk] = y.astype(o_ref.dtype)


def _proj(h, g, w, rope, n_query, n_rope, seq):
    t, d = h.shape
    n = w.shape[1]
    tm = min(ROW_TILE, seq)
    chunk = _col_chunk(n)
    assert n_rope % chunk == 0 and n_query % chunk == 0 and n_query <= n_rope
    nblk = seq // tm
    rope_spec = pl.BlockSpec((tm, LANES), lambda i: (i % nblk, 0))
    return pl.pallas_call(
        functools.partial(_proj_kernel, n_query=n_query, n_rope=n_rope, chunk=chunk),
        out_shape=jax.ShapeDtypeStruct((t, n), _BF16),
        grid=(t // tm,),
        in_specs=[_row_spec(tm, d), _const_spec((1, d)), _const_spec(w.shape), rope_spec, rope_spec, rope_spec],
        out_specs=_row_spec(tm, n),
        compiler_params=_params(1),
        name="proj",
    )(h, g.reshape(1, d), w, *rope)


def _out_proj_kernel(h_ref, o_ref, w_ref, out_ref):
    out_ref[...] = h_ref[...] + _dot(o_ref[...], w_ref[...])


def _out_proj(h, o, w):
    t, d = h.shape
    tm = min(ROW_TILE, t)
    return pl.pallas_call(
        _out_proj_kernel,
        out_shape=jax.ShapeDtypeStruct((t, d), _F32),
        grid=(t // tm,),
        in_specs=[_row_spec(tm, d), _row_spec(tm, o.shape[1]), _const_spec(w.shape)],
        out_specs=_row_spec(tm, d),
        compiler_params=_params(1),
        name="out_proj",
    )(h, o, w)


def _fox_proj_kernel(h_ref, g_ref, w_ref, wf_ref, bf_ref, qkv_ref, lf_ref, *, n_query, chunk):
    xn = _rms_norm(h_ref[...], g_ref[...], NORM_EPS).astype(_BF16)
    for c in range(0, w_ref.shape[1], chunk):
        y = _dot(xn, w_ref[:, c:c + chunk])
        if c < n_query:
            y = y * QK_SCALE
        qkv_ref[:, c:c + chunk] = y.astype(qkv_ref.dtype)
    z = _dot(xn, wf_ref[...]) + bf_ref[...]
    lf_ref[...] = jnp.minimum(z, 0.0) - jnp.log1p(jnp.exp(-jnp.abs(z)))


def _fox_proj(h, g, w_qkv, w_f, b_f):
    t, d = h.shape
    n = w_qkv.shape[1]
    tm = min(ROW_TILE, t)
    return pl.pallas_call(
        functools.partial(_fox_proj_kernel, n_query=d, chunk=_col_chunk(n)),
        out_shape=(jax.ShapeDtypeStruct((t, n), _BF16), jax.ShapeDtypeStruct((t, LANES), _F32)),
        grid=(t // tm,),
        in_specs=[_row_spec(tm, d), _const_spec((1, d)), _const_spec(w_qkv.shape), _const_spec(w_f.shape),
                  _const_spec((1, LANES))],
        out_specs=(_row_spec(tm, n), _row_spec(tm, LANES)),
        compiler_params=_params(1),
        name="fox_proj",
    )(h, g.reshape(1, d), w_qkv, w_f, b_f)


def _bias_placement(d):
    pairs = d // LANES
    w = np.zeros((3, 3 * LANES, d), np.float32)
    ones = np.zeros((3, 1, d), np.float32)
    for hp in range(pairs):
        base = hp * LANES
        for p in range(3):
            w[0, p * LANES + 2 * hp, base + p] = 1.0
            w[1, p * LANES + 2 * hp + 1, base + 6 + p] = 1.0
            w[2, p * LANES + 2 * hp, base + 3 + p] = -1.0
            w[2, p * LANES + 2 * hp + 1, base + 9 + p] = -1.0
            ones[0, 0, base + 3 + p] = 1.0
            ones[1, 0, base + 9 + p] = 1.0
            ones[2, 0, base + p] = 1.0
            ones[2, 0, base + 6 + p] = 1.0
    return jnp.asarray(w, _BF16), jnp.asarray(ones, _F32)


def _cumsum_kernel(x_ref, w_ref, ones_ref, qea_ref, qeb_ref, ke_ref, carry_ref):
    @pl.when(pl.program_id(1) == 0)
    def _():
        carry_ref[...] = jnp.zeros_like(carry_ref)

    x = x_ref[...]
    ts = x.shape[0]
    row = lax.broadcasted_iota(jnp.int32, (ts, ts), 0)
    col = lax.broadcasted_iota(jnp.int32, (ts, ts), 1)
    tri = jnp.where(col <= row, 1.0, 0.0).astype(_BF16)
    hi, mid, lo = _split3(x)
    c = _dot(tri, hi) + _dot(tri, mid) + _dot(tri, lo) + carry_ref[0:1, :]
    carry_ref[0:1, :] = c[ts - 1:ts, :]
    pieces = jnp.concatenate(_split3(c * LOG2E), axis=1)
    for n, out_ref in enumerate((qea_ref, qeb_ref, ke_ref)):
        out_ref[...] = (_dot(pieces, w_ref[n]) + ones_ref[n]).astype(out_ref.dtype)


def _cumsum_bias(x, batch, seq, d, ts):
    ns = seq // ts
    w, ones = _bias_placement(d)
    row_spec = pl.BlockSpec((ts, d), lambda b, s: (b * ns + s, 0))
    out = jax.ShapeDtypeStruct((batch * seq, d), _BF16)
    return pl.pallas_call(
        _cumsum_kernel,
        out_shape=(out, out, out),
        grid=(batch, ns),
        in_specs=[pl.BlockSpec((ts, LANES), lambda b, s: (b * ns + s, 0)),
                  pl.BlockSpec(w.shape, lambda b, s: (0, 0, 0)), pl.BlockSpec(ones.shape, lambda b, s: (0, 0, 0))],
        out_specs=(row_spec, row_spec, row_spec),
        scratch_shapes=[pltpu.VMEM((8, LANES), _F32)],
        compiler_params=_params(2),
        name="cumsum",
    )(x, w, ones)


def _causal_flash(q_maps, k_refs, value_maps, i, blk, scratch):
    s_a, s_b, m_ref, acc_ref = scratch
    width = 2 * blk
    n_items = lax.shift_right_logical(i, 1) + 1

    def produce(t, s_ref):
        start = pl.multiple_of(t * width, width)
        parts = [r[pl.ds(start, width), :] for r in k_refs]
        kb = parts[0] if len(parts) == 1 else jnp.concatenate(parts, axis=1)
        for e in range(2):
            s_ref[e] = _dot_nt(q_maps[e], kb)

    def consume(t, s_ref, masked):
        start = pl.multiple_of(t * width, width)
        vbs = value_maps(start, width)
        for e in range(2):
            s = s_ref[e]
            if masked:
                row = lax.broadcasted_iota(jnp.int32, (blk, width), 0)
                col = lax.broadcasted_iota(jnp.int32, (blk, width), 1)
                s = jnp.where(col - row <= i * blk - start, s, -jnp.inf)
            m_old = m_ref[e]
            m_new = jnp.maximum(m_old, jnp.max(s, axis=1, keepdims=True))
            alpha = jnp.exp2(m_old - m_new)
            p = jnp.exp2((s - m_new).astype(_BF16))
            m_ref[e] = m_new
            acc_ref[e] = acc_ref[e] * alpha + _dot(p, vbs[e])

    m_ref[...] = jnp.full(m_ref.shape, -jnp.inf, _F32)
    acc_ref[...] = jnp.zeros(acc_ref.shape, _F32)
    produce(0, s_a)

    def pair(k, carry):
        t = 2 * k
        produce(t + 1, s_b)
        consume(t, s_a, False)
        produce(t + 2, s_a)
        consume(t + 1, s_b, False)
        return carry

    lax.fori_loop(0, lax.shift_right_logical(n_items - 1, 1), pair, 0)
    odd = ((n_items - 1) & 1) == 1

    @pl.when(odd)
    def _():
        produce(n_items - 1, s_b)
        consume(n_items - 2, s_a, False)
        consume(n_items - 1, s_b, True)

    @pl.when(jnp.logical_not(odd))
    def _():
        consume(n_items - 1, s_a, True)

    return acc_ref[0], acc_ref[1]


def _flash_scratch(blk, n_out):
    return [pltpu.VMEM((2, blk, 2 * blk), _F32), pltpu.VMEM((2, blk, 2 * blk), _F32),
            pltpu.VMEM((2, blk, 1), _F32), pltpu.VMEM((2, blk, n_out), _F32)]


def _split_maps(q_ref, blk):
    lo = lax.broadcasted_iota(jnp.int32, (blk, LANES), 1) < HEAD_DIM
    q = q_ref[...]
    zero = jnp.zeros_like(q)
    return lo, jnp.where(lo, q, zero), jnp.where(lo, zero, q)


def _fox_attn_kernel(q_ref, qea_ref, qeb_ref, k_ref, ke_ref, v_ref, o_ref, *scratch, blk):
    lo, qa, qb = _split_maps(q_ref, blk)
    q_maps = (jnp.concatenate([qa, qea_ref[...]], axis=1), jnp.concatenate([qb, qeb_ref[...]], axis=1))

    def value_maps(start, width):
        vb = v_ref[pl.ds(start, width), :]
        keep = lax.broadcasted_iota(jnp.int32, (width, LANES), 1) < HEAD_DIM
        one = jnp.ones_like(vb)
        return jnp.where(keep, vb, one), jnp.where(keep, one, vb)

    acc_a, acc_b = _causal_flash(q_maps, (k_ref, ke_ref), value_maps, pl.program_id(2), blk, scratch)
    out_a = acc_a / pltpu.roll(acc_a, HEAD_DIM, axis=1)
    out_b = acc_b / pltpu.roll(acc_b, HEAD_DIM, axis=1)
    o_ref[...] = jnp.where(lo, out_a, out_b).astype(o_ref.dtype)


def _fox_attn(qkv, qea, qeb, ke, batch, seq, d):
    blk = min(ATTN_BLOCK, seq)
    assert seq % (2 * blk) == 0
    nq = seq // blk
    pairs = d // LANES
    q_spec = lambda col0: pl.BlockSpec((blk, LANES), lambda b, h, i: (b * nq + i, col0 + h))
    kv_spec = lambda col0: pl.BlockSpec((seq, LANES), lambda b, h, i: (b, col0 + h))
    return pl.pallas_call(
        functools.partial(_fox_attn_kernel, blk=blk),
        out_shape=jax.ShapeDtypeStruct((batch * seq, d), _BF16),
        grid=(batch, pairs, nq),
        in_specs=[q_spec(0), q_spec(0), q_spec(0), kv_spec(pairs), kv_spec(0), kv_spec(2 * pairs)],
        out_specs=q_spec(0),
        scratch_shapes=_flash_scratch(blk, LANES),
        compiler_params=_params(3),
        name="fox_attn",
    )(qkv, qea, qeb, qkv, ke, qkv)


def _fox_mixer(h, g, w_in, b_f, w_out, batch, seq):
    t, d = h.shape
    n_heads = d // HEAD_DIM
    w_qkv = w_in[:, :3 * d].astype(_BF16)
    w_f = jnp.pad(w_in[:, 3 * d:], ((0, 0), (0, LANES - n_heads))).astype(_BF16)
    b_pad = jnp.pad(b_f, (0, LANES - n_heads)).reshape(1, LANES)
    qkv, log_f = _fox_proj(h, g, w_qkv, w_f, b_pad)
    qea, qeb, ke = _cumsum_bias(log_f, batch, seq, d, min(ATTN_BLOCK, seq))
    o = _fox_attn(qkv, qea, qeb, ke, batch, seq, d)
    return _out_proj(h, o, w_out.astype(_BF16))


def _diff_attn_kernel(q_ref, k_ref, v_ref, lam_ref, g_ref, o_ref, *scratch, blk, lam_init):
    _, qa, qb = _split_maps(q_ref, blk)

    def value_maps(start, width):
        vb = v_ref[pl.ds(start, width), :]
        vb = jnp.concatenate([vb, jnp.ones_like(vb)], axis=1)
        return vb, vb

    acc_1, acc_2 = _causal_flash((qa, qb), (k_ref,), value_maps, pl.program_id(2), blk, scratch)
    lp = lam_ref[...]
    lam = (jnp.exp(jnp.sum(lp[0:1] * lp[1:2], axis=1, keepdims=True))
           - jnp.exp(jnp.sum(lp[2:3] * lp[3:4], axis=1, keepdims=True)) + lam_init)
    o = acc_1[:, :LANES] / acc_1[:, LANES:] - lam * (acc_2[:, :LANES] / acc_2[:, LANES:])
    o_ref[...] = (_rms_norm(o, g_ref[...], DIFF_SUBLN_EPS) * (1.0 - lam_init)).astype(o_ref.dtype)


def _diff_attn(proj, lam_params, subln_g, batch, seq, d, lam_init):
    blk = min(ATTN_BLOCK, seq)
    assert seq % (2 * blk) == 0
    nq = seq // blk
    heads = d // LANES
    return pl.pallas_call(
        functools.partial(_diff_attn_kernel, blk=blk, lam_init=lam_init),
        out_shape=jax.ShapeDtypeStruct((batch * seq, d), _BF16),
        grid=(batch, heads, nq),
        in_specs=[pl.BlockSpec((blk, LANES), lambda b, h, i: (b * nq + i, h)),
                  pl.BlockSpec((seq, LANES), lambda b, h, i: (b, heads + h)),
                  pl.BlockSpec((seq, LANES), lambda b, h, i: (b, 2 * heads + h)),
                  pl.BlockSpec(lam_params.shape, lambda b, h, i: (0, 0)),
                  pl.BlockSpec((1, LANES), lambda b, h, i: (0, 0))],
        out_specs=pl.BlockSpec((blk, LANES), lambda b, h, i: (b * nq + i, h)),
        scratch_shapes=_flash_scratch(blk, 2 * LANES),
        compiler_params=_params(3),
        name="diff_attn",
    )(proj, proj, proj, lam_params, subln_g.reshape(1, LANES))


def _diff_mixer(h, g, w_in, lam_params, subln_g, w_out, rope, batch, seq, layer_idx):
    t, d = h.shape
    lam_init = 0.8 - 0.6 * math.exp(-0.3 * layer_idx)
    proj = _proj(h, g, w_in.astype(_BF16), rope, d, 2 * d, seq)
    o = _diff_attn(proj, lam_params, subln_g, batch, seq, d, lam_init)
    return _out_proj(h, o, w_out.astype(_BF16))


def _dil_attn_kernel(q_ref, kp_ref, kc_ref, vp_ref, vc_ref, o_ref, lse_ref, kcat_ref, vcat_ref, s_ref, p_ref, *,
                     n_steps, n_pairs):
    band = DIL_BAND
    n_heads = 2 * n_pairs
    kcat_ref[0:band, :] = kp_ref[...]
    kcat_ref[band:, :] = kc_ref[...]
    vcat_ref[0:band, :] = vp_ref[...]
    vcat_ref[band:, :] = vc_ref[...]
    row = lax.broadcasted_iota(jnp.int32, (band, 2 * band), 0)
    col = lax.broadcasted_iota(jnp.int32, (band, 2 * band), 1)
    in_band = jnp.logical_and(col <= row + band, col >= row + band - n_steps)
    first_valid = jnp.logical_and(in_band, jnp.logical_or(col >= band, pl.program_id(1) > 0))
    lane = lax.broadcasted_iota(jnp.int32, (band, LANES), 1)
    lo = lane < HEAD_DIM
    lo_v = lax.broadcasted_iota(jnp.int32, (2 * band, LANES), 1) < HEAD_DIM
    for r in range(q_ref.shape[0] // band):
        rows = slice(r * band, (r + 1) * band)
        win = slice(r * band, (r + 2) * band)
        valid = first_valid if r == 0 else in_band
        for hp in range(n_pairs):
            sl = slice(hp * LANES, (hp + 1) * LANES)
            q = q_ref[rows, sl]
            zero = jnp.zeros_like(q)
            kw = kcat_ref[win, sl]
            for e, qm in enumerate((jnp.where(lo, q, zero), jnp.where(lo, zero, q))):
                s_ref[(2 * hp + e) * band:(2 * hp + e + 1) * band, :] = jnp.where(valid, _dot_nt(qm, kw), -jnp.inf)
        s = s_ref[...]
        m = jnp.max(s, axis=1, keepdims=True)
        p_ref[...] = jnp.exp2((s - m).astype(_BF16))
        lse_tile = jnp.zeros((band, LANES), _F32)
        for hp in range(n_pairs):
            sl = slice(hp * LANES, (hp + 1) * LANES)
            vw = vcat_ref[win, sl]
            one = jnp.ones_like(vw)
            ha, hb = slice((2 * hp) * band, (2 * hp + 1) * band), slice((2 * hp + 1) * band, (2 * hp + 2) * band)
            pv_a = _dot(p_ref[ha, :], jnp.where(lo_v, vw, one))
            pv_b = _dot(p_ref[hb, :], jnp.where(lo_v, one, vw))
            l_a, l_b = pltpu.roll(pv_a, HEAD_DIM, axis=1), pltpu.roll(pv_b, HEAD_DIM, axis=1)
            o_ref[rows, sl] = jnp.where(lo, pv_a / l_a, pv_b / l_b).astype(o_ref.dtype)
            lse_tile = jnp.where(lane == 2 * hp, (m[ha] + jnp.log2(l_a)) * (1.0 / LOG2E), lse_tile)
            lse_tile = jnp.where(lane == 2 * hp + 1, (m[hb] + jnp.log2(pv_b)) * (1.0 / LOG2E), lse_tile)
        lse_ref[rows, :] = lse_tile


def _dil_attn(q, q_col, k, k_col, v, v_col, n_seq, n_sub, d, n_steps):
    rb = min(ROW_TILE, n_sub)
    nb = n_sub // rb
    bands = rb // DIL_BAND
    n_heads = d // HEAD_DIM
    cur = lambda col: (lambda x, n: (x * nb + n, col))
    prev = lambda col: (lambda x, n: ((x * nb + n) * bands - jnp.where(n > 0, 1, 0), col))
    return pl.pallas_call(
        functools.partial(_dil_attn_kernel, n_steps=n_steps, n_pairs=d // LANES),
        out_shape=(jax.ShapeDtypeStruct((n_seq * n_sub, d), _BF16),
                   jax.ShapeDtypeStruct((n_seq * n_sub, LANES), _F32)),
        grid=(n_seq, nb),
        in_specs=[pl.BlockSpec((rb, d), cur(q_col)),
                  pl.BlockSpec((DIL_BAND, d), prev(k_col)), pl.BlockSpec((rb, d), cur(k_col)),
                  pl.BlockSpec((DIL_BAND, d), prev(v_col)), pl.BlockSpec((rb, d), cur(v_col))],
        out_specs=(pl.BlockSpec((rb, d), cur(0)), pl.BlockSpec((rb, LANES), cur(0))),
        scratch_shapes=[pltpu.VMEM((rb + DIL_BAND, d), _BF16), pltpu.VMEM((rb + DIL_BAND, d), _BF16),
                        pltpu.VMEM((n_heads * DIL_BAND, 2 * DIL_BAND), _F32),
                        pltpu.VMEM((n_heads * DIL_BAND, 2 * DIL_BAND), _BF16)],
        compiler_params=_params(2),
        name="dil_attn",
    )(q, k, k, v, v)


def _dil_merge_kernel(h_ref, o1_ref, o2_ref, o3_ref, l1_ref, l2_ref, l3_ref, e_ref, w_ref, out_ref):
    lses = (l1_ref[...], l2_ref[...], l3_ref[...])
    m = jnp.maximum(jnp.maximum(lses[0], lses[1]), lses[2])
    es = tuple(jnp.exp(l - m) for l in lses)
    den = es[0] + es[1] + es[2]
    acc = jnp.zeros(h_ref.shape, _F32)
    for e, o_ref in zip(es, (o1_ref, o2_ref, o3_ref)):
        hi, mid, lo = _split3(e / den)
        wide = _dot(hi, e_ref[...]) + _dot(mid, e_ref[...]) + _dot(lo, e_ref[...])
        acc = acc + wide * o_ref[...].astype(_F32)
    out_ref[...] = h_ref[...] + _dot(acc.astype(_BF16), w_ref[...])


def _dil_merge(h, outs, lses, w_out):
    t, d = h.shape
    tm = min(ROW_TILE, t)
    n_heads = d // HEAD_DIM
    expand = (jnp.arange(LANES)[:, None] == (jnp.arange(d) // HEAD_DIM)[None, :]) & (jnp.arange(LANES) < n_heads)[:, None]
    return pl.pallas_call(
        _dil_merge_kernel,
        out_shape=jax.ShapeDtypeStruct((t, d), _F32),
        grid=(t // tm,),
        in_specs=[_row_spec(tm, d)] * 4 + [_row_spec(tm, LANES)] * 3 + [_const_spec((LANES, d)), _const_spec(w_out.shape)],
        out_specs=_row_spec(tm, d),
        compiler_params=_params(1),
        name="dil_merge",
    )(h, *outs, *lses, expand.astype(_BF16), w_out)


def _dil_mixer(h, g, w_in, w_out, rope, batch, seq):
    t, d = h.shape
    n_groups = len(DIL_CONFIGS)
    proj = _proj(h, g, w_in.astype(_BF16), rope, n_groups * d, (n_groups + 1) * d, seq)

    def to_res(col, dil):
        x = proj[:, col * d:(col + 1) * d].reshape(batch, seq // dil, dil, d)
        return jnp.transpose(x, (0, 2, 1, 3)).reshape(t, d)

    def from_res(x, dil):
        x = x.reshape(batch, dil, seq // dil, x.shape[-1])
        return jnp.transpose(x, (0, 2, 1, 3)).reshape(t, x.shape[-1])

    outs, lses = [], []
    for gi, (window, dil) in enumerate(DIL_CONFIGS):
        assert seq % (dil * DIL_BAND) == 0
        n_steps = window // dil
        if dil == 1:
            o, lse = _dil_attn(proj, gi, proj, n_groups, proj, n_groups + 1, batch, seq, d, n_steps)
        else:
            o, lse = _dil_attn(to_res(gi, dil), 0, to_res(n_groups, dil), 0, to_res(n_groups + 1, dil), 0,
                               batch * dil, seq // dil, d, n_steps)
            o, lse = from_res(o, dil), from_res(lse, dil)
        outs.append(o)
        lses.append(lse)
    return _dil_merge(h, outs, lses, w_out.astype(_BF16))


def _gelu(x):
    return 0.5 * x * (1.0 + lax.erf(x * math.sqrt(0.5)))


def _sgu_kernel(h_ref, g_ref, win_ref, nv_ref, ws_ref, bs_ref, wout_ref, o_ref, u_ref, v_ref, y_ref, *, chunk):
    x = h_ref[...]
    tm = x.shape[0]
    half = wout_ref.shape[0]
    gdim = half // SGU_GROUPS
    xn = _rms_norm(x, g_ref[...], NORM_EPS).astype(_BF16)
    ssq = jnp.zeros((tm, 1), _F32)
    for c in range(0, half, chunk):
        u_ref[:, c:c + chunk] = _gelu(_dot(xn, win_ref[:, c:c + chunk]))
        vv = _gelu(_dot(xn, win_ref[:, half + c:half + c + chunk]))
        v_ref[:, c:c + chunk] = vv
        ssq = ssq + jnp.sum(vv * vv, axis=1, keepdims=True)
    rstd = lax.rsqrt(ssq / half + NORM_EPS)
    row = lax.broadcasted_iota(jnp.int32, (SGU_CHUNK, SGU_CHUNK), 0)
    col = lax.broadcasted_iota(jnp.int32, (SGU_CHUNK, SGU_CHUNK), 1)
    for gi in range(SGU_GROUPS):
        cols = slice(gi * gdim, (gi + 1) * gdim)
        w = jnp.where(col <= row, ws_ref[gi], jnp.zeros((), _BF16))
        bias = bs_ref[:, gi:gi + 1]
        for r in range(0, tm, SGU_CHUNK):
            rows = slice(r, r + SGU_CHUNK)
            vn = (v_ref[rows, cols] * rstd[rows] * nv_ref[:, cols]).astype(_BF16)
            y_ref[rows, cols] = (u_ref[rows, cols] * (_dot(w, vn) + bias)).astype(_BF16)
    o_ref[...] = x + _dot(y_ref[...], wout_ref[...])


def _sgu_mixer(h, g, w_in, norm_v, w_s, b_s, w_out):
    t, d = h.shape
    half = w_out.shape[0]
    tm = min(ROW_TILE, t)
    assert tm % SGU_CHUNK == 0 and (half // SGU_GROUPS) % LANES == 0
    return pl.pallas_call(
        functools.partial(_sgu_kernel, chunk=_col_chunk(half)),
        out_shape=jax.ShapeDtypeStruct((t, d), _F32),
        grid=(t // tm,),
        in_specs=[_row_spec(tm, d), _const_spec((1, d)), _const_spec(w_in.shape), _const_spec((1, half)),
                  _const_spec(w_s.shape), _const_spec((SGU_CHUNK, SGU_GROUPS)), _const_spec(w_out.shape)],
        out_specs=_row_spec(tm, d),
        scratch_shapes=[pltpu.VMEM((tm, half), _F32), pltpu.VMEM((tm, half), _F32), pltpu.VMEM((tm, half), _BF16)],
        compiler_params=_params(1),
        name="sgu",
    )(h, g.reshape(1, d), w_in.astype(_BF16), norm_v.reshape(1, half), w_s.astype(_BF16), jnp.transpose(b_s),
      w_out.astype(_BF16))


def kernel(x, p, norm_ffn1, w_ffn1_in, w_ffn1_out, norm_mix, norm_ffn2, w_ffn2_in, w_ffn2_out, norm_ple, w_ple_gate, b_ple_gate, w_ple_proj, fox_w_in, fox_b_f, fox_w_out, dil_w_in, dil_w_out, diff_w_in, diff_lambda, diff_subln, diff_w_out, sgu_w_in, sgu_norm_v, sgu_w_s, sgu_b_s, sgu_w_out, norm_final):
    batch, seq, d = x.shape
    depth = p.shape[0]
    t = batch * seq
    h = x.reshape(t, d)
    rope = _rope_tables(seq)
    n_mixers = 4
    for i in range(depth):
        kind, j = i % n_mixers, i // n_mixers
        h = _ffn(h, norm_ffn1[i], w_ffn1_in[i].astype(_BF16), w_ffn1_out[i].astype(_BF16))
        if kind == 0:
            h = _fox_mixer(h, norm_mix[i], fox_w_in[j], fox_b_f[j], fox_w_out[j], batch, seq)
        elif kind == 1:
            h = _dil_mixer(h, norm_mix[i], dil_w_in[j], dil_w_out[j], rope, batch, seq)
        elif kind == 2:
            h = _diff_mixer(h, norm_mix[i], diff_w_in[j], diff_lambda[j], diff_subln[j], diff_w_out[j], rope,
                            batch, seq, i)
        else:
            h = _sgu_mixer(h, norm_mix[i], sgu_w_in[j], sgu_norm_v[j], sgu_w_s[j], sgu_b_s[j], sgu_w_out[j])
        h = _ffn(h, norm_ffn2[i], w_ffn2_in[i].astype(_BF16), w_ffn2_out[i].astype(_BF16))
        h = _ple(h, p[i].reshape(t, p.shape[-1]), norm_ple[i], w_ple_gate[i].astype(_BF16), b_ple_gate[i],
                 w_ple_proj[i].astype(_BF16), norm_final, final_norm=(i == depth - 1))
    return h.reshape(batch, seq, d)
```

```python
import functools
import math

import jax
import jax.numpy as jnp
import numpy as np
from jax import lax
from jax.experimental import pallas as pl
from jax.experimental.pallas import tpu as pltpu

_F32 = jnp.float32
_BF16 = jnp.bfloat16

HEAD_DIM = 64
ROPE_THETA = 500000.0
ROT_DIM = HEAD_DIM // 4
NORM_EPS = 1e-6
DIL_CONFIGS = ((128, 1), (512, 4), (2048, 16))
DIL_BAND = 128
DIFF_SUBLN_EPS = 1e-5
SGU_GROUPS = 8
SGU_CHUNK = 128

LANES = 128
VMEM_LIMIT_BYTES = 56 * 1024 * 1024
ROW_TILE = 512
ATTN_BLOCK = 512
COL_CHUNK = 256
LOG2E = math.log2(math.e)
QK_SCALE = HEAD_DIM ** -0.5 * LOG2E


def _params(n_axes):
    return pltpu.CompilerParams(dimension_semantics=("arbitrary",) * n_axes,
                                vmem_limit_bytes=VMEM_LIMIT_BYTES)


def _col_chunk(n, target=COL_CHUNK):
    best = LANES
    for c in range(LANES, target + 1, LANES):
        if n % c == 0:
            best = c
    assert n % best == 0, (n, best)
    return best


def _row_spec(tm, n):
    return pl.BlockSpec((tm, n), lambda i: (i, 0))


def _const_spec(shape):
    return pl.BlockSpec(shape, lambda i: (0,) * len(shape))


def _rms_norm(x, g, eps):
    return x * lax.rsqrt(jnp.mean(x * x, axis=-1, keepdims=True) + eps) * g


def _dot(a, b):
    return jnp.dot(a, b, preferred_element_type=_F32)


def _dot_nt(a, b):
    return lax.dot_general(a, b, (((1,), (1,)), ((), ())), preferred_element_type=_F32)


def _split3(x):
    hi = x.astype(_BF16)
    r1 = x - hi.astype(_F32)
    mid = r1.astype(_BF16)
    lo = (r1 - mid.astype(_F32)).astype(_BF16)
    return hi, mid, lo


def _ffn_kernel(h_ref, g_ref, win_ref, wout_ref, o_ref, hid_ref, *, d_ff, chunk):
    x = h_ref[...]
    xn = _rms_norm(x, g_ref[...], NORM_EPS).astype(_BF16)
    for c in range(0, d_ff, chunk):
        gate = _dot(xn, win_ref[:, c:c + chunk])
        up = _dot(xn, win_ref[:, d_ff + c:d_ff + c + chunk])
        hid_ref[:, c:c + chunk] = (gate * jax.nn.sigmoid(gate) * up).astype(_BF16)
    o_ref[...] = x + 0.5 * _dot(hid_ref[...], wout_ref[...])


def _ffn(h, g, w_in, w_out):
    t, d = h.shape
    d_ff = w_out.shape[0]
    tm = min(ROW_TILE, t)
    return pl.pallas_call(
        functools.partial(_ffn_kernel, d_ff=d_ff, chunk=_col_chunk(d_ff)),
        out_shape=jax.ShapeDtypeStruct((t, d), _F32),
        grid=(t // tm,),
        in_specs=[_row_spec(tm, d), _const_spec((1, d)), _const_spec(w_in.shape), _const_spec(w_out.shape)],
        out_specs=_row_spec(tm, d),
        scratch_shapes=[pltpu.VMEM((tm, d_ff), _BF16)],
        compiler_params=_params(1),
        name="ffn",
    )(h, g.reshape(1, d), w_in, w_out)


def _ple_kernel(h_ref, p_ref, g_ref, wg_ref, b_ref, wp_ref, gf_ref, o_ref, *, final_norm):
    x = h_ref[...]
    xn = _rms_norm(x, g_ref[...], NORM_EPS).astype(_BF16)
    gate = jax.nn.sigmoid(_dot(xn, wg_ref[...]) + b_ref[...])
    y = x + gate * _dot(p_ref[...].astype(_BF16), wp_ref[...])
    if final_norm:
        y = _rms_norm(y, gf_ref[...], NORM_EPS)
    o_ref[...] = y


def _ple(h, p, g, w_gate, b_gate, w_proj, g_final, final_norm):
    t, d = h.shape
    dp = p.shape[1]
    tm = min(ROW_TILE, t)
    return pl.pallas_call(
        functools.partial(_ple_kernel, final_norm=final_norm),
        out_shape=jax.ShapeDtypeStruct((t, d), _F32),
        grid=(t // tm,),
        in_specs=[_row_spec(tm, d), _row_spec(tm, dp), _const_spec((1, d)), _const_spec(w_gate.shape),
                  _const_spec((1, d)), _const_spec(w_proj.shape), _const_spec((1, d))],
        out_specs=_row_spec(tm, d),
        compiler_params=_params(1),
        name="ple",
    )(h, p, g.reshape(1, d), w_gate, b_gate.reshape(1, d), w_proj, g_final.reshape(1, d))


def _rope_tables(s):
    half = ROT_DIM // 2
    inv_freq = ROPE_THETA ** (-jnp.arange(0, ROT_DIM, 2, dtype=_F32) / ROT_DIM)
    ang = jnp.arange(s, dtype=jnp.int32).astype(_F32)[:, None] * inv_freq[None, :]
    cos, sin = jnp.cos(ang), jnp.sin(ang)
    m = jnp.arange(LANES) % HEAD_DIM
    idx = m % half
    first, second = m < half, (m >= half) & (m < ROT_DIM)
    a = jnp.where((first | second)[None, :], cos[:, idx], 1.0)
    b = jnp.where(first[None, :], -sin[:, idx], 0.0)
    c = jnp.where(second[None, :], sin[:, idx], 0.0)
    return a.astype(_F32), b.astype(_F32), c.astype(_F32)


def _proj_kernel(h_ref, g_ref, w_ref, ra_ref, rb_ref, rc_ref, o_ref, *, n_query, n_rope, chunk):
    xn = _rms_norm(h_ref[...], g_ref[...], NORM_EPS).astype(_BF16)
    half = ROT_DIM // 2
    for c in range(0, w_ref.shape[1], chunk):
        y = _dot(xn, w_ref[:, c:c + chunk])
        if c < n_rope:
            for s in range(0, chunk, LANES):
                ys = y[:, s:s + LANES]
                ys = (ys * ra_ref[...] + pltpu.roll(ys, LANES - half, axis=1) * rb_ref[...]
                      + pltpu.roll(ys, half, axis=1) * rc_ref[...])
                if c < n_query:
                    ys = ys * QK_SCALE
                o_ref[:, c + s:c + s + LANES] = ys.astype(o_ref.dtype)
        else:
            o_ref[:, c:c + chunk] = y.astype(o_ref.dtype)


def _proj(h, g, w, rope, n_query, n_rope, seq):
    t, d = h.shape
    n = w.shape[1]
    tm = min(ROW_TILE, seq)
    chunk = _col_chunk(n)
    assert n_rope % chunk == 0 and n_query % chunk == 0 and n_query <= n_rope
    nblk = seq // tm
    rope_spec = pl.BlockSpec((tm, LANES), lambda i: (i % nblk, 0))
    return pl.pallas_call(
        functools.partial(_proj_kernel, n_query=n_query, n_rope=n_rope, chunk=chunk),
        out_shape=jax.ShapeDtypeStruct((t, n), _BF16),
        grid=(t // tm,),
        in_specs=[_row_spec(tm, d), _const_spec((1, d)), _const_spec(w.shape), rope_spec, rope_spec, rope_spec],
        out_specs=_row_spec(tm, n),
        compiler_params=_params(1),
        name="proj",
    )(h, g.reshape(1, d), w, *rope)


def _out_proj_kernel(h_ref, o_ref, w_ref, out_ref):
    out_ref[...] = h_ref[...] + _dot(o_ref[...], w_ref[...])


def _out_proj(h, o, w):
    t, d = h.shape
    tm = min(ROW_TILE, t)
    return pl.pallas_call(
        _out_proj_kernel,
        out_shape=jax.ShapeDtypeStruct((t, d), _F32),
        grid=(t // tm,),
        in_specs=[_row_spec(tm, d), _row_spec(tm, o.shape[1]), _const_spec(w.shape)],
        out_specs=_row_spec(tm, d),
        compiler_params=_params(1),
        name="out_proj",
    )(h, o, w)


def _fox_proj_kernel(h_ref, g_ref, w_ref, wf_ref, bf_ref, qkv_ref, lf_ref, *, n_query, chunk):
    xn = _rms_norm(h_ref[...], g_ref[...], NORM_EPS).astype(_BF16)
    for c in range(0, w_ref.shape[1], chunk):
        y = _dot(xn, w_ref[:, c:c + chunk])
        if c < n_query:
            y = y * QK_SCALE
        qkv_ref[:, c:c + chunk] = y.astype(qkv_ref.dtype)
    z = _dot(xn, wf_ref[...]) + bf_ref[...]
    lf_ref[...] = jnp.minimum(z, 0.0) - jnp.log1p(jnp.exp(-jnp.abs(z)))


def _fox_proj(h, g, w_qkv, w_f, b_f):
    t, d = h.shape
    n = w_qkv.shape[1]
    tm = min(ROW_TILE, t)
    return pl.pallas_call(
        functools.partial(_fox_proj_kernel, n_query=d, chunk=_col_chunk(n)),
        out_shape=(jax.ShapeDtypeStruct((t, n), _BF16), jax.ShapeDtypeStruct((t, LANES), _F32)),
        grid=(t // tm,),
        in_specs=[_row_spec(tm, d), _const_spec((1, d)), _const_spec(w_qkv.shape), _const_spec(w_f.shape),
                  _const_spec((1, LANES))],
        out_specs=(_row_spec(tm, n), _row_spec(tm, LANES)),
        compiler_params=_params(1),
        name="fox_proj",
    )(h, g.reshape(1, d), w_qkv, w_f, b_f)


def _bias_placement(d):
    pairs = d // LANES
    w = np.zeros((3, 3 * LANES, d), np.float32)
    ones = np.zeros((3, 1, d), np.float32)
    for hp in range(pairs):
        base = hp * LANES
        for p in range(3):
            w[0, p * LANES + 2 * hp, base + p] = 1.0
            w[1, p * LANES + 2 * hp + 1, base + 6 + p] = 1.0
            w[2, p * LANES + 2 * hp, base + 3 + p] = -1.0
            w[2, p * LANES + 2 * hp + 1, base + 9 + p] = -1.0
            ones[0, 0, base + 3 + p] = 1.0
            ones[1, 0, base + 9 + p] = 1.0
            ones[2, 0, base + p] = 1.0
            ones[2, 0, base + 6 + p] = 1.0
    return jnp.asarray(w, _BF16), jnp.asarray(ones, _F32)


def _cumsum_kernel(x_ref, w_ref, ones_ref, qea_ref, qeb_ref, ke_ref, carry_ref):
    @pl.when(pl.program_id(1) == 0)
    def _():
        carry_ref[...] = jnp.zeros_like(carry_ref)

    x = x_ref[...]
    ts = x.shape[0]
    row = lax.broadcasted_iota(jnp.int32, (ts, ts), 0)
    col = lax.broadcasted_iota(jnp.int32, (ts, ts), 1)
    tri = jnp.where(col <= row, 1.0, 0.0).astype(_BF16)
    hi, mid, lo = _split3(x)
    c = _dot(tri, hi) + _dot(tri, mid) + _dot(tri, lo) + carry_ref[0:1, :]
    carry_ref[0:1, :] = c[ts - 1:ts, :]
    pieces = jnp.concatenate(_split3(c * LOG2E), axis=1)
    for n, out_ref in enumerate((qea_ref, qeb_ref, ke_ref)):
        out_ref[...] = (_dot(pieces, w_ref[n]) + ones_ref[n]).astype(out_ref.dtype)


def _cumsum_bias(x, batch, seq, d, ts):
    ns = seq // ts
    w, ones = _bias_placement(d)
    row_spec = pl.BlockSpec((ts, d), lambda b, s: (b * ns + s, 0))
    out = jax.ShapeDtypeStruct((batch * seq, d), _BF16)
    return pl.pallas_call(
        _cumsum_kernel,
        out_shape=(out, out, out),
        grid=(batch, ns),
        in_specs=[pl.BlockSpec((ts, LANES), lambda b, s: (b * ns + s, 0)),
                  pl.BlockSpec(w.shape, lambda b, s: (0, 0, 0)), pl.BlockSpec(ones.shape, lambda b, s: (0, 0, 0))],
        out_specs=(row_spec, row_spec, row_spec),
        scratch_shapes=[pltpu.VMEM((8, LANES), _F32)],
        compiler_params=_params(2),
        name="cumsum",
    )(x, w, ones)


def _flash_items(nq, blk):
    items = []
    for i in range(nq):
        for j in range(i // 2):
            items.append((i, 2 * j * blk, 2 * blk, False))
        if i % 2 == 1:
            items.append((i, (i - 1) * blk, 2 * blk, True))
        else:
            items.append((i, i * blk, blk, True))
    return items


def _causal_flash(query_maps, k_refs, value_maps, finish, nq, blk, scratch):
    s_bufs, m_ref, acc_ref = scratch[:2], scratch[2], scratch[3]
    items = _flash_items(nq, blk)

    def produce(item, s_ref):
        i, start, width, _ = item
        parts = [r[start:start + width, :] for r in k_refs]
        kb = parts[0] if len(parts) == 1 else jnp.concatenate(parts, axis=1)
        for e, qm in enumerate(query_maps(i)):
            s_ref[e, :, :width] = _dot_nt(qm, kb)

    def consume(item, s_ref, first):
        i, start, width, masked = item
        vbs = value_maps(start, width)
        for e in range(2):
            s = s_ref[e, :, :width]
            if masked:
                row = lax.broadcasted_iota(jnp.int32, (blk, width), 0)
                col = lax.broadcasted_iota(jnp.int32, (blk, width), 1)
                s = jnp.where(col - row <= i * blk - start, s, -jnp.inf)
            m_new = jnp.max(s, axis=1, keepdims=True)
            if not first:
                m_old = m_ref[e]
                m_new = jnp.maximum(m_old, m_new)
            pv = _dot(jnp.exp2((s - m_new).astype(_BF16)), vbs[e])
            acc_ref[e] = pv if first else acc_ref[e] * jnp.exp2(m_old - m_new) + pv
            m_ref[e] = m_new

    produce(items[0], s_bufs[0])
    for t, item in enumerate(items):
        if t + 1 < len(items):
            produce(items[t + 1], s_bufs[(t + 1) % 2])
        consume(item, s_bufs[t % 2], first=(t == 0 or items[t - 1][0] != item[0]))
        if t + 1 == len(items) or items[t + 1][0] != item[0]:
            finish(item[0], acc_ref[0], acc_ref[1])


def _flash_scratch(blk, n_out):
    return [pltpu.VMEM((2, blk, 2 * blk), _F32), pltpu.VMEM((2, blk, 2 * blk), _F32),
            pltpu.VMEM((2, blk, 1), _F32), pltpu.VMEM((2, blk, n_out), _F32)]


def _split_maps(q, lo):
    zero = jnp.zeros_like(q)
    return jnp.where(lo, q, zero), jnp.where(lo, zero, q)


def _fox_attn_kernel(q_ref, qea_ref, qeb_ref, k_ref, ke_ref, v_ref, o_ref, *scratch, blk):
    lo = lax.broadcasted_iota(jnp.int32, (blk, LANES), 1) < HEAD_DIM

    def query_maps(i):
        rows = slice(i * blk, (i + 1) * blk)
        qa, qb = _split_maps(q_ref[rows, :], lo)
        return jnp.concatenate([qa, qea_ref[rows, :]], axis=1), jnp.concatenate([qb, qeb_ref[rows, :]], axis=1)

    def value_maps(start, width):
        vb = v_ref[start:start + width, :]
        keep = lax.broadcasted_iota(jnp.int32, (width, LANES), 1) < HEAD_DIM
        one = jnp.ones_like(vb)
        return jnp.where(keep, vb, one), jnp.where(keep, one, vb)

    def finish(i, acc_a, acc_b):
        out_a = acc_a / pltpu.roll(acc_a, HEAD_DIM, axis=1)
        out_b = acc_b / pltpu.roll(acc_b, HEAD_DIM, axis=1)
        o_ref[i * blk:(i + 1) * blk, :] = jnp.where(lo, out_a, out_b).astype(o_ref.dtype)

    _causal_flash(query_maps, (k_ref, ke_ref), value_maps, finish, q_ref.shape[0] // blk, blk, scratch)


def _fox_attn(qkv, qea, qeb, ke, batch, seq, d):
    blk = min(ATTN_BLOCK, seq)
    pairs = d // LANES
    spec = lambda col0: pl.BlockSpec((seq, LANES), lambda b, h: (b, col0 + h))
    return pl.pallas_call(
        functools.partial(_fox_attn_kernel, blk=blk),
        out_shape=jax.ShapeDtypeStruct((batch * seq, d), _BF16),
        grid=(batch, pairs),
        in_specs=[spec(0), spec(0), spec(0), spec(pairs), spec(0), spec(2 * pairs)],
        out_specs=spec(0),
        scratch_shapes=_flash_scratch(blk, LANES),
        compiler_params=_params(2),
        name="fox_attn",
    )(qkv, qea, qeb, qkv, ke, qkv)


def _fox_mixer(h, g, w_in, b_f, w_out, batch, seq):
    t, d = h.shape
    n_heads = d // HEAD_DIM
    w_qkv = w_in[:, :3 * d].astype(_BF16)
    w_f = jnp.pad(w_in[:, 3 * d:], ((0, 0), (0, LANES - n_heads))).astype(_BF16)
    b_pad = jnp.pad(b_f, (0, LANES - n_heads)).reshape(1, LANES)
    qkv, log_f = _fox_proj(h, g, w_qkv, w_f, b_pad)
    qea, qeb, ke = _cumsum_bias(log_f, batch, seq, d, min(ATTN_BLOCK, seq))
    o = _fox_attn(qkv, qea, qeb, ke, batch, seq, d)
    return _out_proj(h, o, w_out.astype(_BF16))


def _diff_attn_kernel(q_ref, k_ref, v_ref, lam_ref, g_ref, o_ref, *scratch, blk, lam_init):
    lo = lax.broadcasted_iota(jnp.int32, (blk, LANES), 1) < HEAD_DIM
    lp = lam_ref[...]
    lam = (jnp.exp(jnp.sum(lp[0:1] * lp[1:2], axis=1, keepdims=True))
           - jnp.exp(jnp.sum(lp[2:3] * lp[3:4], axis=1, keepdims=True)) + lam_init)

    def query_maps(i):
        return _split_maps(q_ref[i * blk:(i + 1) * blk, :], lo)

    def value_maps(start, width):
        vb = v_ref[start:start + width, :]
        vb = jnp.concatenate([vb, jnp.ones_like(vb)], axis=1)
        return vb, vb

    def finish(i, acc_1, acc_2):
        o = acc_1[:, :LANES] / acc_1[:, LANES:] - lam * (acc_2[:, :LANES] / acc_2[:, LANES:])
        o = _rms_norm(o, g_ref[...], DIFF_SUBLN_EPS) * (1.0 - lam_init)
        o_ref[i * blk:(i + 1) * blk, :] = o.astype(o_ref.dtype)

    _causal_flash(query_maps, (k_ref,), value_maps, finish, q_ref.shape[0] // blk, blk, scratch)


def _diff_attn(proj, lam_params, subln_g, batch, seq, d, lam_init):
    blk = min(ATTN_BLOCK, seq)
    heads = d // LANES
    spec = lambda col0: pl.BlockSpec((seq, LANES), lambda b, h: (b, col0 + h))
    return pl.pallas_call(
        functools.partial(_diff_attn_kernel, blk=blk, lam_init=lam_init),
        out_shape=jax.ShapeDtypeStruct((batch * seq, d), _BF16),
        grid=(batch, heads),
        in_specs=[spec(0), spec(heads), spec(2 * heads),
                  pl.BlockSpec(lam_params.shape, lambda b, h: (0, 0)), pl.BlockSpec((1, LANES), lambda b, h: (0, 0))],
        out_specs=spec(0),
        scratch_shapes=_flash_scratch(blk, 2 * LANES),
        compiler_params=_params(2),
        name="diff_attn",
    )(proj, proj, proj, lam_params, subln_g.reshape(1, LANES))


def _diff_mixer(h, g, w_in, lam_params, subln_g, w_out, rope, batch, seq, layer_idx):
    t, d = h.shape
    lam_init = 0.8 - 0.6 * math.exp(-0.3 * layer_idx)
    proj = _proj(h, g, w_in.astype(_BF16), rope, d, 2 * d, seq)
    o = _diff_attn(proj, lam_params, subln_g, batch, seq, d, lam_init)
    return _out_proj(h, o, w_out.astype(_BF16))


def _dil_attn_kernel(q_ref, kp_ref, kc_ref, vp_ref, vc_ref, o_ref, lse_ref, kcat_ref, vcat_ref, s_ref, p_ref, *,
                     n_steps, n_pairs):
    band = DIL_BAND
    n_heads = 2 * n_pairs
    kcat_ref[0:band, :] = kp_ref[...]
    kcat_ref[band:, :] = kc_ref[...]
    vcat_ref[0:band, :] = vp_ref[...]
    vcat_ref[band:, :] = vc_ref[...]
    row = lax.broadcasted_iota(jnp.int32, (band, 2 * band), 0)
    col = lax.broadcasted_iota(jnp.int32, (band, 2 * band), 1)
    in_band = jnp.logical_and(col <= row + band, col >= row + band - n_steps)
    first_valid = jnp.logical_and(in_band, jnp.logical_or(col >= band, pl.program_id(1) > 0))
    lane = lax.broadcasted_iota(jnp.int32, (band, LANES), 1)
    lo = lane < HEAD_DIM
    lo_v = lax.broadcasted_iota(jnp.int32, (2 * band, LANES), 1) < HEAD_DIM
    for r in range(q_ref.shape[0] // band):
        rows = slice(r * band, (r + 1) * band)
        win = slice(r * band, (r + 2) * band)
        valid = first_valid if r == 0 else in_band
        for hp in range(n_pairs):
            sl = slice(hp * LANES, (hp + 1) * LANES)
            q = q_ref[rows, sl]
            zero = jnp.zeros_like(q)
            kw = kcat_ref[win, sl]
            for e, qm in enumerate((jnp.where(lo, q, zero), jnp.where(lo, zero, q))):
                s_ref[(2 * hp + e) * band:(2 * hp + e + 1) * band, :] = jnp.where(valid, _dot_nt(qm, kw), -jnp.inf)
        s = s_ref[...]
        m = jnp.max(s, axis=1, keepdims=True)
        p_ref[...] = jnp.exp2((s - m).astype(_BF16))
        lse_tile = jnp.zeros((band, LANES), _F32)
        for hp in range(n_pairs):
            sl = slice(hp * LANES, (hp + 1) * LANES)
            vw = vcat_ref[win, sl]
            one = jnp.ones_like(vw)
            ha, hb = slice((2 * hp) * band, (2 * hp + 1) * band), slice((2 * hp + 1) * band, (2 * hp + 2) * band)
            pv_a = _dot(p_ref[ha, :], jnp.where(lo_v, vw, one))
            pv_b = _dot(p_ref[hb, :], jnp.where(lo_v, one, vw))
            l_a, l_b = pltpu.roll(pv_a, HEAD_DIM, axis=1), pltpu.roll(pv_b, HEAD_DIM, axis=1)
            o_ref[rows, sl] = jnp.where(lo, pv_a / l_a, pv_b / l_b).astype(o_ref.dtype)
            lse_tile = jnp.where(lane == 2 * hp, (m[ha] + jnp.log2(l_a)) * (1.0 / LOG2E), lse_tile)
            lse_tile = jnp.where(lane == 2 * hp + 1, (m[hb] + jnp.log2(pv_b)) * (1.0 / LOG2E), lse_tile)
        lse_ref[rows, :] = lse_tile


def _dil_attn(q, q_col, k, k_col, v, v_col, n_seq, n_sub, d, n_steps):
    rb = min(ROW_TILE, n_sub)
    nb = n_sub // rb
    bands = rb // DIL_BAND
    n_heads = d // HEAD_DIM
    cur = lambda col: (lambda x, n: (x * nb + n, col))
    prev = lambda col: (lambda x, n: ((x * nb + n) * bands - jnp.where(n > 0, 1, 0), col))
    return pl.pallas_call(
        functools.partial(_dil_attn_kernel, n_steps=n_steps, n_pairs=d // LANES),
        out_shape=(jax.ShapeDtypeStruct((n_seq * n_sub, d), _BF16),
                   jax.ShapeDtypeStruct((n_seq * n_sub, LANES), _F32)),
        grid=(n_seq, nb),
        in_specs=[pl.BlockSpec((rb, d), cur(q_col)),
                  pl.BlockSpec((DIL_BAND, d), prev(k_col)), pl.BlockSpec((rb, d), cur(k_col)),
                  pl.BlockSpec((DIL_BAND, d), prev(v_col)), pl.BlockSpec((rb, d), cur(v_col))],
        out_specs=(pl.BlockSpec((rb, d), cur(0)), pl.BlockSpec((rb, LANES), cur(0))),
        scratch_shapes=[pltpu.VMEM((rb + DIL_BAND, d), _BF16), pltpu.VMEM((rb + DIL_BAND, d), _BF16),
                        pltpu.VMEM((n_heads * DIL_BAND, 2 * DIL_BAND), _F32),
                        pltpu.VMEM((n_heads * DIL_BAND, 2 * DIL_BAND), _BF16)],
        compiler_params=_params(2),
        name="dil_attn",
    )(q, k, k, v, v)


def _dil_merge_kernel(h_ref, o1_ref, o2_ref, o3_ref, l1_ref, l2_ref, l3_ref, e_ref, w_ref, out_ref):
    lses = (l1_ref[...], l2_ref[...], l3_ref[...])
    m = jnp.maximum(jnp.maximum(lses[0], lses[1]), lses[2])
    es = tuple(jnp.exp(l - m) for l in lses)
    den = es[0] + es[1] + es[2]
    acc = jnp.zeros(h_ref.shape, _F32)
    for e, o_ref in zip(es, (o1_ref, o2_ref, o3_ref)):
        hi, mid, lo = _split3(e / den)
        wide = _dot(hi, e_ref[...]) + _dot(mid, e_ref[...]) + _dot(lo, e_ref[...])
        acc = acc + wide * o_ref[...].astype(_F32)
    out_ref[...] = h_ref[...] + _dot(acc.astype(_BF16), w_ref[...])


def _dil_merge(h, outs, lses, w_out):
    t, d = h.shape
    tm = min(ROW_TILE, t)
    n_heads = d // HEAD_DIM
    expand = (jnp.arange(LANES)[:, None] == (jnp.arange(d) // HEAD_DIM)[None, :]) & (jnp.arange(LANES) < n_heads)[:, None]
    return pl.pallas_call(
        _dil_merge_kernel,
        out_shape=jax.ShapeDtypeStruct((t, d), _F32),
        grid=(t // tm,),
        in_specs=[_row_spec(tm, d)] * 4 + [_row_spec(tm, LANES)] * 3 + [_const_spec((LANES, d)), _const_spec(w_out.shape)],
        out_specs=_row_spec(tm, d),
        compiler_params=_params(1),
        name="dil_merge",
    )(h, *outs, *lses, expand.astype(_BF16), w_out)


def _dil_mixer(h, g, w_in, w_out, rope, batch, seq):
    t, d = h.shape
    n_groups = len(DIL_CONFIGS)
    proj = _proj(h, g, w_in.astype(_BF16), rope, n_groups * d, (n_groups + 1) * d, seq)

    def to_res(col, dil):
        x = proj[:, col * d:(col + 1) * d].reshape(batch, seq // dil, dil, d)
        return jnp.transpose(x, (0, 2, 1, 3)).reshape(t, d)

    def from_res(x, dil):
        x = x.reshape(batch, dil, seq // dil, x.shape[-1])
        return jnp.transpose(x, (0, 2, 1, 3)).reshape(t, x.shape[-1])

    outs, lses = [], []
    for gi, (window, dil) in enumerate(DIL_CONFIGS):
        assert seq % (dil * DIL_BAND) == 0
        n_steps = window // dil
        if dil == 1:
            o, lse = _dil_attn(proj, gi, proj, n_groups, proj, n_groups + 1, batch, seq, d, n_steps)
        else:
            o, lse = _dil_attn(to_res(gi, dil), 0, to_res(n_groups, dil), 0, to_res(n_groups + 1, dil), 0,
                               batch * dil, seq // dil, d, n_steps)
            o, lse = from_res(o, dil), from_res(lse, dil)
        outs.append(o)
        lses.append(lse)
    return _dil_merge(h, outs, lses, w_out.astype(_BF16))


def _gelu(x):
    return 0.5 * x * (1.0 + lax.erf(x * math.sqrt(0.5)))


def _sgu_kernel(h_ref, g_ref, win_ref, nv_ref, ws_ref, bs_ref, wout_ref, o_ref, u_ref, v_ref, y_ref, *, chunk):
    x = h_ref[...]
    tm = x.shape[0]
    half = wout_ref.shape[0]
    gdim = half // SGU_GROUPS
    xn = _rms_norm(x, g_ref[...], NORM_EPS).astype(_BF16)
    ssq = jnp.zeros((tm, 1), _F32)
    for c in range(0, half, chunk):
        u_ref[:, c:c + chunk] = _gelu(_dot(xn, win_ref[:, c:c + chunk]))
        vv = _gelu(_dot(xn, win_ref[:, half + c:half + c + chunk]))
        v_ref[:, c:c + chunk] = vv
        ssq = ssq + jnp.sum(vv * vv, axis=1, keepdims=True)
    rstd = lax.rsqrt(ssq / half + NORM_EPS)
    row = lax.broadcasted_iota(jnp.int32, (SGU_CHUNK, SGU_CHUNK), 0)
    col = lax.broadcasted_iota(jnp.int32, (SGU_CHUNK, SGU_CHUNK), 1)
    for gi in range(SGU_GROUPS):
        cols = slice(gi * gdim, (gi + 1) * gdim)
        w = jnp.where(col <= row, ws_ref[gi], jnp.zeros((), _BF16))
        bias = bs_ref[:, gi:gi + 1]
        for r in range(0, tm, SGU_CHUNK):
            rows = slice(r, r + SGU_CHUNK)
            vn = (v_ref[rows, cols] * rstd[rows] * nv_ref[:, cols]).astype(_BF16)
            y_ref[rows, cols] = (u_ref[rows, cols] * (_dot(w, vn) + bias)).astype(_BF16)
    o_ref[...] = x + _dot(y_ref[...], wout_ref[...])


def _sgu_mixer(h, g, w_in, norm_v, w_s, b_s, w_out):
    t, d = h.shape
    half = w_out.shape[0]
    tm = min(ROW_TILE, t)
    assert tm % SGU_CHUNK == 0 and (half // SGU_GROUPS) % LANES == 0
    return pl.pallas_call(
        functools.partial(_sgu_kernel, chunk=_col_chunk(half)),
        out_shape=jax.ShapeDtypeStruct((t, d), _F32),
        grid=(t // tm,),
        in_specs=[_row_spec(tm, d), _const_spec((1, d)), _const_spec(w_in.shape), _const_spec((1, half)),
                  _const_spec(w_s.shape), _const_spec((SGU_CHUNK, SGU_GROUPS)), _const_spec(w_out.shape)],
        out_specs=_row_spec(tm, d),
        scratch_shapes=[pltpu.VMEM((tm, half), _F32), pltpu.VMEM((tm, half), _F32), pltpu.VMEM((tm, half), _BF16)],
        compiler_params=_params(1),
        name="sgu",
    )(h, g.reshape(1, d), w_in.astype(_BF16), norm_v.reshape(1, half), w_s.astype(_BF16), jnp.transpose(b_s),
      w_out.astype(_BF16))


def kernel(x, p, norm_ffn1, w_ffn1_in, w_ffn1_out, norm_mix, norm_ffn2, w_ffn2_in, w_ffn2_out, norm_ple, w_ple_gate, b_ple_gate, w_ple_proj, fox_w_in, fox_b_f, fox_w_out, dil_w_in, dil_w_out, diff_w_in, diff_lambda, diff_subln, diff_w_out, sgu_w_in, sgu_norm_v, sgu_w_s, sgu_b_s, sgu_w_out, norm_final):
    batch, seq, d = x.shape
    depth = p.shape[0]
    t = batch * seq
    h = x.reshape(t, d)
    rope = _rope_tables(seq)
    n_mixers = 4
    for i in range(depth):
        kind, j = i % n_mixers, i // n_mixers
        h = _ffn(h, norm_ffn1[i], w_ffn1_in[i].astype(_BF16), w_ffn1_out[i].astype(_BF16))
        if kind == 0:
            h = _fox_mixer(h, norm_mix[i], fox_w_in[j], fox_b_f[j], fox_w_out[j], batch, seq)
        elif kind == 1:
            h = _dil_mixer(h, norm_mix[i], dil_w_in[j], dil_w_out[j], rope, batch, seq)
        elif kind == 2:
            h = _diff_mixer(h, norm_mix[i], diff_w_in[j], diff_lambda[j], diff_subln[j], diff_w_out[j], rope,
                            batch, seq, i)
        else:
            h = _sgu_mixer(h, norm_mix[i], sgu_w_in[j], sgu_norm_v[j], sgu_w_s[j], sgu_b_s[j], sgu_w_out[j])
        h = _ffn(h, norm_ffn2[i], w_ffn2_in[i].astype(_BF16), w_ffn2_out[i].astype(_BF16))
        h = _ple(h, p[i].reshape(t, p.shape[-1]), norm_ple[i], w_ple_gate[i].astype(_BF16), b_ple_gate[i],
                 w_ple_proj[i].astype(_BF16), norm_final, final_norm=(i == depth - 1))
    return h.reshape(batch, seq, d)
```

```python
import functools
import math

import jax
import jax.numpy as jnp
import numpy as np
from jax import lax
from jax.experimental import pallas as pl
from jax.experimental.pallas import tpu as pltpu

_F32 = jnp.float32
_BF16 = jnp.bfloat16

HEAD_DIM = 64
ROPE_THETA = 500000.0
ROT_DIM = HEAD_DIM // 4
NORM_EPS = 1e-6
DIL_CONFIGS = ((128, 1), (512, 4), (2048, 16))
DIL_BAND = 128
DIFF_SUBLN_EPS = 1e-5
SGU_GROUPS = 8
SGU_CHUNK = 128

LANES = 128
VMEM_LIMIT_BYTES = 56 * 1024 * 1024
ROW_TILE = 512
ATTN_BLOCK = 512
COL_CHUNK = 256
LOG2E = math.log2(math.e)
QK_SCALE = HEAD_DIM ** -0.5 * LOG2E


def _params(n_axes):
    return pltpu.CompilerParams(dimension_semantics=("arbitrary",) * n_axes,
                                vmem_limit_bytes=VMEM_LIMIT_BYTES)


def _col_chunk(n, target=COL_CHUNK):
    best = LANES
    for c in range(LANES, target + 1, LANES):
        if n % c == 0:
            best = c
    assert n % best == 0, (n, best)
    return best


def _row_spec(tm, n):
    return pl.BlockSpec((tm, n), lambda i: (i, 0))


def _const_spec(shape):
    return pl.BlockSpec(shape, lambda i: (0,) * len(shape))


def _rms_norm(x, g, eps):
    return x * lax.rsqrt(jnp.mean(x * x, axis=-1, keepdims=True) + eps) * g


def _dot(a, b):
    return jnp.dot(a, b, preferred_element_type=_F32)


def _dot_nt(a, b):
    return lax.dot_general(a, b, (((1,), (1,)), ((), ())), preferred_element_type=_F32)


def _split3(x):
    hi = x.astype(_BF16)
    r1 = x - hi.astype(_F32)
    mid = r1.astype(_BF16)
    lo = (r1 - mid.astype(_F32)).astype(_BF16)
    return hi, mid, lo


def _ffn_kernel(h_ref, g_ref, win_ref, wout_ref, o_ref, hid_ref, *, d_ff, chunk):
    x = h_ref[...]
    xn = _rms_norm(x, g_ref[...], NORM_EPS).astype(_BF16)
    for c in range(0, d_ff, chunk):
        gate = _dot(xn, win_ref[:, c:c + chunk])
        up = _dot(xn, win_ref[:, d_ff + c:d_ff + c + chunk])
        hid_ref[:, c:c + chunk] = (gate * jax.nn.sigmoid(gate) * up).astype(_BF16)
    o_ref[...] = x + 0.5 * _dot(hid_ref[...], wout_ref[...])


def _ffn(h, g, w_in, w_out):
    t, d = h.shape
    d_ff = w_out.shape[0]
    tm = min(ROW_TILE, t)
    return pl.pallas_call(
        functools.partial(_ffn_kernel, d_ff=d_ff, chunk=_col_chunk(d_ff)),
        out_shape=jax.ShapeDtypeStruct((t, d), _F32),
        grid=(t // tm,),
        in_specs=[_row_spec(tm, d), _const_spec((1, d)), _const_spec(w_in.shape), _const_spec(w_out.shape)],
        out_specs=_row_spec(tm, d),
        scratch_shapes=[pltpu.VMEM((tm, d_ff), _BF16)],
        compiler_params=_params(1),
        name="ffn",
    )(h, g.reshape(1, d), w_in, w_out)


def _post_kernel(*refs, has_attn, final_norm, d_ff, chunk):
    h_ref = refs[0]
    o_ref, wo_ref = refs[1:3] if has_attn else (None, None)
    (gf2_ref, win_ref, wout_ref, p_ref, gp_ref, wg_ref, bg_ref, wp_ref, gfin_ref,
     out_ref, hid_ref) = refs[3 if has_attn else 1:]
    x = h_ref[...]
    if has_attn:
        x = x + _dot(o_ref[...], wo_ref[...])
    xn = _rms_norm(x, gf2_ref[...], NORM_EPS).astype(_BF16)
    for c in range(0, d_ff, chunk):
        gate = _dot(xn, win_ref[:, c:c + chunk])
        up = _dot(xn, win_ref[:, d_ff + c:d_ff + c + chunk])
        hid_ref[:, c:c + chunk] = (gate * jax.nn.sigmoid(gate) * up).astype(_BF16)
    x = x + 0.5 * _dot(hid_ref[...], wout_ref[...])
    xn = _rms_norm(x, gp_ref[...], NORM_EPS).astype(_BF16)
    gate = jax.nn.sigmoid(_dot(xn, wg_ref[...]) + bg_ref[...])
    x = x + gate * _dot(p_ref[...].astype(_BF16), wp_ref[...])
    if final_norm:
        x = _rms_norm(x, gfin_ref[...], NORM_EPS)
    out_ref[...] = x


def _post_mixer(h, attn, g_ffn, w_in, w_out, p, g_ple, w_gate, b_gate, w_proj, g_final, final_norm):
    t, d = h.shape
    d_ff = w_out.shape[0]
    dp = p.shape[1]
    tm = min(ROW_TILE, t)
    args, specs = [h], [_row_spec(tm, d)]
    if attn is not None:
        args += list(attn)
        specs += [_row_spec(tm, attn[0].shape[1]), _const_spec(attn[1].shape)]
    args += [g_ffn.reshape(1, d), w_in, w_out, p, g_ple.reshape(1, d), w_gate, b_gate.reshape(1, d), w_proj,
             g_final.reshape(1, d)]
    specs += [_const_spec((1, d)), _const_spec(w_in.shape), _const_spec(w_out.shape), _row_spec(tm, dp),
              _const_spec((1, d)), _const_spec(w_gate.shape), _const_spec((1, d)), _const_spec(w_proj.shape),
              _const_spec((1, d))]
    return pl.pallas_call(
        functools.partial(_post_kernel, has_attn=attn is not None, final_norm=final_norm, d_ff=d_ff,
                          chunk=_col_chunk(d_ff)),
        out_shape=jax.ShapeDtypeStruct((t, d), _F32),
        grid=(t // tm,),
        in_specs=specs,
        out_specs=_row_spec(tm, d),
        scratch_shapes=[pltpu.VMEM((tm, d_ff), _BF16)],
        compiler_params=_params(1),
        name="post_mixer",
    )(*args)


def _rope_tables(s):
    half = ROT_DIM // 2
    inv_freq = ROPE_THETA ** (-jnp.arange(0, ROT_DIM, 2, dtype=_F32) / ROT_DIM)
    ang = jnp.arange(s, dtype=jnp.int32).astype(_F32)[:, None] * inv_freq[None, :]
    cos, sin = jnp.cos(ang), jnp.sin(ang)
    m = jnp.arange(LANES) % HEAD_DIM
    idx = m % half
    first, second = m < half, (m >= half) & (m < ROT_DIM)
    a = jnp.where((first | second)[None, :], cos[:, idx], 1.0)
    b = jnp.where(first[None, :], -sin[:, idx], 0.0)
    c = jnp.where(second[None, :], sin[:, idx], 0.0)
    return a.astype(_F32), b.astype(_F32), c.astype(_F32)


def _proj_kernel(h_ref, g_ref, w_ref, ra_ref, rb_ref, rc_ref, o_ref, *, n_query, n_rope, chunk):
    xn = _rms_norm(h_ref[...], g_ref[...], NORM_EPS).astype(_BF16)
    half = ROT_DIM // 2
    for c in range(0, w_ref.shape[1], chunk):
        y = _dot(xn, w_ref[:, c:c + chunk])
        if c < n_rope:
            for s in range(0, chunk, LANES):
                ys = y[:, s:s + LANES]
                ys = (ys * ra_ref[...] + pltpu.roll(ys, LANES - half, axis=1) * rb_ref[...]
                      + pltpu.roll(ys, half, axis=1) * rc_ref[...])
                if c < n_query:
                    ys = ys * QK_SCALE
                o_ref[:, c + s:c + s + LANES] = ys.astype(o_ref.dtype)
        else:
            o_ref[:, c:c + chunk] = y.astype(o_ref.dtype)


def _proj(h, g, w, rope, n_query, n_rope, seq):
    t, d = h.shape
    n = w.shape[1]
    tm = min(ROW_TILE, seq)
    chunk = _col_chunk(n)
    assert n_rope % chunk == 0 and n_query % chunk == 0 and n_query <= n_rope
    nblk = seq // tm
    rope_spec = pl.BlockSpec((tm, LANES), lambda i: (i % nblk, 0))
    return pl.pallas_call(
        functools.partial(_proj_kernel, n_query=n_query, n_rope=n_rope, chunk=chunk),
        out_shape=jax.ShapeDtypeStruct((t, n), _BF16),
        grid=(t // tm,),
        in_specs=[_row_spec(tm, d), _const_spec((1, d)), _const_spec(w.shape), rope_spec, rope_spec, rope_spec],
        out_specs=_row_spec(tm, n),
        compiler_params=_params(1),
        name="proj",
    )(h, g.reshape(1, d), w, *rope)


def _fox_proj_kernel(h_ref, g_ref, w_ref, wf_ref, bf_ref, qkv_ref, lf_ref, *, n_query, chunk):
    xn = _rms_norm(h_ref[...], g_ref[...], NORM_EPS).astype(_BF16)
    for c in range(0, w_ref.shape[1], chunk):
        y = _dot(xn, w_ref[:, c:c + chunk])
        if c < n_query:
            y = y * QK_SCALE
        qkv_ref[:, c:c + chunk] = y.astype(qkv_ref.dtype)
    z = _dot(xn, wf_ref[...]) + bf_ref[...]
    lf_ref[...] = jnp.minimum(z, 0.0) - jnp.log1p(jnp.exp(-jnp.abs(z)))


def _fox_proj(h, g, w_qkv, w_f, b_f):
    t, d = h.shape
    n = w_qkv.shape[1]
    tm = min(ROW_TILE, t)
    return pl.pallas_call(
        functools.partial(_fox_proj_kernel, n_query=d, chunk=_col_chunk(n)),
        out_shape=(jax.ShapeDtypeStruct((t, n), _BF16), jax.ShapeDtypeStruct((t, LANES), _F32)),
        grid=(t // tm,),
        in_specs=[_row_spec(tm, d), _const_spec((1, d)), _const_spec(w_qkv.shape), _const_spec(w_f.shape),
                  _const_spec((1, LANES))],
        out_specs=(_row_spec(tm, n), _row_spec(tm, LANES)),
        compiler_params=_params(1),
        name="fox_proj",
    )(h, g.reshape(1, d), w_qkv, w_f, b_f)


def _bias_placement(d):
    pairs = d // LANES
    w = np.zeros((3, 3 * LANES, d), np.float32)
    ones = np.zeros((3, 1, d), np.float32)
    for hp in range(pairs):
        base = hp * LANES
        for p in range(3):
            w[0, p * LANES + 2 * hp, base + p] = 1.0
            w[1, p * LANES + 2 * hp + 1, base + 6 + p] = 1.0
            w[2, p * LANES + 2 * hp, base + 3 + p] = -1.0
            w[2, p * LANES + 2 * hp + 1, base + 9 + p] = -1.0
            ones[0, 0, base + 3 + p] = 1.0
            ones[1, 0, base + 9 + p] = 1.0
            ones[2, 0, base + p] = 1.0
            ones[2, 0, base + 6 + p] = 1.0
    return jnp.asarray(w, _BF16), jnp.asarray(ones, _F32)


def _cumsum_kernel(x_ref, w_ref, ones_ref, qea_ref, qeb_ref, ke_ref, carry_ref):
    @pl.when(pl.program_id(1) == 0)
    def _():
        carry_ref[...] = jnp.zeros_like(carry_ref)

    x = x_ref[...]
    ts = x.shape[0]
    row = lax.broadcasted_iota(jnp.int32, (ts, ts), 0)
    col = lax.broadcasted_iota(jnp.int32, (ts, ts), 1)
    tri = jnp.where(col <= row, 1.0, 0.0).astype(_BF16)
    hi, mid, lo = _split3(x)
    c = _dot(tri, hi) + _dot(tri, mid) + _dot(tri, lo) + carry_ref[0:1, :]
    carry_ref[0:1, :] = c[ts - 1:ts, :]
    pieces = jnp.concatenate(_split3(c * LOG2E), axis=1)
    for n, out_ref in enumerate((qea_ref, qeb_ref, ke_ref)):
        out_ref[...] = (_dot(pieces, w_ref[n]) + ones_ref[n]).astype(out_ref.dtype)


def _cumsum_bias(x, batch, seq, d, ts):
    ns = seq // ts
    w, ones = _bias_placement(d)
    row_spec = pl.BlockSpec((ts, d), lambda b, s: (b * ns + s, 0))
    out = jax.ShapeDtypeStruct((batch * seq, d), _BF16)
    return pl.pallas_call(
        _cumsum_kernel,
        out_shape=(out, out, out),
        grid=(batch, ns),
        in_specs=[pl.BlockSpec((ts, LANES), lambda b, s: (b * ns + s, 0)),
                  pl.BlockSpec(w.shape, lambda b, s: (0, 0, 0)), pl.BlockSpec(ones.shape, lambda b, s: (0, 0, 0))],
        out_specs=(row_spec, row_spec, row_spec),
        scratch_shapes=[pltpu.VMEM((8, LANES), _F32)],
        compiler_params=_params(2),
        name="cumsum",
    )(x, w, ones)


def _flash_items(nq, blk):
    items = []
    for i in range(nq):
        for j in range(i // 2):
            items.append((i, 2 * j * blk, 2 * blk, False))
        if i % 2 == 1:
            items.append((i, (i - 1) * blk, 2 * blk, True))
        else:
            items.append((i, i * blk, blk, True))
    return items


def _causal_flash(query_maps, k_refs, value_maps, finish, nq, blk, scratch):
    s_bufs, m_ref, acc_ref = scratch[:2], scratch[2], scratch[3]
    items = _flash_items(nq, blk)

    def produce(item, s_ref):
        i, start, width, _ = item
        parts = [r[start:start + width, :] for r in k_refs]
        kb = parts[0] if len(parts) == 1 else jnp.concatenate(parts, axis=1)
        for e, qm in enumerate(query_maps(i)):
            s_ref[e, :, :width] = _dot_nt(qm, kb)

    def consume(item, s_ref, first):
        i, start, width, masked = item
        vbs = value_maps(start, width)
        for e in range(2):
            s = s_ref[e, :, :width]
            if masked:
                row = lax.broadcasted_iota(jnp.int32, (blk, width), 0)
                col = lax.broadcasted_iota(jnp.int32, (blk, width), 1)
                s = jnp.where(col - row <= i * blk - start, s, -jnp.inf)
            m_new = jnp.max(s, axis=1, keepdims=True)
            if not first:
                m_old = m_ref[e]
                m_new = jnp.maximum(m_old, m_new)
            pv = _dot(jnp.exp2((s - m_new).astype(_BF16)), vbs[e])
            acc_ref[e] = pv if first else acc_ref[e] * jnp.exp2(m_old - m_new) + pv
            m_ref[e] = m_new

    produce(items[0], s_bufs[0])
    for t, item in enumerate(items):
        if t + 1 < len(items):
            produce(items[t + 1], s_bufs[(t + 1) % 2])
        consume(item, s_bufs[t % 2], first=(t == 0 or items[t - 1][0] != item[0]))
        if t + 1 == len(items) or items[t + 1][0] != item[0]:
            finish(item[0], acc_ref[0], acc_ref[1])


def _flash_scratch(blk, n_out):
    return [pltpu.VMEM((2, blk, 2 * blk), _F32), pltpu.VMEM((2, blk, 2 * blk), _F32),
            pltpu.VMEM((2, blk, 1), _F32), pltpu.VMEM((2, blk, n_out), _F32)]


def _split_maps(q, lo):
    zero = jnp.zeros_like(q)
    return jnp.where(lo, q, zero), jnp.where(lo, zero, q)


def _fox_attn_kernel(q_ref, qea_ref, qeb_ref, k_ref, ke_ref, v_ref, o_ref, *scratch, blk):
    lo = lax.broadcasted_iota(jnp.int32, (blk, LANES), 1) < HEAD_DIM

    def query_maps(i):
        rows = slice(i * blk, (i + 1) * blk)
        qa, qb = _split_maps(q_ref[rows, :], lo)
        return jnp.concatenate([qa, qea_ref[rows, :]], axis=1), jnp.concatenate([qb, qeb_ref[rows, :]], axis=1)

    def value_maps(start, width):
        vb = v_ref[start:start + width, :]
        keep = lax.broadcasted_iota(jnp.int32, (width, LANES), 1) < HEAD_DIM
        one = jnp.ones_like(vb)
        return jnp.where(keep, vb, one), jnp.where(keep, one, vb)

    def finish(i, acc_a, acc_b):
        out_a = acc_a / pltpu.roll(acc_a, HEAD_DIM, axis=1)
        out_b = acc_b / pltpu.roll(acc_b, HEAD_DIM, axis=1)
        o_ref[i * blk:(i + 1) * blk, :] = jnp.where(lo, out_a, out_b).astype(o_ref.dtype)

    _causal_flash(query_maps, (k_ref, ke_ref), value_maps, finish, q_ref.shape[0] // blk, blk, scratch)


def _fox_attn(qkv, qea, qeb, ke, batch, seq, d):
    blk = min(ATTN_BLOCK, seq)
    pairs = d // LANES
    spec = lambda col0: pl.BlockSpec((seq, LANES), lambda b, h: (b, col0 + h))
    return pl.pallas_call(
        functools.partial(_fox_attn_kernel, blk=blk),
        out_shape=jax.ShapeDtypeStruct((batch * seq, d), _BF16),
        grid=(batch, pairs),
        in_specs=[spec(0), spec(0), spec(0), spec(pairs), spec(0), spec(2 * pairs)],
        out_specs=spec(0),
        scratch_shapes=_flash_scratch(blk, LANES),
        compiler_params=_params(2),
        name="fox_attn",
    )(qkv, qea, qeb, qkv, ke, qkv)


def _fox_mixer(h, g, w_in, b_f, w_out, batch, seq):
    t, d = h.shape
    n_heads = d // HEAD_DIM
    w_qkv = w_in[:, :3 * d].astype(_BF16)
    w_f = jnp.pad(w_in[:, 3 * d:], ((0, 0), (0, LANES - n_heads))).astype(_BF16)
    b_pad = jnp.pad(b_f, (0, LANES - n_heads)).reshape(1, LANES)
    qkv, log_f = _fox_proj(h, g, w_qkv, w_f, b_pad)
    qea, qeb, ke = _cumsum_bias(log_f, batch, seq, d, min(ATTN_BLOCK, seq))
    return _fox_attn(qkv, qea, qeb, ke, batch, seq, d), w_out.astype(_BF16)


def _diff_attn_kernel(q_ref, k_ref, v_ref, lam_ref, g_ref, o_ref, *scratch, blk, lam_init):
    lo = lax.broadcasted_iota(jnp.int32, (blk, LANES), 1) < HEAD_DIM
    lp = lam_ref[...]
    lam = (jnp.exp(jnp.sum(lp[0:1] * lp[1:2], axis=1, keepdims=True))
           - jnp.exp(jnp.sum(lp[2:3] * lp[3:4], axis=1, keepdims=True)) + lam_init)

    def query_maps(i):
        return _split_maps(q_ref[i * blk:(i + 1) * blk, :], lo)

    def value_maps(start, width):
        vb = v_ref[start:start + width, :]
        vb = jnp.concatenate([vb, jnp.ones_like(vb)], axis=1)
        return vb, vb

    def finish(i, acc_1, acc_2):
        o = acc_1[:, :LANES] / acc_1[:, LANES:] - lam * (acc_2[:, :LANES] / acc_2[:, LANES:])
        o = _rms_norm(o, g_ref[...], DIFF_SUBLN_EPS) * (1.0 - lam_init)
        o_ref[i * blk:(i + 1) * blk, :] = o.astype(o_ref.dtype)

    _causal_flash(query_maps, (k_ref,), value_maps, finish, q_ref.shape[0] // blk, blk, scratch)


def _diff_attn(proj, lam_params, subln_g, batch, seq, d, lam_init):
    blk = min(ATTN_BLOCK, seq)
    heads = d // LANES
    spec = lambda col0: pl.BlockSpec((seq, LANES), lambda b, h: (b, col0 + h))
    return pl.pallas_call(
        functools.partial(_diff_attn_kernel, blk=blk, lam_init=lam_init),
        out_shape=jax.ShapeDtypeStruct((batch * seq, d), _BF16),
        grid=(batch, heads),
        in_specs=[spec(0), spec(heads), spec(2 * heads),
                  pl.BlockSpec(lam_params.shape, lambda b, h: (0, 0)), pl.BlockSpec((1, LANES), lambda b, h: (0, 0))],
        out_specs=spec(0),
        scratch_shapes=_flash_scratch(blk, 2 * LANES),
        compiler_params=_params(2),
        name="diff_attn",
    )(proj, proj, proj, lam_params, subln_g.reshape(1, LANES))


def _diff_mixer(h, g, w_in, lam_params, subln_g, w_out, rope, batch, seq, layer_idx):
    t, d = h.shape
    lam_init = 0.8 - 0.6 * math.exp(-0.3 * layer_idx)
    proj = _proj(h, g, w_in.astype(_BF16), rope, d, 2 * d, seq)
    return _diff_attn(proj, lam_params, subln_g, batch, seq, d, lam_init), w_out.astype(_BF16)


def _dil_attn_kernel(q_ref, kp_ref, kc_ref, vp_ref, vc_ref, o_ref, lse_ref, kcat_ref, vcat_ref, s_ref, p_ref, *,
                     n_steps, n_pairs):
    band = DIL_BAND
    n_heads = 2 * n_pairs
    kcat_ref[0:band, :] = kp_ref[...]
    kcat_ref[band:, :] = kc_ref[...]
    vcat_ref[0:band, :] = vp_ref[...]
    vcat_ref[band:, :] = vc_ref[...]
    row = lax.broadcasted_iota(jnp.int32, (band, 2 * band), 0)
    col = lax.broadcasted_iota(jnp.int32, (band, 2 * band), 1)
    in_band = jnp.logical_and(col <= row + band, col >= row + band - n_steps)
    first_valid = jnp.logical_and(in_band, jnp.logical_or(col >= band, pl.program_id(1) > 0))
    lane = lax.broadcasted_iota(jnp.int32, (band, LANES), 1)
    lo = lane < HEAD_DIM
    lo_v = lax.broadcasted_iota(jnp.int32, (2 * band, LANES), 1) < HEAD_DIM
    for r in range(q_ref.shape[0] // band):
        rows = slice(r * band, (r + 1) * band)
        win = slice(r * band, (r + 2) * band)
        valid = first_valid if r == 0 else in_band
        for hp in range(n_pairs):
            sl = slice(hp * LANES, (hp + 1) * LANES)
            q = q_ref[rows, sl]
            zero = jnp.zeros_like(q)
            kw = kcat_ref[win, sl]
            for e, qm in enumerate((jnp.where(lo, q, zero), jnp.where(lo, zero, q))):
                s_ref[(2 * hp + e) * band:(2 * hp + e + 1) * band, :] = jnp.where(valid, _dot_nt(qm, kw), -jnp.inf)
        s = s_ref[...]
        m = jnp.max(s, axis=1, keepdims=True)
        p_ref[...] = jnp.exp2((s - m).astype(_BF16))
        lse_tile = jnp.zeros((band, LANES), _F32)
        for hp in range(n_pairs):
            sl = slice(hp * LANES, (hp + 1) * LANES)
            vw = vcat_ref[win, sl]
            one = jnp.ones_like(vw)
            ha, hb = slice((2 * hp) * band, (2 * hp + 1) * band), slice((2 * hp + 1) * band, (2 * hp + 2) * band)
            pv_a = _dot(p_ref[ha, :], jnp.where(lo_v, vw, one))
            pv_b = _dot(p_ref[hb, :], jnp.where(lo_v, one, vw))
            l_a, l_b = pltpu.roll(pv_a, HEAD_DIM, axis=1), pltpu.roll(pv_b, HEAD_DIM, axis=1)
            o_ref[rows, sl] = jnp.where(lo, pv_a / l_a, pv_b / l_b).astype(o_ref.dtype)
            lse_tile = jnp.where(lane == 2 * hp, (m[ha] + jnp.log2(l_a)) * (1.0 / LOG2E), lse_tile)
            lse_tile = jnp.where(lane == 2 * hp + 1, (m[hb] + jnp.log2(pv_b)) * (1.0 / LOG2E), lse_tile)
        lse_ref[rows, :] = lse_tile


def _dil_attn(q, q_col, k, k_col, v, v_col, n_seq, n_sub, d, n_steps):
    rb = min(ROW_TILE, n_sub)
    nb = n_sub // rb
    bands = rb // DIL_BAND
    n_heads = d // HEAD_DIM
    cur = lambda col: (lambda x, n: (x * nb + n, col))
    prev = lambda col: (lambda x, n: ((x * nb + n) * bands - jnp.where(n > 0, 1, 0), col))
    return pl.pallas_call(
        functools.partial(_dil_attn_kernel, n_steps=n_steps, n_pairs=d // LANES),
        out_shape=(jax.ShapeDtypeStruct((n_seq * n_sub, d), _BF16),
                   jax.ShapeDtypeStruct((n_seq * n_sub, LANES), _F32)),
        grid=(n_seq, nb),
        in_specs=[pl.BlockSpec((rb, d), cur(q_col)),
                  pl.BlockSpec((DIL_BAND, d), prev(k_col)), pl.BlockSpec((rb, d), cur(k_col)),
                  pl.BlockSpec((DIL_BAND, d), prev(v_col)), pl.BlockSpec((rb, d), cur(v_col))],
        out_specs=(pl.BlockSpec((rb, d), cur(0)), pl.BlockSpec((rb, LANES), cur(0))),
        scratch_shapes=[pltpu.VMEM((rb + DIL_BAND, d), _BF16), pltpu.VMEM((rb + DIL_BAND, d), _BF16),
                        pltpu.VMEM((n_heads * DIL_BAND, 2 * DIL_BAND), _F32),
                        pltpu.VMEM((n_heads * DIL_BAND, 2 * DIL_BAND), _BF16)],
        compiler_params=_params(2),
        name="dil_attn",
    )(q, k, k, v, v)


def _dil_proj_kernel(h_ref, g_ref, w_ref, ra_ref, rb_ref, rc_ref, *refs, d_model, dilations):
    out_refs, ybuf = refs[:len(dilations)], refs[len(dilations)]
    n_groups = len(dilations)
    tm = h_ref.shape[0]
    xn = _rms_norm(h_ref[...], g_ref[...], NORM_EPS).astype(_BF16)
    half = ROT_DIM // 2
    n_slab = 0
    for c in range(0, w_ref.shape[1], LANES * 2):
        y = _dot(xn, w_ref[:, c:c + 2 * LANES])
        for s in range(0, 2 * LANES, LANES):
            src, col = divmod(c + s, d_model)
            ys = y[:, s:s + LANES]
            if src <= n_groups:
                ys = (ys * ra_ref[...] + pltpu.roll(ys, LANES - half, axis=1) * rb_ref[...]
                      + pltpu.roll(ys, half, axis=1) * rc_ref[...])
            if src < n_groups:
                ys = ys * QK_SCALE
            dests = [(src, 0)] if src < n_groups else [(g, src - n_groups + 1) for g in range(n_groups)]
            slab = ybuf.at[n_slab % ybuf.shape[0]]
            n_slab += 1
            if any(dilations[g] > 1 for g, _ in dests):
                slab[...] = ys
            for g, blk in dests:
                dil = dilations[g]
                lanes = slice(blk * d_model + col, blk * d_model + col + LANES)
                if dil == 1:
                    out_refs[g][0, :, lanes] = ys.astype(_BF16)
                else:
                    for r in range(dil):
                        out_refs[g][r, :, lanes] = slab[pl.ds(r, tm // dil, stride=dil), :].astype(_BF16)


def _dil_proj(h, g, w, rope, batch, seq, dilations):
    t, d = h.shape
    tm = min(ROW_TILE, seq)
    nblk = seq // tm
    rope_spec = pl.BlockSpec((tm, LANES), lambda i: (i % nblk, 0))
    out_shapes = tuple(jax.ShapeDtypeStruct((batch, dil, seq // dil, 3 * d), _BF16) for dil in dilations)
    out_specs = tuple(pl.BlockSpec((None, dil, tm // dil, 3 * d), lambda i: (i // nblk, 0, i % nblk, 0))
                      for dil in dilations)
    return pl.pallas_call(
        functools.partial(_dil_proj_kernel, d_model=d, dilations=dilations),
        out_shape=out_shapes,
        grid=(t // tm,),
        in_specs=[_row_spec(tm, d), _const_spec((1, d)), _const_spec(w.shape), rope_spec, rope_spec, rope_spec],
        out_specs=out_specs,
        scratch_shapes=[pltpu.VMEM((4, tm, LANES), _F32)],
        compiler_params=_params(1),
        name="dil_proj",
    )(h, g.reshape(1, d), w, *rope)


def _dil_merge_kernel(h_ref, *refs, dilations):
    n = len(dilations)
    o_refs, l_refs = refs[:n], refs[n:2 * n]
    e_ref, w_ref, out_ref, obuf, lbuf, acc_ref = refs[2 * n:]
    tm, d = h_ref.shape

    def by_position(src_ref, dil, lanes, buf):
        if dil == 1:
            return src_ref[0, :, lanes].astype(_F32)
        for r in range(dil):
            buf[pl.ds(r, tm // dil, stride=dil), :] = src_ref[r, :, lanes].astype(_F32)
        return buf[...]

    lses = [by_position(l_refs[gi], dil, slice(0, LANES), lbuf.at[gi]) for gi, dil in enumerate(dilations)]
    m = functools.reduce(jnp.maximum, lses)
    es = [jnp.exp(l - m) for l in lses]
    den = functools.reduce(lambda a, b: a + b, es)
    wides = []
    for e in es:
        hi, mid, lo = _split3(e / den)
        wides.append(_dot(hi, e_ref[...]) + _dot(mid, e_ref[...]) + _dot(lo, e_ref[...]))
    for c in range(0, d, LANES):
        lanes = slice(c, c + LANES)
        acc = jnp.zeros((tm, LANES), _F32)
        for gi, dil in enumerate(dilations):
            acc = acc + wides[gi][:, lanes] * by_position(o_refs[gi], dil, lanes, obuf.at[(c // LANES) % 2, gi])
        acc_ref[:, lanes] = acc.astype(_BF16)
    out_ref[...] = h_ref[...] + _dot(acc_ref[...], w_ref[...])


def _dil_merge(h, outs, lses, w_out, batch, seq, dilations):
    t, d = h.shape
    tm = min(ROW_TILE, seq)
    nblk = seq // tm
    n_heads = d // HEAD_DIM
    expand = (jnp.arange(LANES)[:, None] == (jnp.arange(d) // HEAD_DIM)[None, :]) & (jnp.arange(LANES) < n_heads)[:, None]
    res_spec = lambda dil, n: pl.BlockSpec((None, dil, tm // dil, n), lambda i: (i // nblk, 0, i % nblk, 0))
    n = len(dilations)
    return pl.pallas_call(
        functools.partial(_dil_merge_kernel, dilations=dilations),
        out_shape=jax.ShapeDtypeStruct((t, d), _F32),
        grid=(t // tm,),
        in_specs=([_row_spec(tm, d)] + [res_spec(dil, d) for dil in dilations] + [res_spec(dil, LANES) for dil in dilations]
                  + [_const_spec((LANES, d)), _const_spec(w_out.shape)]),
        out_specs=_row_spec(tm, d),
        scratch_shapes=[pltpu.VMEM((2, n, tm, LANES), _F32), pltpu.VMEM((n, tm, LANES), _F32),
                        pltpu.VMEM((tm, d), _BF16)],
        compiler_params=_params(1),
        name="dil_merge",
    )(h, *outs, *lses, expand.astype(_BF16), w_out)


def _dil_mixer(h, g, w_in, w_out, rope, batch, seq):
    t, d = h.shape
    dilations = tuple(dil for _, dil in DIL_CONFIGS)
    projs = _dil_proj(h, g, w_in.astype(_BF16), rope, batch, seq, dilations)
    outs, lses = [], []
    for (window, dil), proj in zip(DIL_CONFIGS, projs):
        assert seq % (dil * DIL_BAND) == 0
        proj = proj.reshape(t, 3 * d)
        o, lse = _dil_attn(proj, 0, proj, 1, proj, 2, batch * dil, seq // dil, d, window // dil)
        outs.append(o.reshape(batch, dil, seq // dil, d))
        lses.append(lse.reshape(batch, dil, seq // dil, LANES))
    return _dil_merge(h, outs, lses, w_out.astype(_BF16), batch, seq, dilations)


def _gelu(x):
    return 0.5 * x * (1.0 + lax.erf(x * math.sqrt(0.5)))


def _sgu_kernel(h_ref, g_ref, win_ref, nv_ref, ws_ref, bs_ref, wout_ref, o_ref, u_ref, v_ref, y_ref, *, chunk):
    x = h_ref[...]
    tm = x.shape[0]
    half = wout_ref.shape[0]
    gdim = half // SGU_GROUPS
    xn = _rms_norm(x, g_ref[...], NORM_EPS).astype(_BF16)
    ssq = jnp.zeros((tm, 1), _F32)
    for c in range(0, half, chunk):
        u_ref[:, c:c + chunk] = _gelu(_dot(xn, win_ref[:, c:c + chunk]))
        vv = _gelu(_dot(xn, win_ref[:, half + c:half + c + chunk]))
        v_ref[:, c:c + chunk] = vv
        ssq = ssq + jnp.sum(vv * vv, axis=1, keepdims=True)
    rstd = lax.rsqrt(ssq / half + NORM_EPS)
    row = lax.broadcasted_iota(jnp.int32, (SGU_CHUNK, SGU_CHUNK), 0)
    col = lax.broadcasted_iota(jnp.int32, (SGU_CHUNK, SGU_CHUNK), 1)
    for gi in range(SGU_GROUPS):
        cols = slice(gi * gdim, (gi + 1) * gdim)
        w = jnp.where(col <= row, ws_ref[gi], jnp.zeros((), _BF16))
        bias = bs_ref[:, gi:gi + 1]
        for r in range(0, tm, SGU_CHUNK):
            rows = slice(r, r + SGU_CHUNK)
            vn = (v_ref[rows, cols] * rstd[rows] * nv_ref[:, cols]).astype(_BF16)
            y_ref[rows, cols] = (u_ref[rows, cols] * (_dot(w, vn) + bias)).astype(_BF16)
    o_ref[...] = x + _dot(y_ref[...], wout_ref[...])


def _sgu_mixer(h, g, w_in, norm_v, w_s, b_s, w_out):
    t, d = h.shape
    half = w_out.shape[0]
    tm = min(ROW_TILE, t)
    assert tm % SGU_CHUNK == 0 and (half // SGU_GROUPS) % LANES == 0
    return pl.pallas_call(
        functools.partial(_sgu_kernel, chunk=_col_chunk(half)),
        out_shape=jax.ShapeDtypeStruct((t, d), _F32),
        grid=(t // tm,),
        in_specs=[_row_spec(tm, d), _const_spec((1, d)), _const_spec(w_in.shape), _const_spec((1, half)),
                  _const_spec(w_s.shape), _const_spec((SGU_CHUNK, SGU_GROUPS)), _const_spec(w_out.shape)],
        out_specs=_row_spec(tm, d),
        scratch_shapes=[pltpu.VMEM((tm, half), _F32), pltpu.VMEM((tm, half), _F32), pltpu.VMEM((tm, half), _BF16)],
        compiler_params=_params(1),
        name="sgu",
    )(h, g.reshape(1, d), w_in.astype(_BF16), norm_v.reshape(1, half), w_s.astype(_BF16), jnp.transpose(b_s),
      w_out.astype(_BF16))


def kernel(x, p, norm_ffn1, w_ffn1_in, w_ffn1_out, norm_mix, norm_ffn2, w_ffn2_in, w_ffn2_out, norm_ple, w_ple_gate, b_ple_gate, w_ple_proj, fox_w_in, fox_b_f, fox_w_out, dil_w_in, dil_w_out, diff_w_in, diff_lambda, diff_subln, diff_w_out, sgu_w_in, sgu_norm_v, sgu_w_s, sgu_b_s, sgu_w_out, norm_final):
    batch, seq, d = x.shape
    depth = p.shape[0]
    t = batch * seq
    h = x.reshape(t, d)
    rope = _rope_tables(seq)
    n_mixers = 4
    for i in range(depth):
        kind, j = i % n_mixers, i // n_mixers
        h = _ffn(h, norm_ffn1[i], w_ffn1_in[i].astype(_BF16), w_ffn1_out[i].astype(_BF16))
        attn = None
        if kind == 0:
            attn = _fox_mixer(h, norm_mix[i], fox_w_in[j], fox_b_f[j], fox_w_out[j], batch, seq)
        elif kind == 1:
            h = _dil_mixer(h, norm_mix[i], dil_w_in[j], dil_w_out[j], rope, batch, seq)
        elif kind == 2:
            attn = _diff_mixer(h, norm_mix[i], diff_w_in[j], diff_lambda[j], diff_subln[j], diff_w_out[j], rope,
                               batch, seq, i)
        else:
            h = _sgu_mixer(h, norm_mix[i], sgu_w_in[j], sgu_norm_v[j], sgu_w_s[j], sgu_b_s[j], sgu_w_out[j])
        h = _post_mixer(h, attn, norm_ffn2[i], w_ffn2_in[i].astype(_BF16), w_ffn2_out[i].astype(_BF16),
                        p[i].reshape(t, p.shape[-1]), norm_ple[i], w_ple_gate[i].astype(_BF16), b_ple_gate[i],
                        w_ple_proj[i].astype(_BF16), norm_final, final_norm=(i == depth - 1))
    return h.reshape(batch, seq, d)
```

```python
import functools
import math

import jax
import jax.numpy as jnp
import numpy as np
from jax import lax
from jax.experimental import pallas as pl
from jax.experimental.pallas import tpu as pltpu

_F32 = jnp.float32
_BF16 = jnp.bfloat16

HEAD_DIM = 64
ROPE_THETA = 500000.0
ROT_DIM = HEAD_DIM // 4
NORM_EPS = 1e-6
DIL_CONFIGS = ((128, 1), (512, 4), (2048, 16))
DIL_BAND = 128
DIFF_SUBLN_EPS = 1e-5
SGU_GROUPS = 8
SGU_CHUNK = 128

LANES = 128
VMEM_LIMIT_BYTES = 56 * 1024 * 1024
ROW_TILE = 512
ATTN_BLOCK = 512
COL_CHUNK = 256
LOG2E = math.log2(math.e)
QK_SCALE = HEAD_DIM ** -0.5 * LOG2E


def _params(n_axes):
    return pltpu.CompilerParams(dimension_semantics=("arbitrary",) * n_axes,
                                vmem_limit_bytes=VMEM_LIMIT_BYTES)


def _col_chunk(n, target=COL_CHUNK):
    best = LANES
    for c in range(LANES, target + 1, LANES):
        if n % c == 0:
            best = c
    assert n % best == 0, (n, best)
    return best


def _row_spec(tm, n):
    return pl.BlockSpec((tm, n), lambda i: (i, 0))


def _const_spec(shape):
    return pl.BlockSpec(shape, lambda i: (0,) * len(shape))


def _rms_norm(x, g, eps):
    return x * lax.rsqrt(jnp.mean(x * x, axis=-1, keepdims=True) + eps) * g


def _dot(a, b):
    return jnp.dot(a, b, preferred_element_type=_F32)


def _dot_nt(a, b):
    return lax.dot_general(a, b, (((1,), (1,)), ((), ())), preferred_element_type=_F32)


def _split3(x):
    hi = x.astype(_BF16)
    r1 = x - hi.astype(_F32)
    mid = r1.astype(_BF16)
    lo = (r1 - mid.astype(_F32)).astype(_BF16)
    return hi, mid, lo


def _ffn_kernel(h_ref, g_ref, win_ref, wout_ref, o_ref, hid_ref, *, d_ff, chunk):
    x = h_ref[...]
    xn = _rms_norm(x, g_ref[...], NORM_EPS).astype(_BF16)
    for c in range(0, d_ff, chunk):
        gate = _dot(xn, win_ref[:, c:c + chunk])
        up = _dot(xn, win_ref[:, d_ff + c:d_ff + c + chunk])
        hid_ref[:, c:c + chunk] = (gate * jax.nn.sigmoid(gate) * up).astype(_BF16)
    o_ref[...] = x + 0.5 * _dot(hid_ref[...], wout_ref[...])


def _ffn(h, g, w_in, w_out):
    t, d = h.shape
    d_ff = w_out.shape[0]
    tm = min(ROW_TILE, t)
    return pl.pallas_call(
        functools.partial(_ffn_kernel, d_ff=d_ff, chunk=_col_chunk(d_ff)),
        out_shape=jax.ShapeDtypeStruct((t, d), _F32),
        grid=(t // tm,),
        in_specs=[_row_spec(tm, d), _const_spec((1, d)), _const_spec(w_in.shape), _const_spec(w_out.shape)],
        out_specs=_row_spec(tm, d),
        scratch_shapes=[pltpu.VMEM((tm, d_ff), _BF16)],
        compiler_params=_params(1),
        name="ffn",
    )(h, g.reshape(1, d), w_in, w_out)


def _post_kernel(*refs, has_attn, final_norm, d_ff, chunk):
    h_ref = refs[0]
    o_ref, wo_ref = refs[1:3] if has_attn else (None, None)
    (gf2_ref, win_ref, wout_ref, p_ref, gp_ref, wg_ref, bg_ref, wp_ref, gfin_ref,
     out_ref, hid_ref) = refs[3 if has_attn else 1:]
    x = h_ref[...]
    if has_attn:
        x = x + _dot(o_ref[...], wo_ref[...])
    xn = _rms_norm(x, gf2_ref[...], NORM_EPS).astype(_BF16)
    for c in range(0, d_ff, chunk):
        gate = _dot(xn, win_ref[:, c:c + chunk])
        up = _dot(xn, win_ref[:, d_ff + c:d_ff + c + chunk])
        hid_ref[:, c:c + chunk] = (gate * jax.nn.sigmoid(gate) * up).astype(_BF16)
    x = x + 0.5 * _dot(hid_ref[...], wout_ref[...])
    xn = _rms_norm(x, gp_ref[...], NORM_EPS).astype(_BF16)
    gate = jax.nn.sigmoid(_dot(xn, wg_ref[...]) + bg_ref[...])
    x = x + gate * _dot(p_ref[...].astype(_BF16), wp_ref[...])
    if final_norm:
        x = _rms_norm(x, gfin_ref[...], NORM_EPS)
    out_ref[...] = x


def _post_mixer(h, attn, g_ffn, w_in, w_out, p, layer, g_ple, w_gate, b_gate, w_proj, g_final, final_norm):
    t, d = h.shape
    d_ff = w_out.shape[0]
    dp = p.shape[2]
    tm = min(ROW_TILE, t)
    args, specs = [h], [_row_spec(tm, d)]
    if attn is not None:
        args += list(attn)
        specs += [_row_spec(tm, attn[0].shape[1]), _const_spec(attn[1].shape)]
    args += [g_ffn.reshape(1, d), w_in, w_out, p, g_ple.reshape(1, d), w_gate, b_gate.reshape(1, d), w_proj,
             g_final.reshape(1, d)]
    specs += [_const_spec((1, d)), _const_spec(w_in.shape), _const_spec(w_out.shape),
              pl.BlockSpec((None, tm, dp), lambda i: (layer, i, 0)),
              _const_spec((1, d)), _const_spec(w_gate.shape), _const_spec((1, d)), _const_spec(w_proj.shape),
              _const_spec((1, d))]
    return pl.pallas_call(
        functools.partial(_post_kernel, has_attn=attn is not None, final_norm=final_norm, d_ff=d_ff,
                          chunk=_col_chunk(d_ff)),
        out_shape=jax.ShapeDtypeStruct((t, d), _F32),
        grid=(t // tm,),
        in_specs=specs,
        out_specs=_row_spec(tm, d),
        scratch_shapes=[pltpu.VMEM((tm, d_ff), _BF16)],
        compiler_params=_params(1),
        name="post_mixer",
    )(*args)


def _rope_tables(s):
    half = ROT_DIM // 2
    inv_freq = ROPE_THETA ** (-jnp.arange(0, ROT_DIM, 2, dtype=_F32) / ROT_DIM)
    ang = jnp.arange(s, dtype=jnp.int32).astype(_F32)[:, None] * inv_freq[None, :]
    cos, sin = jnp.cos(ang), jnp.sin(ang)
    m = jnp.arange(LANES) % HEAD_DIM
    idx = m % half
    first, second = m < half, (m >= half) & (m < ROT_DIM)
    a = jnp.where((first | second)[None, :], cos[:, idx], 1.0)
    b = jnp.where(first[None, :], -sin[:, idx], 0.0)
    c = jnp.where(second[None, :], sin[:, idx], 0.0)
    return a.astype(_F32), b.astype(_F32), c.astype(_F32)


def _proj_kernel(h_ref, g_ref, w_ref, ra_ref, rb_ref, rc_ref, o_ref, *, n_query, n_rope, chunk):
    xn = _rms_norm(h_ref[...], g_ref[...], NORM_EPS).astype(_BF16)
    half = ROT_DIM // 2
    for c in range(0, w_ref.shape[1], chunk):
        y = _dot(xn, w_ref[:, c:c + chunk])
        if c < n_rope:
            for s in range(0, chunk, LANES):
                ys = y[:, s:s + LANES]
                ys = (ys * ra_ref[...] + pltpu.roll(ys, LANES - half, axis=1) * rb_ref[...]
                      + pltpu.roll(ys, half, axis=1) * rc_ref[...])
                if c < n_query:
                    ys = ys * QK_SCALE
                o_ref[:, c + s:c + s + LANES] = ys.astype(o_ref.dtype)
        else:
            o_ref[:, c:c + chunk] = y.astype(o_ref.dtype)


def _proj(h, g, w, rope, n_query, n_rope, seq):
    t, d = h.shape
    n = w.shape[1]
    tm = min(ROW_TILE, seq)
    chunk = _col_chunk(n)
    assert n_rope % chunk == 0 and n_query % chunk == 0 and n_query <= n_rope
    nblk = seq // tm
    rope_spec = pl.BlockSpec((tm, LANES), lambda i: (i % nblk, 0))
    return pl.pallas_call(
        functools.partial(_proj_kernel, n_query=n_query, n_rope=n_rope, chunk=chunk),
        out_shape=jax.ShapeDtypeStruct((t, n), _BF16),
        grid=(t // tm,),
        in_specs=[_row_spec(tm, d), _const_spec((1, d)), _const_spec(w.shape), rope_spec, rope_spec, rope_spec],
        out_specs=_row_spec(tm, n),
        compiler_params=_params(1),
        name="proj",
    )(h, g.reshape(1, d), w, *rope)


def _fox_proj_kernel(h_ref, g_ref, w_ref, wf_ref, bf_ref, qkv_ref, lf_ref, *, n_query, chunk):
    xn = _rms_norm(h_ref[...], g_ref[...], NORM_EPS).astype(_BF16)
    for c in range(0, w_ref.shape[1], chunk):
        y = _dot(xn, w_ref[:, c:c + chunk])
        if c < n_query:
            y = y * QK_SCALE
        qkv_ref[:, c:c + chunk] = y.astype(qkv_ref.dtype)
    z = _dot(xn, wf_ref[...]) + bf_ref[...]
    lf_ref[...] = jnp.minimum(z, 0.0) - jnp.log1p(jnp.exp(-jnp.abs(z)))


def _fox_proj(h, g, w_qkv, w_f, b_f):
    t, d = h.shape
    n = w_qkv.shape[1]
    tm = min(ROW_TILE, t)
    return pl.pallas_call(
        functools.partial(_fox_proj_kernel, n_query=d, chunk=_col_chunk(n)),
        out_shape=(jax.ShapeDtypeStruct((t, n), _BF16), jax.ShapeDtypeStruct((t, LANES), _F32)),
        grid=(t // tm,),
        in_specs=[_row_spec(tm, d), _const_spec((1, d)), _const_spec(w_qkv.shape), _const_spec(w_f.shape),
                  _const_spec((1, LANES))],
        out_specs=(_row_spec(tm, n), _row_spec(tm, LANES)),
        compiler_params=_params(1),
        name="fox_proj",
    )(h, g.reshape(1, d), w_qkv, w_f, b_f)


def _bias_placement(d):
    pairs = d // LANES
    w = np.zeros((3, 3 * LANES, d), np.float32)
    ones = np.zeros((3, 1, d), np.float32)
    for hp in range(pairs):
        base = hp * LANES
        for p in range(3):
            w[0, p * LANES + 2 * hp, base + p] = 1.0
            w[1, p * LANES + 2 * hp + 1, base + 6 + p] = 1.0
            w[2, p * LANES + 2 * hp, base + 3 + p] = -1.0
            w[2, p * LANES + 2 * hp + 1, base + 9 + p] = -1.0
            ones[0, 0, base + 3 + p] = 1.0
            ones[1, 0, base + 9 + p] = 1.0
            ones[2, 0, base + p] = 1.0
            ones[2, 0, base + 6 + p] = 1.0
    return jnp.asarray(w, _BF16), jnp.asarray(ones, _F32)


def _cumsum_kernel(x_ref, w_ref, ones_ref, qea_ref, qeb_ref, ke_ref, carry_ref):
    @pl.when(pl.program_id(1) == 0)
    def _():
        carry_ref[...] = jnp.zeros_like(carry_ref)

    x = x_ref[...]
    ts = x.shape[0]
    row = lax.broadcasted_iota(jnp.int32, (ts, ts), 0)
    col = lax.broadcasted_iota(jnp.int32, (ts, ts), 1)
    tri = jnp.where(col <= row, 1.0, 0.0).astype(_BF16)
    hi, mid, lo = _split3(x)
    c = _dot(tri, hi) + _dot(tri, mid) + _dot(tri, lo) + carry_ref[0:1, :]
    carry_ref[0:1, :] = c[ts - 1:ts, :]
    pieces = jnp.concatenate(_split3(c * LOG2E), axis=1)
    for n, out_ref in enumerate((qea_ref, qeb_ref, ke_ref)):
        out_ref[...] = (_dot(pieces, w_ref[n]) + ones_ref[n]).astype(out_ref.dtype)


def _cumsum_bias(x, batch, seq, d, ts):
    ns = seq // ts
    w, ones = _bias_placement(d)
    row_spec = pl.BlockSpec((ts, d), lambda b, s: (b * ns + s, 0))
    out = jax.ShapeDtypeStruct((batch * seq, d), _BF16)
    return pl.pallas_call(
        _cumsum_kernel,
        out_shape=(out, out, out),
        grid=(batch, ns),
        in_specs=[pl.BlockSpec((ts, LANES), lambda b, s: (b * ns + s, 0)),
                  pl.BlockSpec(w.shape, lambda b, s: (0, 0, 0)), pl.BlockSpec(ones.shape, lambda b, s: (0, 0, 0))],
        out_specs=(row_spec, row_spec, row_spec),
        scratch_shapes=[pltpu.VMEM((8, LANES), _F32)],
        compiler_params=_params(2),
        name="cumsum",
    )(x, w, ones)


def _flash_items(nq, blk):
    items = []
    for i in range(nq):
        for j in range(i // 2):
            items.append((i, 2 * j * blk, 2 * blk, False))
        if i % 2 == 1:
            items.append((i, (i - 1) * blk, 2 * blk, True))
        else:
            items.append((i, i * blk, blk, True))
    return items


def _causal_flash(query_maps, k_refs, value_maps, finish, nq, blk, scratch):
    s_bufs, m_ref, acc_ref = scratch[:2], scratch[2], scratch[3]
    items = _flash_items(nq, blk)

    def produce(item, s_ref):
        i, start, width, _ = item
        parts = [r[start:start + width, :] for r in k_refs]
        kb = parts[0] if len(parts) == 1 else jnp.concatenate(parts, axis=1)
        for e, qm in enumerate(query_maps(i)):
            s_ref[e, :, :width] = _dot_nt(qm, kb)

    def consume(item, s_ref, first):
        i, start, width, masked = item
        vbs = value_maps(start, width)
        for e in range(2):
            s = s_ref[e, :, :width]
            if masked:
                row = lax.broadcasted_iota(jnp.int32, (blk, width), 0)
                col = lax.broadcasted_iota(jnp.int32, (blk, width), 1)
                s = jnp.where(col - row <= i * blk - start, s, -jnp.inf)
            m_new = jnp.max(s, axis=1, keepdims=True)
            if not first:
                m_old = m_ref[e]
                m_new = jnp.maximum(m_old, m_new)
            pv = _dot(jnp.exp2((s - m_new).astype(_BF16)), vbs[e])
            acc_ref[e] = pv if first else acc_ref[e] * jnp.exp2(m_old - m_new) + pv
            m_ref[e] = m_new

    produce(items[0], s_bufs[0])
    for t, item in enumerate(items):
        if t + 1 < len(items):
            produce(items[t + 1], s_bufs[(t + 1) % 2])
        consume(item, s_bufs[t % 2], first=(t == 0 or items[t - 1][0] != item[0]))
        if t + 1 == len(items) or items[t + 1][0] != item[0]:
            finish(item[0], acc_ref[0], acc_ref[1])


def _flash_scratch(blk, n_out):
    return [pltpu.VMEM((2, blk, 2 * blk), _F32), pltpu.VMEM((2, blk, 2 * blk), _F32),
            pltpu.VMEM((2, blk, 1), _F32), pltpu.VMEM((2, blk, n_out), _F32)]


def _split_maps(q, lo):
    zero = jnp.zeros_like(q)
    return jnp.where(lo, q, zero), jnp.where(lo, zero, q)


def _fox_attn_kernel(q_ref, qea_ref, qeb_ref, k_ref, ke_ref, v_ref, o_ref, *scratch, blk):
    lo = lax.broadcasted_iota(jnp.int32, (blk, LANES), 1) < HEAD_DIM

    def query_maps(i):
        rows = slice(i * blk, (i + 1) * blk)
        qa, qb = _split_maps(q_ref[rows, :], lo)
        return jnp.concatenate([qa, qea_ref[rows, :]], axis=1), jnp.concatenate([qb, qeb_ref[rows, :]], axis=1)

    def value_maps(start, width):
        vb = v_ref[start:start + width, :]
        keep = lax.broadcasted_iota(jnp.int32, (width, LANES), 1) < HEAD_DIM
        one = jnp.ones_like(vb)
        return jnp.where(keep, vb, one), jnp.where(keep, one, vb)

    def finish(i, acc_a, acc_b):
        out_a = acc_a / pltpu.roll(acc_a, HEAD_DIM, axis=1)
        out_b = acc_b / pltpu.roll(acc_b, HEAD_DIM, axis=1)
        o_ref[i * blk:(i + 1) * blk, :] = jnp.where(lo, out_a, out_b).astype(o_ref.dtype)

    _causal_flash(query_maps, (k_ref, ke_ref), value_maps, finish, q_ref.shape[0] // blk, blk, scratch)


def _fox_attn(qkv, qea, qeb, ke, batch, seq, d):
    blk = min(ATTN_BLOCK, seq)
    pairs = d // LANES
    spec = lambda col0: pl.BlockSpec((seq, LANES), lambda b, h: (b, col0 + h))
    return pl.pallas_call(
        functools.partial(_fox_attn_kernel, blk=blk),
        out_shape=jax.ShapeDtypeStruct((batch * seq, d), _BF16),
        grid=(batch, pairs),
        in_specs=[spec(0), spec(0), spec(0), spec(pairs), spec(0), spec(2 * pairs)],
        out_specs=spec(0),
        scratch_shapes=_flash_scratch(blk, LANES),
        compiler_params=_params(2),
        name="fox_attn",
    )(qkv, qea, qeb, qkv, ke, qkv)


def _fox_mixer(h, g, w_in, b_f, w_out, batch, seq):
    t, d = h.shape
    n_heads = d // HEAD_DIM
    w_qkv = w_in[:, :3 * d].astype(_BF16)
    w_f = jnp.pad(w_in[:, 3 * d:], ((0, 0), (0, LANES - n_heads))).astype(_BF16)
    b_pad = jnp.pad(b_f, (0, LANES - n_heads)).reshape(1, LANES)
    qkv, log_f = _fox_proj(h, g, w_qkv, w_f, b_pad)
    qea, qeb, ke = _cumsum_bias(log_f, batch, seq, d, min(ATTN_BLOCK, seq))
    return _fox_attn(qkv, qea, qeb, ke, batch, seq, d), w_out.astype(_BF16)


def _diff_attn_kernel(q_ref, k_ref, v_ref, lam_ref, g_ref, o_ref, *scratch, blk, lam_init):
    lo = lax.broadcasted_iota(jnp.int32, (blk, LANES), 1) < HEAD_DIM
    lp = lam_ref[...]
    lam = (jnp.exp(jnp.sum(lp[0:1] * lp[1:2], axis=1, keepdims=True))
           - jnp.exp(jnp.sum(lp[2:3] * lp[3:4], axis=1, keepdims=True)) + lam_init)

    def query_maps(i):
        return _split_maps(q_ref[i * blk:(i + 1) * blk, :], lo)

    def value_maps(start, width):
        vb = v_ref[start:start + width, :]
        vb = jnp.concatenate([vb, jnp.ones_like(vb)], axis=1)
        return vb, vb

    def finish(i, acc_1, acc_2):
        o = acc_1[:, :LANES] / acc_1[:, LANES:] - lam * (acc_2[:, :LANES] / acc_2[:, LANES:])
        o = _rms_norm(o, g_ref[...], DIFF_SUBLN_EPS) * (1.0 - lam_init)
        o_ref[i * blk:(i + 1) * blk, :] = o.astype(o_ref.dtype)

    _causal_flash(query_maps, (k_ref,), value_maps, finish, q_ref.shape[0] // blk, blk, scratch)


def _diff_attn(proj, lam_params, subln_g, batch, seq, d, lam_init):
    blk = min(ATTN_BLOCK, seq)
    heads = d // LANES
    spec = lambda col0: pl.BlockSpec((seq, LANES), lambda b, h: (b, col0 + h))
    return pl.pallas_call(
        functools.partial(_diff_attn_kernel, blk=blk, lam_init=lam_init),
        out_shape=jax.ShapeDtypeStruct((batch * seq, d), _BF16),
        grid=(batch, heads),
        in_specs=[spec(0), spec(heads), spec(2 * heads),
                  pl.BlockSpec(lam_params.shape, lambda b, h: (0, 0)), pl.BlockSpec((1, LANES), lambda b, h: (0, 0))],
        out_specs=spec(0),
        scratch_shapes=_flash_scratch(blk, 2 * LANES),
        compiler_params=_params(2),
        name="diff_attn",
    )(proj, proj, proj, lam_params, subln_g.reshape(1, LANES))


def _diff_mixer(h, g, w_in, lam_params, subln_g, w_out, rope, batch, seq, layer_idx):
    t, d = h.shape
    lam_init = 0.8 - 0.6 * math.exp(-0.3 * layer_idx)
    proj = _proj(h, g, w_in.astype(_BF16), rope, d, 2 * d, seq)
    return _diff_attn(proj, lam_params, subln_g, batch, seq, d, lam_init), w_out.astype(_BF16)


def _dil_attn_kernel(q_ref, kp_ref, kc_ref, vp_ref, vc_ref, o_ref, lse_ref, kcat_ref, vcat_ref, s_ref, p_ref, *,
                     n_steps, n_pairs):
    band = DIL_BAND
    n_heads = 2 * n_pairs
    kcat_ref[0:band, :] = kp_ref[...]
    kcat_ref[band:, :] = kc_ref[...]
    vcat_ref[0:band, :] = vp_ref[...]
    vcat_ref[band:, :] = vc_ref[...]
    row = lax.broadcasted_iota(jnp.int32, (band, 2 * band), 0)
    col = lax.broadcasted_iota(jnp.int32, (band, 2 * band), 1)
    in_band = jnp.logical_and(col <= row + band, col >= row + band - n_steps)
    first_valid = jnp.logical_and(in_band, jnp.logical_or(col >= band, pl.program_id(1) > 0))
    lane = lax.broadcasted_iota(jnp.int32, (band, LANES), 1)
    lo = lane < HEAD_DIM
    lo_v = lax.broadcasted_iota(jnp.int32, (2 * band, LANES), 1) < HEAD_DIM
    for r in range(q_ref.shape[0] // band):
        rows = slice(r * band, (r + 1) * band)
        win = slice(r * band, (r + 2) * band)
        valid = first_valid if r == 0 else in_band
        for hp in range(n_pairs):
            sl = slice(hp * LANES, (hp + 1) * LANES)
            q = q_ref[rows, sl]
            zero = jnp.zeros_like(q)
            kw = kcat_ref[win, sl]
            for e, qm in enumerate((jnp.where(lo, q, zero), jnp.where(lo, zero, q))):
                s_ref[(2 * hp + e) * band:(2 * hp + e + 1) * band, :] = jnp.where(valid, _dot_nt(qm, kw), -jnp.inf)
        s = s_ref[...]
        m = jnp.max(s, axis=1, keepdims=True)
        p_ref[...] = jnp.exp2((s - m).astype(_BF16))
        m_tile = jnp.zeros((band, LANES), _F32)
        l_tile = jnp.ones((band, LANES), _F32)
        for hp in range(n_pairs):
            sl = slice(hp * LANES, (hp + 1) * LANES)
            vw = vcat_ref[win, sl]
            one = jnp.ones_like(vw)
            ha, hb = slice((2 * hp) * band, (2 * hp + 1) * band), slice((2 * hp + 1) * band, (2 * hp + 2) * band)
            pv_a = _dot(p_ref[ha, :], jnp.where(lo_v, vw, one))
            pv_b = _dot(p_ref[hb, :], jnp.where(lo_v, one, vw))
            l_a, l_b = pltpu.roll(pv_a, HEAD_DIM, axis=1), pltpu.roll(pv_b, HEAD_DIM, axis=1)
            o_ref[rows, sl] = (jnp.where(lo, pv_a, pv_b) / jnp.where(lo, l_a, l_b)).astype(o_ref.dtype)
            m_tile = jnp.where(lane == 2 * hp, m[ha], jnp.where(lane == 2 * hp + 1, m[hb], m_tile))
            l_tile = jnp.where(lane == 2 * hp, l_a, jnp.where(lane == 2 * hp + 1, pv_b, l_tile))
        lse_ref[rows, :] = (m_tile + jnp.log2(l_tile)) * (1.0 / LOG2E)


def _dil_attn(q, q_col, k, k_col, v, v_col, n_seq, n_sub, d, n_steps):
    rb = min(ROW_TILE, n_sub)
    nb = n_sub // rb
    bands = rb // DIL_BAND
    n_heads = d // HEAD_DIM
    cur = lambda col: (lambda x, n: (x * nb + n, col))
    prev = lambda col: (lambda x, n: ((x * nb + n) * bands - jnp.where(n > 0, 1, 0), col))
    return pl.pallas_call(
        functools.partial(_dil_attn_kernel, n_steps=n_steps, n_pairs=d // LANES),
        out_shape=(jax.ShapeDtypeStruct((n_seq * n_sub, d), _BF16),
                   jax.ShapeDtypeStruct((n_seq * n_sub, LANES), _F32)),
        grid=(n_seq, nb),
        in_specs=[pl.BlockSpec((rb, d), cur(q_col)),
                  pl.BlockSpec((DIL_BAND, d), prev(k_col)), pl.BlockSpec((rb, d), cur(k_col)),
                  pl.BlockSpec((DIL_BAND, d), prev(v_col)), pl.BlockSpec((rb, d), cur(v_col))],
        out_specs=(pl.BlockSpec((rb, d), cur(0)), pl.BlockSpec((rb, LANES), cur(0))),
        scratch_shapes=[pltpu.VMEM((rb + DIL_BAND, d), _BF16), pltpu.VMEM((rb + DIL_BAND, d), _BF16),
                        pltpu.VMEM((n_heads * DIL_BAND, 2 * DIL_BAND), _F32),
                        pltpu.VMEM((n_heads * DIL_BAND, 2 * DIL_BAND), _BF16)],
        compiler_params=_params(2),
        name="dil_attn",
    )(q, k, k, v, v)


def _dil_proj_kernel(h_ref, g_ref, w_ref, ra_ref, rb_ref, rc_ref, *refs, d_model, dilations):
    out_refs, ybuf = refs[:len(dilations)], refs[len(dilations)]
    n_groups = len(dilations)
    tm = h_ref.shape[0]
    xn = _rms_norm(h_ref[...], g_ref[...], NORM_EPS).astype(_BF16)
    half = ROT_DIM // 2
    n_slab = 0
    for c in range(0, w_ref.shape[1], LANES * 2):
        y = _dot(xn, w_ref[:, c:c + 2 * LANES])
        for s in range(0, 2 * LANES, LANES):
            src, col = divmod(c + s, d_model)
            ys = y[:, s:s + LANES]
            if src <= n_groups:
                ys = (ys * ra_ref[...] + pltpu.roll(ys, LANES - half, axis=1) * rb_ref[...]
                      + pltpu.roll(ys, half, axis=1) * rc_ref[...])
            if src < n_groups:
                ys = ys * QK_SCALE
            dests = [(src, 0)] if src < n_groups else [(g, src - n_groups + 1) for g in range(n_groups)]
            slab = ybuf.at[n_slab % ybuf.shape[0]]
            n_slab += 1
            if any(dilations[g] > 1 for g, _ in dests):
                slab[...] = ys
            for g, blk in dests:
                dil = dilations[g]
                lanes = slice(blk * d_model + col, blk * d_model + col + LANES)
                if dil == 1:
                    out_refs[g][0, :, lanes] = ys.astype(_BF16)
                else:
                    for r in range(dil):
                        out_refs[g][r, :, lanes] = slab[pl.ds(r, tm // dil, stride=dil), :].astype(_BF16)


def _dil_proj(h, g, w, rope, batch, seq, dilations):
    t, d = h.shape
    tm = min(ROW_TILE, seq)
    nblk = seq // tm
    rope_spec = pl.BlockSpec((tm, LANES), lambda i: (i % nblk, 0))
    out_shapes = tuple(jax.ShapeDtypeStruct((batch, dil, seq // dil, 3 * d), _BF16) for dil in dilations)
    out_specs = tuple(pl.BlockSpec((None, dil, tm // dil, 3 * d), lambda i: (i // nblk, 0, i % nblk, 0))
                      for dil in dilations)
    return pl.pallas_call(
        functools.partial(_dil_proj_kernel, d_model=d, dilations=dilations),
        out_shape=out_shapes,
        grid=(t // tm,),
        in_specs=[_row_spec(tm, d), _const_spec((1, d)), _const_spec(w.shape), rope_spec, rope_spec, rope_spec],
        out_specs=out_specs,
        scratch_shapes=[pltpu.VMEM((4, tm, LANES), _F32)],
        compiler_params=_params(1),
        name="dil_proj",
    )(h, g.reshape(1, d), w, *rope)


def _dil_merge_kernel(h_ref, *refs, dilations):
    n = len(dilations)
    o_refs, l_refs = refs[:n], refs[n:2 * n]
    e_ref, w_ref, out_ref, obuf, lbuf, acc_ref = refs[2 * n:]
    tm, d = h_ref.shape

    def by_position(src_ref, dil, lanes, buf):
        if dil == 1:
            return src_ref[0, :, lanes].astype(_F32)
        for r in range(dil):
            buf[pl.ds(r, tm // dil, stride=dil), :] = src_ref[r, :, lanes].astype(_F32)
        return buf[...]

    lses = [by_position(l_refs[gi], dil, slice(0, LANES), lbuf.at[gi]) for gi, dil in enumerate(dilations)]
    m = functools.reduce(jnp.maximum, lses)
    es = [jnp.exp(l - m) for l in lses]
    den = functools.reduce(lambda a, b: a + b, es)
    wides = []
    for e in es:
        hi, mid, lo = _split3(e / den)
        wides.append(_dot(hi, e_ref[...]) + _dot(mid, e_ref[...]) + _dot(lo, e_ref[...]))
    for c in range(0, d, LANES):
        lanes = slice(c, c + LANES)
        acc = jnp.zeros((tm, LANES), _F32)
        for gi, dil in enumerate(dilations):
            acc = acc + wides[gi][:, lanes] * by_position(o_refs[gi], dil, lanes, obuf.at[(c // LANES) % 2, gi])
        acc_ref[:, lanes] = acc.astype(_BF16)
    out_ref[...] = h_ref[...] + _dot(acc_ref[...], w_ref[...])


def _dil_merge(h, outs, lses, w_out, batch, seq, dilations):
    t, d = h.shape
    tm = min(ROW_TILE, seq)
    nblk = seq // tm
    n_heads = d // HEAD_DIM
    expand = (jnp.arange(LANES)[:, None] == (jnp.arange(d) // HEAD_DIM)[None, :]) & (jnp.arange(LANES) < n_heads)[:, None]
    res_spec = lambda dil, n: pl.BlockSpec((None, dil, tm // dil, n), lambda i: (i // nblk, 0, i % nblk, 0))
    n = len(dilations)
    return pl.pallas_call(
        functools.partial(_dil_merge_kernel, dilations=dilations),
        out_shape=jax.ShapeDtypeStruct((t, d), _F32),
        grid=(t // tm,),
        in_specs=([_row_spec(tm, d)] + [res_spec(dil, d) for dil in dilations] + [res_spec(dil, LANES) for dil in dilations]
                  + [_const_spec((LANES, d)), _const_spec(w_out.shape)]),
        out_specs=_row_spec(tm, d),
        scratch_shapes=[pltpu.VMEM((2, n, tm, LANES), _F32), pltpu.VMEM((n, tm, LANES), _F32),
                        pltpu.VMEM((tm, d), _BF16)],
        compiler_params=_params(1),
        name="dil_merge",
    )(h, *outs, *lses, expand.astype(_BF16), w_out)


def _dil_mixer(h, g, w_in, w_out, rope, batch, seq):
    t, d = h.shape
    dilations = tuple(dil for _, dil in DIL_CONFIGS)
    projs = _dil_proj(h, g, w_in.astype(_BF16), rope, batch, seq, dilations)
    outs, lses = [], []
    for (window, dil), proj in zip(DIL_CONFIGS, projs):
        assert seq % (dil * DIL_BAND) == 0
        proj = proj.reshape(t, 3 * d)
        o, lse = _dil_attn(proj, 0, proj, 1, proj, 2, batch * dil, seq // dil, d, window // dil)
        outs.append(o.reshape(batch, dil, seq // dil, d))
        lses.append(lse.reshape(batch, dil, seq // dil, LANES))
    return _dil_merge(h, outs, lses, w_out.astype(_BF16), batch, seq, dilations)


def _gelu(x):
    return 0.5 * x * (1.0 + lax.erf(x * math.sqrt(0.5)))


def _sgu_kernel(h_ref, g_ref, win_ref, nv_ref, ws_ref, bs_ref, wout_ref, o_ref, u_ref, v_ref, y_ref, *, chunk):
    x = h_ref[...]
    tm = x.shape[0]
    half = wout_ref.shape[0]
    gdim = half // SGU_GROUPS
    xn = _rms_norm(x, g_ref[...], NORM_EPS).astype(_BF16)
    ssq = jnp.zeros((tm, 1), _F32)
    for c in range(0, half, chunk):
        u_ref[:, c:c + chunk] = _gelu(_dot(xn, win_ref[:, c:c + chunk]))
        vv = _gelu(_dot(xn, win_ref[:, half + c:half + c + chunk]))
        v_ref[:, c:c + chunk] = vv
        ssq = ssq + jnp.sum(vv * vv, axis=1, keepdims=True)
    rstd = lax.rsqrt(ssq / half + NORM_EPS)
    row = lax.broadcasted_iota(jnp.int32, (SGU_CHUNK, SGU_CHUNK), 0)
    col = lax.broadcasted_iota(jnp.int32, (SGU_CHUNK, SGU_CHUNK), 1)
    for gi in range(SGU_GROUPS):
        cols = slice(gi * gdim, (gi + 1) * gdim)
        w = jnp.where(col <= row, ws_ref[gi], jnp.zeros((), _BF16))
        bias = bs_ref[:, gi:gi + 1]
        for r in range(0, tm, SGU_CHUNK):
            rows = slice(r, r + SGU_CHUNK)
            vn = (v_ref[rows, cols] * rstd[rows] * nv_ref[:, cols]).astype(_BF16)
            y_ref[rows, cols] = (u_ref[rows, cols] * (_dot(w, vn) + bias)).astype(_BF16)
    o_ref[...] = x + _dot(y_ref[...], wout_ref[...])


def _sgu_mixer(h, g, w_in, norm_v, w_s, b_s, w_out):
    t, d = h.shape
    half = w_out.shape[0]
    tm = min(ROW_TILE, t)
    assert tm % SGU_CHUNK == 0 and (half // SGU_GROUPS) % LANES == 0
    return pl.pallas_call(
        functools.partial(_sgu_kernel, chunk=_col_chunk(half)),
        out_shape=jax.ShapeDtypeStruct((t, d), _F32),
        grid=(t // tm,),
        in_specs=[_row_spec(tm, d), _const_spec((1, d)), _const_spec(w_in.shape), _const_spec((1, half)),
                  _const_spec(w_s.shape), _const_spec((SGU_CHUNK, SGU_GROUPS)), _const_spec(w_out.shape)],
        out_specs=_row_spec(tm, d),
        scratch_shapes=[pltpu.VMEM((tm, half), _F32), pltpu.VMEM((tm, half), _F32), pltpu.VMEM((tm, half), _BF16)],
        compiler_params=_params(1),
        name="sgu",
    )(h, g.reshape(1, d), w_in.astype(_BF16), norm_v.reshape(1, half), w_s.astype(_BF16), jnp.transpose(b_s),
      w_out.astype(_BF16))


def kernel(x, p, norm_ffn1, w_ffn1_in, w_ffn1_out, norm_mix, norm_ffn2, w_ffn2_in, w_ffn2_out, norm_ple, w_ple_gate, b_ple_gate, w_ple_proj, fox_w_in, fox_b_f, fox_w_out, dil_w_in, dil_w_out, diff_w_in, diff_lambda, diff_subln, diff_w_out, sgu_w_in, sgu_norm_v, sgu_w_s, sgu_b_s, sgu_w_out, norm_final):
    batch, seq, d = x.shape
    depth = p.shape[0]
    t = batch * seq
    h = x.reshape(t, d)
    rope = _rope_tables(seq)
    n_mixers = 4
    for i in range(depth):
        kind, j = i % n_mixers, i // n_mixers
        h = _ffn(h, norm_ffn1[i], w_ffn1_in[i].astype(_BF16), w_ffn1_out[i].astype(_BF16))
        attn = None
        if kind == 0:
            attn = _fox_mixer(h, norm_mix[i], fox_w_in[j], fox_b_f[j], fox_w_out[j], batch, seq)
        elif kind == 1:
            h = _dil_mixer(h, norm_mix[i], dil_w_in[j], dil_w_out[j], rope, batch, seq)
        elif kind == 2:
            attn = _diff_mixer(h, norm_mix[i], diff_w_in[j], diff_lambda[j], diff_subln[j], diff_w_out[j], rope,
                               batch, seq, i)
        else:
            h = _sgu_mixer(h, norm_mix[i], sgu_w_in[j], sgu_norm_v[j], sgu_w_s[j], sgu_b_s[j], sgu_w_out[j])
        h = _post_mixer(h, attn, norm_ffn2[i], w_ffn2_in[i].astype(_BF16), w_ffn2_out[i].astype(_BF16),
                        p.reshape(depth, t, p.shape[-1]), i, norm_ple[i], w_ple_gate[i].astype(_BF16), b_ple_gate[i],
                        w_ple_proj[i].astype(_BF16), norm_final, final_norm=(i == depth - 1))
    return h.reshape(batch, seq, d)
```

```python
import functools
import math

import jax
import jax.numpy as jnp
import numpy as np
from jax import lax
from jax.experimental import pallas as pl
from jax.experimental.pallas import tpu as pltpu

_F32 = jnp.float32
_BF16 = jnp.bfloat16

HEAD_DIM = 64
ROPE_THETA = 500000.0
ROT_DIM = HEAD_DIM // 4
NORM_EPS = 1e-6
DIL_CONFIGS = ((128, 1), (512, 4), (2048, 16))
DIL_BAND = 128
DIFF_SUBLN_EPS = 1e-5
SGU_GROUPS = 8
SGU_CHUNK = 128

LANES = 128
VMEM_LIMIT_BYTES = 56 * 1024 * 1024
ROW_TILE = 512
ATTN_BLOCK = 512
COL_CHUNK = 256
LOG2E = math.log2(math.e)
QK_SCALE = HEAD_DIM ** -0.5 * LOG2E


def _params(n_axes):
    return pltpu.CompilerParams(dimension_semantics=("arbitrary",) * n_axes,
                                vmem_limit_bytes=VMEM_LIMIT_BYTES)


def _col_chunk(n, target=COL_CHUNK):
    best = LANES
    for c in range(LANES, target + 1, LANES):
        if n % c == 0:
            best = c
    assert n % best == 0, (n, best)
    return best


def _row_spec(tm, n):
    return pl.BlockSpec((tm, n), lambda i: (i, 0))


def _const_spec(shape):
    return pl.BlockSpec(shape, lambda i: (0,) * len(shape))


def _rms_norm(x, g, eps):
    return x * lax.rsqrt(jnp.mean(x * x, axis=-1, keepdims=True) + eps) * g


def _dot(a, b):
    return jnp.dot(a, b, preferred_element_type=_F32)


def _dot_nt(a, b):
    return lax.dot_general(a, b, (((1,), (1,)), ((), ())), preferred_element_type=_F32)


def _split3(x):
    hi = x.astype(_BF16)
    r1 = x - hi.astype(_F32)
    mid = r1.astype(_BF16)
    lo = (r1 - mid.astype(_F32)).astype(_BF16)
    return hi, mid, lo


def _packed_pieces(x, n_heads):
    assert 3 * n_heads <= LANES
    lane = lax.broadcasted_iota(jnp.int32, x.shape, 1)
    packed = jnp.zeros_like(x)
    for n, piece in enumerate(_split3(x)):
        piece = jnp.where(lane < n_heads, piece.astype(_F32), 0.0)
        packed = packed + (pltpu.roll(piece, n * n_heads, axis=1) if n else piece)
    return packed.astype(_BF16)


def _ffn_kernel(h_ref, g_ref, win_ref, wout_ref, o_ref, hid_ref, *, d_ff, chunk):
    x = h_ref[...]
    xn = _rms_norm(x, g_ref[...], NORM_EPS).astype(_BF16)
    for c in range(0, d_ff, chunk):
        gate = _dot(xn, win_ref[:, c:c + chunk])
        up = _dot(xn, win_ref[:, d_ff + c:d_ff + c + chunk])
        hid_ref[:, c:c + chunk] = (gate * jax.nn.sigmoid(gate) * up).astype(_BF16)
    o_ref[...] = x + 0.5 * _dot(hid_ref[...], wout_ref[...])


def _ffn(h, g, w_in, w_out):
    t, d = h.shape
    d_ff = w_out.shape[0]
    tm = min(ROW_TILE, t)
    return pl.pallas_call(
        functools.partial(_ffn_kernel, d_ff=d_ff, chunk=_col_chunk(d_ff)),
        out_shape=jax.ShapeDtypeStruct((t, d), _F32),
        grid=(t // tm,),
        in_specs=[_row_spec(tm, d), _const_spec((1, d)), _const_spec(w_in.shape), _const_spec(w_out.shape)],
        out_specs=_row_spec(tm, d),
        scratch_shapes=[pltpu.VMEM((tm, d_ff), _BF16)],
        compiler_params=_params(1),
        name="ffn",
    )(h, g.reshape(1, d), w_in, w_out)


def _post_kernel(*refs, has_attn, final_norm, d_ff, chunk):
    h_ref = refs[0]
    o_ref, wo_ref = refs[1:3] if has_attn else (None, None)
    (gf2_ref, win_ref, wout_ref, p_ref, gp_ref, wg_ref, bg_ref, wp_ref, gfin_ref,
     out_ref, hid_ref) = refs[3 if has_attn else 1:]
    x = h_ref[...]
    if has_attn:
        x = x + _dot(o_ref[...], wo_ref[...])
    xn = _rms_norm(x, gf2_ref[...], NORM_EPS).astype(_BF16)
    for c in range(0, d_ff, chunk):
        gate = _dot(xn, win_ref[:, c:c + chunk])
        up = _dot(xn, win_ref[:, d_ff + c:d_ff + c + chunk])
        hid_ref[:, c:c + chunk] = (gate * jax.nn.sigmoid(gate) * up).astype(_BF16)
    x = x + 0.5 * _dot(hid_ref[...], wout_ref[...])
    xn = _rms_norm(x, gp_ref[...], NORM_EPS).astype(_BF16)
    gate = jax.nn.sigmoid(_dot(xn, wg_ref[...]) + bg_ref[...])
    x = x + gate * _dot(p_ref[...].astype(_BF16), wp_ref[...])
    if final_norm:
        x = _rms_norm(x, gfin_ref[...], NORM_EPS)
    out_ref[...] = x


def _post_mixer(h, attn, g_ffn, w_in, w_out, p, layer, g_ple, w_gate, b_gate, w_proj, g_final, final_norm):
    t, d = h.shape
    d_ff = w_out.shape[0]
    dp = p.shape[2]
    tm = min(ROW_TILE, t)
    args, specs = [h], [_row_spec(tm, d)]
    if attn is not None:
        args += list(attn)
        specs += [_row_spec(tm, attn[0].shape[1]), _const_spec(attn[1].shape)]
    args += [g_ffn.reshape(1, d), w_in, w_out, p, g_ple.reshape(1, d), w_gate, b_gate.reshape(1, d), w_proj,
             g_final.reshape(1, d)]
    specs += [_const_spec((1, d)), _const_spec(w_in.shape), _const_spec(w_out.shape),
              pl.BlockSpec((None, tm, dp), lambda i: (layer, i, 0)),
              _const_spec((1, d)), _const_spec(w_gate.shape), _const_spec((1, d)), _const_spec(w_proj.shape),
              _const_spec((1, d))]
    return pl.pallas_call(
        functools.partial(_post_kernel, has_attn=attn is not None, final_norm=final_norm, d_ff=d_ff,
                          chunk=_col_chunk(d_ff)),
        out_shape=jax.ShapeDtypeStruct((t, d), _F32),
        grid=(t // tm,),
        in_specs=specs,
        out_specs=_row_spec(tm, d),
        scratch_shapes=[pltpu.VMEM((tm, d_ff), _BF16)],
        compiler_params=_params(1),
        name="post_mixer",
    )(*args)


def _rope_tables(s):
    half = ROT_DIM // 2
    inv_freq = ROPE_THETA ** (-jnp.arange(0, ROT_DIM, 2, dtype=_F32) / ROT_DIM)
    ang = jnp.arange(s, dtype=jnp.int32).astype(_F32)[:, None] * inv_freq[None, :]
    cos, sin = jnp.cos(ang), jnp.sin(ang)
    m = jnp.arange(LANES) % HEAD_DIM
    idx = m % half
    first, second = m < half, (m >= half) & (m < ROT_DIM)
    a = jnp.where((first | second)[None, :], cos[:, idx], 1.0)
    b = jnp.where(first[None, :], -sin[:, idx], 0.0)
    c = jnp.where(second[None, :], sin[:, idx], 0.0)
    return a.astype(_F32), b.astype(_F32), c.astype(_F32)


def _proj_kernel(h_ref, g_ref, w_ref, ra_ref, rb_ref, rc_ref, o_ref, *, n_query, n_rope, chunk):
    xn = _rms_norm(h_ref[...], g_ref[...], NORM_EPS).astype(_BF16)
    half = ROT_DIM // 2
    for c in range(0, w_ref.shape[1], chunk):
        y = _dot(xn, w_ref[:, c:c + chunk])
        if c < n_rope:
            for s in range(0, chunk, LANES):
                ys = y[:, s:s + LANES]
                ys = (ys * ra_ref[...] + pltpu.roll(ys, LANES - half, axis=1) * rb_ref[...]
                      + pltpu.roll(ys, half, axis=1) * rc_ref[...])
                if c < n_query:
                    ys = ys * QK_SCALE
                o_ref[:, c + s:c + s + LANES] = ys.astype(o_ref.dtype)
        else:
            o_ref[:, c:c + chunk] = y.astype(o_ref.dtype)


def _proj(h, g, w, rope, n_query, n_rope, seq):
    t, d = h.shape
    n = w.shape[1]
    tm = min(ROW_TILE, seq)
    chunk = _col_chunk(n)
    assert n_rope % chunk == 0 and n_query % chunk == 0 and n_query <= n_rope
    nblk = seq // tm
    rope_spec = pl.BlockSpec((tm, LANES), lambda i: (i % nblk, 0))
    return pl.pallas_call(
        functools.partial(_proj_kernel, n_query=n_query, n_rope=n_rope, chunk=chunk),
        out_shape=jax.ShapeDtypeStruct((t, n), _BF16),
        grid=(t // tm,),
        in_specs=[_row_spec(tm, d), _const_spec((1, d)), _const_spec(w.shape), rope_spec, rope_spec, rope_spec],
        out_specs=_row_spec(tm, n),
        compiler_params=_params(1),
        name="proj",
    )(h, g.reshape(1, d), w, *rope)


def _fox_proj_kernel(h_ref, g_ref, w_ref, wf_ref, bf_ref, qkv_ref, lf_ref, *, n_query, chunk):
    xn = _rms_norm(h_ref[...], g_ref[...], NORM_EPS).astype(_BF16)
    for c in range(0, w_ref.shape[1], chunk):
        y = _dot(xn, w_ref[:, c:c + chunk])
        if c < n_query:
            y = y * QK_SCALE
        qkv_ref[:, c:c + chunk] = y.astype(qkv_ref.dtype)
    z = _dot(xn, wf_ref[...]) + bf_ref[...]
    lf_ref[...] = jnp.minimum(z, 0.0) - jnp.log1p(jnp.exp(-jnp.abs(z)))


def _fox_proj(h, g, w_qkv, w_f, b_f):
    t, d = h.shape
    n = w_qkv.shape[1]
    tm = min(ROW_TILE, t)
    return pl.pallas_call(
        functools.partial(_fox_proj_kernel, n_query=d, chunk=_col_chunk(n)),
        out_shape=(jax.ShapeDtypeStruct((t, n), _BF16), jax.ShapeDtypeStruct((t, LANES), _F32)),
        grid=(t // tm,),
        in_specs=[_row_spec(tm, d), _const_spec((1, d)), _const_spec(w_qkv.shape), _const_spec(w_f.shape),
                  _const_spec((1, LANES))],
        out_specs=(_row_spec(tm, n), _row_spec(tm, LANES)),
        compiler_params=_params(1),
        name="fox_proj",
    )(h, g.reshape(1, d), w_qkv, w_f, b_f)


def _bias_placement(d):
    pairs = d // LANES
    n_heads = 2 * pairs
    w = np.zeros((3, LANES, d), np.float32)
    ones = np.zeros((3, 1, d), np.float32)
    for hp in range(pairs):
        base = hp * LANES
        for p in range(3):
            w[0, p * n_heads + 2 * hp, base + p] = 1.0
            w[1, p * n_heads + 2 * hp + 1, base + 6 + p] = 1.0
            w[2, p * n_heads + 2 * hp, base + 3 + p] = -1.0
            w[2, p * n_heads + 2 * hp + 1, base + 9 + p] = -1.0
            ones[0, 0, base + 3 + p] = 1.0
            ones[1, 0, base + 9 + p] = 1.0
            ones[2, 0, base + p] = 1.0
            ones[2, 0, base + 6 + p] = 1.0
    return jnp.asarray(w, _BF16), jnp.asarray(ones, _F32)


def _cumsum_kernel(x_ref, w_ref, ones_ref, qea_ref, qeb_ref, ke_ref, carry_ref, *, n_heads):
    @pl.when(pl.program_id(1) == 0)
    def _():
        carry_ref[...] = jnp.zeros_like(carry_ref)

    x = x_ref[...]
    ts = x.shape[0]
    row = lax.broadcasted_iota(jnp.int32, (ts, ts), 0)
    col = lax.broadcasted_iota(jnp.int32, (ts, ts), 1)
    tri = jnp.where(col <= row, 1.0, 0.0).astype(_BF16)
    hi, mid, lo = _split3(x)
    c = _dot(tri, hi) + _dot(tri, mid) + _dot(tri, lo) + carry_ref[0:1, :]
    carry_ref[0:1, :] = c[ts - 1:ts, :]
    pieces = _packed_pieces(c * LOG2E, n_heads)
    for n, out_ref in enumerate((qea_ref, qeb_ref, ke_ref)):
        out_ref[...] = (_dot(pieces, w_ref[n]) + ones_ref[n]).astype(out_ref.dtype)


def _cumsum_bias(x, batch, seq, d, ts):
    ns = seq // ts
    w, ones = _bias_placement(d)
    row_spec = pl.BlockSpec((ts, d), lambda b, s: (b * ns + s, 0))
    out = jax.ShapeDtypeStruct((batch * seq, d), _BF16)
    return pl.pallas_call(
        functools.partial(_cumsum_kernel, n_heads=d // HEAD_DIM),
        out_shape=(out, out, out),
        grid=(batch, ns),
        in_specs=[pl.BlockSpec((ts, LANES), lambda b, s: (b * ns + s, 0)),
                  pl.BlockSpec(w.shape, lambda b, s: (0, 0, 0)), pl.BlockSpec(ones.shape, lambda b, s: (0, 0, 0))],
        out_specs=(row_spec, row_spec, row_spec),
        scratch_shapes=[pltpu.VMEM((8, LANES), _F32)],
        compiler_params=_params(2),
        name="cumsum",
    )(x, w, ones)


def _flash_items(nq, blk):
    items = []
    for i in range(nq):
        for j in range(i // 2):
            items.append((i, 2 * j * blk, 2 * blk, False))
        if i % 2 == 1:
            items.append((i, (i - 1) * blk, 2 * blk, True))
        else:
            items.append((i, i * blk, blk, True))
    return items


def _causal_flash(query_maps, k_refs, value_maps, finish, nq, blk, scratch):
    s_bufs, m_ref, acc_ref = scratch[:2], scratch[2], scratch[3]
    items = _flash_items(nq, blk)

    def produce(item, s_ref):
        i, start, width, _ = item
        parts = [r[start:start + width, :] for r in k_refs]
        kb = parts[0] if len(parts) == 1 else jnp.concatenate(parts, axis=1)
        for e, qm in enumerate(query_maps(i)):
            s_ref[e, :, :width] = _dot_nt(qm, kb)

    def consume(item, s_ref, first):
        i, start, width, masked = item
        vbs = value_maps(start, width)
        for e in range(2):
            s = s_ref[e, :, :width]
            if masked:
                row = lax.broadcasted_iota(jnp.int32, (blk, width), 0)
                col = lax.broadcasted_iota(jnp.int32, (blk, width), 1)
                s = jnp.where(col - row <= i * blk - start, s, -jnp.inf)
            m_new = jnp.max(s, axis=1, keepdims=True)
            if not first:
                m_old = m_ref[e]
                m_new = jnp.maximum(m_old, m_new)
            pv = _dot(jnp.exp2((s - m_new).astype(_BF16)), vbs[e])
            acc_ref[e] = pv if first else acc_ref[e] * jnp.exp2(m_old - m_new) + pv
            m_ref[e] = m_new

    produce(items[0], s_bufs[0])
    for t, item in enumerate(items):
        if t + 1 < len(items):
            produce(items[t + 1], s_bufs[(t + 1) % 2])
        consume(item, s_bufs[t % 2], first=(t == 0 or items[t - 1][0] != item[0]))
        if t + 1 == len(items) or items[t + 1][0] != item[0]:
            finish(item[0], acc_ref[0], acc_ref[1])


def _flash_scratch(blk, n_out):
    return [pltpu.VMEM((2, blk, 2 * blk), _F32), pltpu.VMEM((2, blk, 2 * blk), _F32),
            pltpu.VMEM((2, blk, 1), _F32), pltpu.VMEM((2, blk, n_out), _F32)]


def _split_maps(q, lo):
    zero = jnp.zeros_like(q)
    return jnp.where(lo, q, zero), jnp.where(lo, zero, q)


def _fox_attn_kernel(q_ref, qea_ref, qeb_ref, k_ref, ke_ref, v_ref, o_ref, *scratch, blk):
    lo = lax.broadcasted_iota(jnp.int32, (blk, LANES), 1) < HEAD_DIM

    def query_maps(i):
        rows = slice(i * blk, (i + 1) * blk)
        qa, qb = _split_maps(q_ref[rows, :], lo)
        return jnp.concatenate([qa, qea_ref[rows, :]], axis=1), jnp.concatenate([qb, qeb_ref[rows, :]], axis=1)

    def value_maps(start, width):
        vb = v_ref[start:start + width, :]
        keep = lax.broadcasted_iota(jnp.int32, (width, LANES), 1) < HEAD_DIM
        one = jnp.ones_like(vb)
        return jnp.where(keep, vb, one), jnp.where(keep, one, vb)

    def finish(i, acc_a, acc_b):
        out_a = acc_a / pltpu.roll(acc_a, HEAD_DIM, axis=1)
        out_b = acc_b / pltpu.roll(acc_b, HEAD_DIM, axis=1)
        o_ref[i * blk:(i + 1) * blk, :] = jnp.where(lo, out_a, out_b).astype(o_ref.dtype)

    _causal_flash(query_maps, (k_ref, ke_ref), value_maps, finish, q_ref.shape[0] // blk, blk, scratch)


def _fox_attn(qkv, qea, qeb, ke, batch, seq, d):
    blk = min(ATTN_BLOCK, seq)
    pairs = d // LANES
    spec = lambda col0: pl.BlockSpec((seq, LANES), lambda b, h: (b, col0 + h))
    return pl.pallas_call(
        functools.partial(_fox_attn_kernel, blk=blk),
        out_shape=jax.ShapeDtypeStruct((batch * seq, d), _BF16),
        grid=(batch, pairs),
        in_specs=[spec(0), spec(0), spec(0), spec(pairs), spec(0), spec(2 * pairs)],
        out_specs=spec(0),
        scratch_shapes=_flash_scratch(blk, LANES),
        compiler_params=_params(2),
        name="fox_attn",
    )(qkv, qea, qeb, qkv, ke, qkv)


def _fox_mixer(h, g, w_in, b_f, w_out, batch, seq):
    t, d = h.shape
    n_heads = d // HEAD_DIM
    w_qkv = w_in[:, :3 * d].astype(_BF16)
    w_f = jnp.pad(w_in[:, 3 * d:], ((0, 0), (0, LANES - n_heads))).astype(_BF16)
    b_pad = jnp.pad(b_f, (0, LANES - n_heads)).reshape(1, LANES)
    qkv, log_f = _fox_proj(h, g, w_qkv, w_f, b_pad)
    qea, qeb, ke = _cumsum_bias(log_f, batch, seq, d, min(ATTN_BLOCK, seq))
    return _fox_attn(qkv, qea, qeb, ke, batch, seq, d), w_out.astype(_BF16)


def _diff_attn_kernel(q_ref, k_ref, v_ref, lam_ref, g_ref, o_ref, *scratch, blk, lam_init):
    lo = lax.broadcasted_iota(jnp.int32, (blk, LANES), 1) < HEAD_DIM
    lp = lam_ref[...]
    lam = (jnp.exp(jnp.sum(lp[0:1] * lp[1:2], axis=1, keepdims=True))
           - jnp.exp(jnp.sum(lp[2:3] * lp[3:4], axis=1, keepdims=True)) + lam_init)

    def query_maps(i):
        return _split_maps(q_ref[i * blk:(i + 1) * blk, :], lo)

    def value_maps(start, width):
        vb = v_ref[start:start + width, :]
        vb = jnp.concatenate([vb, jnp.ones_like(vb)], axis=1)
        return vb, vb

    def finish(i, acc_1, acc_2):
        o = acc_1[:, :LANES] / acc_1[:, LANES:] - lam * (acc_2[:, :LANES] / acc_2[:, LANES:])
        o = _rms_norm(o, g_ref[...], DIFF_SUBLN_EPS) * (1.0 - lam_init)
        o_ref[i * blk:(i + 1) * blk, :] = o.astype(o_ref.dtype)

    _causal_flash(query_maps, (k_ref,), value_maps, finish, q_ref.shape[0] // blk, blk, scratch)


def _diff_attn(proj, lam_params, subln_g, batch, seq, d, lam_init):
    blk = min(ATTN_BLOCK, seq)
    heads = d // LANES
    spec = lambda col0: pl.BlockSpec((seq, LANES), lambda b, h: (b, col0 + h))
    return pl.pallas_call(
        functools.partial(_diff_attn_kernel, blk=blk, lam_init=lam_init),
        out_shape=jax.ShapeDtypeStruct((batch * seq, d), _BF16),
        grid=(batch, heads),
        in_specs=[spec(0), spec(heads), spec(2 * heads),
                  pl.BlockSpec(lam_params.shape, lambda b, h: (0, 0)), pl.BlockSpec((1, LANES), lambda b, h: (0, 0))],
        out_specs=spec(0),
        scratch_shapes=_flash_scratch(blk, 2 * LANES),
        compiler_params=_params(2),
        name="diff_attn",
    )(proj, proj, proj, lam_params, subln_g.reshape(1, LANES))


def _diff_mixer(h, g, w_in, lam_params, subln_g, w_out, rope, batch, seq, layer_idx):
    t, d = h.shape
    lam_init = 0.8 - 0.6 * math.exp(-0.3 * layer_idx)
    proj = _proj(h, g, w_in.astype(_BF16), rope, d, 2 * d, seq)
    return _diff_attn(proj, lam_params, subln_g, batch, seq, d, lam_init), w_out.astype(_BF16)


def _dil_attn_kernel(q_ref, kp_ref, kc_ref, vp_ref, vc_ref, o_ref, lse_ref, kcat_ref, vcat_ref, s_ref, p_ref, *,
                     n_steps, n_pairs):
    band = DIL_BAND
    n_heads = 2 * n_pairs
    kcat_ref[0:band, :] = kp_ref[...]
    kcat_ref[band:, :] = kc_ref[...]
    vcat_ref[0:band, :] = vp_ref[...]
    vcat_ref[band:, :] = vc_ref[...]
    row = lax.broadcasted_iota(jnp.int32, (band, 2 * band), 0)
    col = lax.broadcasted_iota(jnp.int32, (band, 2 * band), 1)
    in_band = jnp.logical_and(col <= row + band, col >= row + band - n_steps)
    first_valid = jnp.logical_and(in_band, jnp.logical_or(col >= band, pl.program_id(1) > 0))
    lane = lax.broadcasted_iota(jnp.int32, (band, LANES), 1)
    lo = lane < HEAD_DIM
    lo_v = lax.broadcasted_iota(jnp.int32, (2 * band, LANES), 1) < HEAD_DIM
    for r in range(q_ref.shape[0] // band):
        rows = slice(r * band, (r + 1) * band)
        win = slice(r * band, (r + 2) * band)
        valid = first_valid if r == 0 else in_band
        for hp in range(n_pairs):
            sl = slice(hp * LANES, (hp + 1) * LANES)
            q = q_ref[rows, sl]
            zero = jnp.zeros_like(q)
            kw = kcat_ref[win, sl]
            for e, qm in enumerate((jnp.where(lo, q, zero), jnp.where(lo, zero, q))):
                s_ref[(2 * hp + e) * band:(2 * hp + e + 1) * band, :] = jnp.where(valid, _dot_nt(qm, kw), -jnp.inf)
        s = s_ref[...]
        m = jnp.max(s, axis=1, keepdims=True)
        p_ref[...] = jnp.exp2((s - m).astype(_BF16))
        m_tile = jnp.zeros((band, LANES), _F32)
        l_tile = jnp.ones((band, LANES), _F32)
        for hp in range(n_pairs):
            sl = slice(hp * LANES, (hp + 1) * LANES)
            vw = vcat_ref[win, sl]
            one = jnp.ones_like(vw)
            ha, hb = slice((2 * hp) * band, (2 * hp + 1) * band), slice((2 * hp + 1) * band, (2 * hp + 2) * band)
            pv_a = _dot(p_ref[ha, :], jnp.where(lo_v, vw, one))
            pv_b = _dot(p_ref[hb, :], jnp.where(lo_v, one, vw))
            l_a, l_b = pltpu.roll(pv_a, HEAD_DIM, axis=1), pltpu.roll(pv_b, HEAD_DIM, axis=1)
            o_ref[rows, sl] = (jnp.where(lo, pv_a, pv_b) / jnp.where(lo, l_a, l_b)).astype(o_ref.dtype)
            m_tile = jnp.where(lane == 2 * hp, m[ha], jnp.where(lane == 2 * hp + 1, m[hb], m_tile))
            l_tile = jnp.where(lane == 2 * hp, l_a, jnp.where(lane == 2 * hp + 1, pv_b, l_tile))
        lse_ref[rows, :] = (m_tile + jnp.log2(l_tile)) * (1.0 / LOG2E)


def _dil_attn(q, q_col, k, k_col, v, v_col, n_seq, n_sub, d, n_steps):
    assert n_steps <= DIL_BAND
    rb = min(ROW_TILE, n_sub)
    nb = n_sub // rb
    bands = rb // DIL_BAND
    n_heads = d // HEAD_DIM
    cur = lambda col: (lambda x, n: (x * nb + n, col))
    prev = lambda col: (lambda x, n: ((x * nb + n) * bands - jnp.where(n > 0, 1, 0), col))
    return pl.pallas_call(
        functools.partial(_dil_attn_kernel, n_steps=n_steps, n_pairs=d // LANES),
        out_shape=(jax.ShapeDtypeStruct((n_seq * n_sub, d), _BF16),
                   jax.ShapeDtypeStruct((n_seq * n_sub, LANES), _F32)),
        grid=(n_seq, nb),
        in_specs=[pl.BlockSpec((rb, d), cur(q_col)),
                  pl.BlockSpec((DIL_BAND, d), prev(k_col)), pl.BlockSpec((rb, d), cur(k_col)),
                  pl.BlockSpec((DIL_BAND, d), prev(v_col)), pl.BlockSpec((rb, d), cur(v_col))],
        out_specs=(pl.BlockSpec((rb, d), cur(0)), pl.BlockSpec((rb, LANES), cur(0))),
        scratch_shapes=[pltpu.VMEM((rb + DIL_BAND, d), _BF16), pltpu.VMEM((rb + DIL_BAND, d), _BF16),
                        pltpu.VMEM((n_heads * DIL_BAND, 2 * DIL_BAND), _F32),
                        pltpu.VMEM((n_heads * DIL_BAND, 2 * DIL_BAND), _BF16)],
        compiler_params=_params(2),
        name="dil_attn",
    )(q, k, k, v, v)


def _dil_proj_kernel(h_ref, g_ref, w_ref, ra_ref, rb_ref, rc_ref, *refs, d_model, dilations):
    out_refs, ybuf = refs[:len(dilations)], refs[len(dilations)]
    n_groups = len(dilations)
    tm = h_ref.shape[0]
    xn = _rms_norm(h_ref[...], g_ref[...], NORM_EPS).astype(_BF16)
    half = ROT_DIM // 2
    n_slab = 0
    for c in range(0, w_ref.shape[1], LANES * 2):
        y = _dot(xn, w_ref[:, c:c + 2 * LANES])
        for s in range(0, 2 * LANES, LANES):
            src, col = divmod(c + s, d_model)
            ys = y[:, s:s + LANES]
            if src <= n_groups:
                ys = (ys * ra_ref[...] + pltpu.roll(ys, LANES - half, axis=1) * rb_ref[...]
                      + pltpu.roll(ys, half, axis=1) * rc_ref[...])
            if src < n_groups:
                ys = ys * QK_SCALE
            dests = [(src, 0)] if src < n_groups else [(g, src - n_groups + 1) for g in range(n_groups)]
            slab = ybuf.at[n_slab % ybuf.shape[0]]
            n_slab += 1
            if any(dilations[g] > 1 for g, _ in dests):
                slab[...] = ys
            for g, blk in dests:
                dil = dilations[g]
                lanes = slice(blk * d_model + col, blk * d_model + col + LANES)
                if dil == 1:
                    out_refs[g][0, :, lanes] = ys.astype(_BF16)
                else:
                    for r in range(dil):
                        out_refs[g][r, :, lanes] = slab[pl.ds(r, tm // dil, stride=dil), :].astype(_BF16)


def _dil_proj(h, g, w, rope, batch, seq, dilations):
    t, d = h.shape
    tm = min(ROW_TILE, seq)
    nblk = seq // tm
    rope_spec = pl.BlockSpec((tm, LANES), lambda i: (i % nblk, 0))
    out_shapes = tuple(jax.ShapeDtypeStruct((batch, dil, seq // dil, 3 * d), _BF16) for dil in dilations)
    out_specs = tuple(pl.BlockSpec((None, dil, tm // dil, 3 * d), lambda i: (i // nblk, 0, i % nblk, 0))
                      for dil in dilations)
    return pl.pallas_call(
        functools.partial(_dil_proj_kernel, d_model=d, dilations=dilations),
        out_shape=out_shapes,
        grid=(t // tm,),
        in_specs=[_row_spec(tm, d), _const_spec((1, d)), _const_spec(w.shape), rope_spec, rope_spec, rope_spec],
        out_specs=out_specs,
        scratch_shapes=[pltpu.VMEM((4, tm, LANES), _F32)],
        compiler_params=_params(1),
        name="dil_proj",
    )(h, g.reshape(1, d), w, *rope)


def _dil_merge_kernel(h_ref, *refs, dilations):
    n = len(dilations)
    o_refs, l_refs = refs[:n], refs[n:2 * n]
    e_ref, w_ref, out_ref, obuf, lbuf, acc_ref = refs[2 * n:]
    tm, d = h_ref.shape

    def by_position(src_ref, dil, lanes, buf):
        if dil == 1:
            return src_ref[0, :, lanes].astype(_F32)
        for r in range(dil):
            buf[pl.ds(r, tm // dil, stride=dil), :] = src_ref[r, :, lanes].astype(_F32)
        return buf[...]

    lses = [by_position(l_refs[gi], dil, slice(0, LANES), lbuf.at[gi]) for gi, dil in enumerate(dilations)]
    m = functools.reduce(jnp.maximum, lses)
    es = [jnp.exp(l - m) for l in lses]
    den = functools.reduce(lambda a, b: a + b, es)
    wides = [_dot(_packed_pieces(e / den, d // HEAD_DIM), e_ref[...]) for e in es]
    for c in range(0, d, LANES):
        lanes = slice(c, c + LANES)
        acc = jnp.zeros((tm, LANES), _F32)
        for gi, dil in enumerate(dilations):
            acc = acc + wides[gi][:, lanes] * by_position(o_refs[gi], dil, lanes, obuf.at[(c // LANES) % 2, gi])
        acc_ref[:, lanes] = acc.astype(_BF16)
    out_ref[...] = h_ref[...] + _dot(acc_ref[...], w_ref[...])


def _dil_merge(h, outs, lses, w_out, batch, seq, dilations):
    t, d = h.shape
    tm = min(ROW_TILE, seq)
    nblk = seq // tm
    n_heads = d // HEAD_DIM
    piece_row = jnp.arange(LANES)
    expand = ((piece_row % n_heads)[:, None] == (jnp.arange(d) // HEAD_DIM)[None, :]) & (piece_row < 3 * n_heads)[:, None]
    res_spec = lambda dil, n: pl.BlockSpec((None, dil, tm // dil, n), lambda i: (i // nblk, 0, i % nblk, 0))
    n = len(dilations)
    return pl.pallas_call(
        functools.partial(_dil_merge_kernel, dilations=dilations),
        out_shape=jax.ShapeDtypeStruct((t, d), _F32),
        grid=(t // tm,),
        in_specs=([_row_spec(tm, d)] + [res_spec(dil, d) for dil in dilations] + [res_spec(dil, LANES) for dil in dilations]
                  + [_const_spec((LANES, d)), _const_spec(w_out.shape)]),
        out_specs=_row_spec(tm, d),
        scratch_shapes=[pltpu.VMEM((2, n, tm, LANES), _F32), pltpu.VMEM((n, tm, LANES), _F32),
                        pltpu.VMEM((tm, d), _BF16)],
        compiler_params=_params(1),
        name="dil_merge",
    )(h, *outs, *lses, expand.astype(_BF16), w_out)


def _dil_mixer(h, g, w_in, w_out, rope, batch, seq):
    t, d = h.shape
    dilations = tuple(dil for _, dil in DIL_CONFIGS)
    projs = _dil_proj(h, g, w_in.astype(_BF16), rope, batch, seq, dilations)
    outs, lses = [], []
    for (window, dil), proj in zip(DIL_CONFIGS, projs):
        assert seq % (dil * DIL_BAND) == 0
        proj = proj.reshape(t, 3 * d)
        o, lse = _dil_attn(proj, 0, proj, 1, proj, 2, batch * dil, seq // dil, d, window // dil)
        outs.append(o.reshape(batch, dil, seq // dil, d))
        lses.append(lse.reshape(batch, dil, seq // dil, LANES))
    return _dil_merge(h, outs, lses, w_out.astype(_BF16), batch, seq, dilations)


def _gelu(x):
    return 0.5 * x * (1.0 + lax.erf(x * math.sqrt(0.5)))


def _sgu_kernel(h_ref, g_ref, win_ref, nv_ref, ws_ref, bs_ref, wout_ref, o_ref, u_ref, v_ref, y_ref, *, chunk):
    x = h_ref[...]
    tm = x.shape[0]
    half = wout_ref.shape[0]
    gdim = half // SGU_GROUPS
    xn = _rms_norm(x, g_ref[...], NORM_EPS).astype(_BF16)
    ssq = jnp.zeros((tm, 1), _F32)
    for c in range(0, half, chunk):
        u_ref[:, c:c + chunk] = _gelu(_dot(xn, win_ref[:, c:c + chunk]))
        vv = _gelu(_dot(xn, win_ref[:, half + c:half + c + chunk]))
        v_ref[:, c:c + chunk] = vv
        ssq = ssq + jnp.sum(vv * vv, axis=1, keepdims=True)
    rstd = lax.rsqrt(ssq / half + NORM_EPS)
    row = lax.broadcasted_iota(jnp.int32, (SGU_CHUNK, SGU_CHUNK), 0)
    col = lax.broadcasted_iota(jnp.int32, (SGU_CHUNK, SGU_CHUNK), 1)
    for gi in range(SGU_GROUPS):
        cols = slice(gi * gdim, (gi + 1) * gdim)
        w = jnp.where(col <= row, ws_ref[gi], jnp.zeros((), _BF16))
        bias = bs_ref[:, gi:gi + 1]
        for r in range(0, tm, SGU_CHUNK):
            rows = slice(r, r + SGU_CHUNK)
            vn = (v_ref[rows, cols] * rstd[rows] * nv_ref[:, cols]).astype(_BF16)
            y_ref[rows, cols] = (u_ref[rows, cols] * (_dot(w, vn) + bias)).astype(_BF16)
    o_ref[...] = x + _dot(y_ref[...], wout_ref[...])


def _sgu_mixer(h, g, w_in, norm_v, w_s, b_s, w_out):
    t, d = h.shape
    half = w_out.shape[0]
    tm = min(ROW_TILE, t)
    assert tm % SGU_CHUNK == 0 and (half // SGU_GROUPS) % LANES == 0
    return pl.pallas_call(
        functools.partial(_sgu_kernel, chunk=_col_chunk(half)),
        out_shape=jax.ShapeDtypeStruct((t, d), _F32),
        grid=(t // tm,),
        in_specs=[_row_spec(tm, d), _const_spec((1, d)), _const_spec(w_in.shape), _const_spec((1, half)),
                  _const_spec(w_s.shape), _const_spec((SGU_CHUNK, SGU_GROUPS)), _const_spec(w_out.shape)],
        out_specs=_row_spec(tm, d),
        scratch_shapes=[pltpu.VMEM((tm, half), _F32), pltpu.VMEM((tm, half), _F32), pltpu.VMEM((tm, half), _BF16)],
        compiler_params=_params(1),
        name="sgu",
    )(h, g.reshape(1, d), w_in.astype(_BF16), norm_v.reshape(1, half), w_s.astype(_BF16), jnp.transpose(b_s),
      w_out.astype(_BF16))


def kernel(x, p, norm_ffn1, w_ffn1_in, w_ffn1_out, norm_mix, norm_ffn2, w_ffn2_in, w_ffn2_out, norm_ple, w_ple_gate, b_ple_gate, w_ple_proj, fox_w_in, fox_b_f, fox_w_out, dil_w_in, dil_w_out, diff_w_in, diff_lambda, diff_subln, diff_w_out, sgu_w_in, sgu_norm_v, sgu_w_s, sgu_b_s, sgu_w_out, norm_final):
    batch, seq, d = x.shape
    depth = p.shape[0]
    t = batch * seq
    h = x.reshape(t, d)
    rope = _rope_tables(seq)
    n_mixers = 4
    for i in range(depth):
        kind, j = i % n_mixers, i // n_mixers
        h = _ffn(h, norm_ffn1[i], w_ffn1_in[i].astype(_BF16), w_ffn1_out[i].astype(_BF16))
        attn = None
        if kind == 0:
            attn = _fox_mixer(h, norm_mix[i], fox_w_in[j], fox_b_f[j], fox_w_out[j], batch, seq)
        elif kind == 1:
            h = _dil_mixer(h, norm_mix[i], dil_w_in[j], dil_w_out[j], rope, batch, seq)
        elif kind == 2:
            attn = _diff_mixer(h, norm_mix[i], diff_w_in[j], diff_lambda[j], diff_subln[j], diff_w_out[j], rope,
                               batch, seq, i)
        else:
            h = _sgu_mixer(h, norm_mix[i], sgu_w_in[j], sgu_norm_v[j], sgu_w_s[j], sgu_b_s[j], sgu_w_out[j])
        h = _post_mixer(h, attn, norm_ffn2[i], w_ffn2_in[i].astype(_BF16), w_ffn2_out[i].astype(_BF16),
                        p.reshape(depth, t, p.shape[-1]), i, norm_ple[i], w_ple_gate[i].astype(_BF16), b_ple_gate[i],
                        w_ple_proj[i].astype(_BF16), norm_final, final_norm=(i == depth - 1))
    return h.reshape(batch, seq, d)
```

```python
import functools
import math

import jax
import jax.numpy as jnp
import numpy as np
from jax import lax
from jax.experimental import pallas as pl
from jax.experimental.pallas import tpu as pltpu

_F32 = jnp.float32
_BF16 = jnp.bfloat16

HEAD_DIM = 64
ROPE_THETA = 500000.0
ROT_DIM = HEAD_DIM // 4
NORM_EPS = 1e-6
DIL_CONFIGS = ((128, 1), (512, 4), (2048, 16))
DIL_BAND = 128
DIFF_SUBLN_EPS = 1e-5
SGU_GROUPS = 8
SGU_CHUNK = 128

LANES = 128
VMEM_LIMIT_BYTES = 56 * 1024 * 1024
ROW_TILE = 512
ATTN_BLOCK = 512
COL_CHUNK = 256
LOG2E = math.log2(math.e)
QK_SCALE = HEAD_DIM ** -0.5 * LOG2E


def _params(n_axes):
    return pltpu.CompilerParams(dimension_semantics=("arbitrary",) * n_axes,
                                vmem_limit_bytes=VMEM_LIMIT_BYTES)


def _col_chunk(n, target=COL_CHUNK):
    best = LANES
    for c in range(LANES, target + 1, LANES):
        if n % c == 0:
            best = c
    assert n % best == 0, (n, best)
    return best


def _row_spec(tm, n):
    return pl.BlockSpec((tm, n), lambda i: (i, 0))


def _const_spec(shape):
    return pl.BlockSpec(shape, lambda i: (0,) * len(shape))


def _rms_norm(x, g, eps):
    return x * lax.rsqrt(jnp.mean(x * x, axis=-1, keepdims=True) + eps) * g


def _dot(a, b):
    return jnp.dot(a, b, preferred_element_type=_F32)


def _dot_nt(a, b):
    return lax.dot_general(a, b, (((1,), (1,)), ((), ())), preferred_element_type=_F32)


def _split3(x):
    hi = x.astype(_BF16)
    r1 = x - hi.astype(_F32)
    mid = r1.astype(_BF16)
    lo = (r1 - mid.astype(_F32)).astype(_BF16)
    return hi, mid, lo


def _packed_pieces(x, n_heads):
    assert 3 * n_heads <= LANES
    lane = lax.broadcasted_iota(jnp.int32, x.shape, 1)
    packed = jnp.zeros_like(x)
    for n, piece in enumerate(_split3(x)):
        piece = jnp.where(lane < n_heads, piece.astype(_F32), 0.0)
        packed = packed + (pltpu.roll(piece, n * n_heads, axis=1) if n else piece)
    return packed.astype(_BF16)


def _ffn_kernel(h_ref, g_ref, win_ref, wout_ref, o_ref, hid_ref, *, d_ff, chunk):
    x = h_ref[...]
    xn = _rms_norm(x, g_ref[...], NORM_EPS).astype(_BF16)
    for c in range(0, d_ff, chunk):
        gate = _dot(xn, win_ref[:, c:c + chunk])
        up = _dot(xn, win_ref[:, d_ff + c:d_ff + c + chunk])
        hid_ref[:, c:c + chunk] = (gate * jax.nn.sigmoid(gate) * up).astype(_BF16)
    o_ref[...] = x + 0.5 * _dot(hid_ref[...], wout_ref[...])


def _layer_spec(stacked, layer):
    zeros = (0,) * (stacked.ndim - 1)
    return pl.BlockSpec((None,) + stacked.shape[1:], lambda i: (layer,) + zeros, pipeline_mode=pl.Buffered(1))


def _ffn(h, g, w_in, w_out, layer):
    t, d = h.shape
    d_ff = w_out.shape[1]
    tm = min(ROW_TILE, t)
    return pl.pallas_call(
        functools.partial(_ffn_kernel, d_ff=d_ff, chunk=_col_chunk(d_ff)),
        out_shape=jax.ShapeDtypeStruct((t, d), _F32),
        grid=(t // tm,),
        in_specs=[_row_spec(tm, d), _const_spec((1, d)), _layer_spec(w_in, layer), _layer_spec(w_out, layer)],
        out_specs=_row_spec(tm, d),
        scratch_shapes=[pltpu.VMEM((tm, d_ff), _BF16)],
        compiler_params=_params(1),
        name="ffn",
    )(h, g.reshape(1, d), w_in, w_out)


def _post_kernel(*refs, has_attn, final_norm, d_ff, chunk):
    h_ref = refs[0]
    o_ref, wo_ref = refs[1:3] if has_attn else (None, None)
    (gf2_ref, win_ref, wout_ref, p_ref, gp_ref, wg_ref, bg_ref, wp_ref, gfin_ref,
     out_ref, hid_ref) = refs[3 if has_attn else 1:]
    x = h_ref[...]
    if has_attn:
        x = x + _dot(o_ref[...], wo_ref[...])
    xn = _rms_norm(x, gf2_ref[...], NORM_EPS).astype(_BF16)
    for c in range(0, d_ff, chunk):
        gate = _dot(xn, win_ref[:, c:c + chunk])
        up = _dot(xn, win_ref[:, d_ff + c:d_ff + c + chunk])
        hid_ref[:, c:c + chunk] = (gate * jax.nn.sigmoid(gate) * up).astype(_BF16)
    x = x + 0.5 * _dot(hid_ref[...], wout_ref[...])
    xn = _rms_norm(x, gp_ref[...], NORM_EPS).astype(_BF16)
    gate = jax.nn.sigmoid(_dot(xn, wg_ref[...]) + bg_ref[...])
    x = x + gate * _dot(p_ref[...].astype(_BF16), wp_ref[...])
    if final_norm:
        x = _rms_norm(x, gfin_ref[...], NORM_EPS)
    out_ref[...] = x


def _post_mixer(h, attn, g_ffn, w_in, w_out, p, layer, g_ple, w_gate, b_gate, w_proj, g_final, final_norm):
    t, d = h.shape
    d_ff = w_out.shape[1]
    dp = p.shape[2]
    tm = min(ROW_TILE, t)
    args, specs = [h], [_row_spec(tm, d)]
    if attn is not None:
        args += list(attn)
        specs += [_row_spec(tm, attn[0].shape[1]), _const_spec(attn[1].shape)]
    args += [g_ffn.reshape(1, d), w_in, w_out, p, g_ple.reshape(1, d), w_gate, b_gate.reshape(1, d), w_proj,
             g_final.reshape(1, d)]
    specs += [_const_spec((1, d)), _layer_spec(w_in, layer), _layer_spec(w_out, layer),
              pl.BlockSpec((None, tm, dp), lambda i: (layer, i, 0)),
              _const_spec((1, d)), _layer_spec(w_gate, layer), _const_spec((1, d)), _layer_spec(w_proj, layer),
              _const_spec((1, d))]
    return pl.pallas_call(
        functools.partial(_post_kernel, has_attn=attn is not None, final_norm=final_norm, d_ff=d_ff,
                          chunk=_col_chunk(d_ff)),
        out_shape=jax.ShapeDtypeStruct((t, d), _F32),
        grid=(t // tm,),
        in_specs=specs,
        out_specs=_row_spec(tm, d),
        scratch_shapes=[pltpu.VMEM((tm, d_ff), _BF16)],
        compiler_params=_params(1),
        name="post_mixer",
    )(*args)


def _rope_tables(s):
    half = ROT_DIM // 2
    inv_freq = ROPE_THETA ** (-jnp.arange(0, ROT_DIM, 2, dtype=_F32) / ROT_DIM)
    ang = jnp.arange(s, dtype=jnp.int32).astype(_F32)[:, None] * inv_freq[None, :]
    cos, sin = jnp.cos(ang), jnp.sin(ang)
    m = jnp.arange(LANES) % HEAD_DIM
    idx = m % half
    first, second = m < half, (m >= half) & (m < ROT_DIM)
    a = jnp.where((first | second)[None, :], cos[:, idx], 1.0)
    b = jnp.where(first[None, :], -sin[:, idx], 0.0)
    c = jnp.where(second[None, :], sin[:, idx], 0.0)
    return a.astype(_F32), b.astype(_F32), c.astype(_F32)


def _proj_kernel(h_ref, g_ref, w_ref, ra_ref, rb_ref, rc_ref, o_ref, *, n_query, n_rope, chunk):
    xn = _rms_norm(h_ref[...], g_ref[...], NORM_EPS).astype(_BF16)
    half = ROT_DIM // 2
    for c in range(0, w_ref.shape[1], chunk):
        y = _dot(xn, w_ref[:, c:c + chunk])
        if c < n_rope:
            for s in range(0, chunk, LANES):
                ys = y[:, s:s + LANES]
                ys = (ys * ra_ref[...] + pltpu.roll(ys, LANES - half, axis=1) * rb_ref[...]
                      + pltpu.roll(ys, half, axis=1) * rc_ref[...])
                if c < n_query:
                    ys = ys * QK_SCALE
                o_ref[:, c + s:c + s + LANES] = ys.astype(o_ref.dtype)
        else:
            o_ref[:, c:c + chunk] = y.astype(o_ref.dtype)


def _proj(h, g, w, rope, n_query, n_rope, seq):
    t, d = h.shape
    n = w.shape[1]
    tm = min(ROW_TILE, seq)
    chunk = _col_chunk(n)
    assert n_rope % chunk == 0 and n_query % chunk == 0 and n_query <= n_rope
    nblk = seq // tm
    rope_spec = pl.BlockSpec((tm, LANES), lambda i: (i % nblk, 0))
    return pl.pallas_call(
        functools.partial(_proj_kernel, n_query=n_query, n_rope=n_rope, chunk=chunk),
        out_shape=jax.ShapeDtypeStruct((t, n), _BF16),
        grid=(t // tm,),
        in_specs=[_row_spec(tm, d), _const_spec((1, d)), _const_spec(w.shape), rope_spec, rope_spec, rope_spec],
        out_specs=_row_spec(tm, n),
        compiler_params=_params(1),
        name="proj",
    )(h, g.reshape(1, d), w, *rope)


def _fox_proj_kernel(h_ref, g_ref, w_ref, wf_ref, bf_ref, qkv_ref, lf_ref, *, n_query, chunk):
    xn = _rms_norm(h_ref[...], g_ref[...], NORM_EPS).astype(_BF16)
    for c in range(0, w_ref.shape[1], chunk):
        y = _dot(xn, w_ref[:, c:c + chunk])
        if c < n_query:
            y = y * QK_SCALE
        qkv_ref[:, c:c + chunk] = y.astype(qkv_ref.dtype)
    z = _dot(xn, wf_ref[...]) + bf_ref[...]
    lf_ref[...] = jnp.minimum(z, 0.0) - jnp.log1p(jnp.exp(-jnp.abs(z)))


def _fox_proj(h, g, w_qkv, w_f, b_f):
    t, d = h.shape
    n = w_qkv.shape[1]
    tm = min(ROW_TILE, t)
    return pl.pallas_call(
        functools.partial(_fox_proj_kernel, n_query=d, chunk=_col_chunk(n)),
        out_shape=(jax.ShapeDtypeStruct((t, n), _BF16), jax.ShapeDtypeStruct((t, LANES), _F32)),
        grid=(t // tm,),
        in_specs=[_row_spec(tm, d), _const_spec((1, d)), _const_spec(w_qkv.shape), _const_spec(w_f.shape),
                  _const_spec((1, LANES))],
        out_specs=(_row_spec(tm, n), _row_spec(tm, LANES)),
        compiler_params=_params(1),
        name="fox_proj",
    )(h, g.reshape(1, d), w_qkv, w_f, b_f)


def _bias_placement(d):
    pairs = d // LANES
    n_heads = 2 * pairs
    w = np.zeros((3, LANES, d), np.float32)
    ones = np.zeros((3, 1, d), np.float32)
    for hp in range(pairs):
        base = hp * LANES
        for p in range(3):
            w[0, p * n_heads + 2 * hp, base + p] = 1.0
            w[1, p * n_heads + 2 * hp + 1, base + 6 + p] = 1.0
            w[2, p * n_heads + 2 * hp, base + 3 + p] = -1.0
            w[2, p * n_heads + 2 * hp + 1, base + 9 + p] = -1.0
            ones[0, 0, base + 3 + p] = 1.0
            ones[1, 0, base + 9 + p] = 1.0
            ones[2, 0, base + p] = 1.0
            ones[2, 0, base + 6 + p] = 1.0
    return jnp.asarray(w, _BF16), jnp.asarray(ones, _F32)


def _cumsum_kernel(x_ref, w_ref, ones_ref, qea_ref, qeb_ref, ke_ref, carry_ref, *, n_heads):
    @pl.when(pl.program_id(1) == 0)
    def _():
        carry_ref[...] = jnp.zeros_like(carry_ref)

    x = x_ref[...]
    ts = x.shape[0]
    row = lax.broadcasted_iota(jnp.int32, (ts, ts), 0)
    col = lax.broadcasted_iota(jnp.int32, (ts, ts), 1)
    tri = jnp.where(col <= row, 1.0, 0.0).astype(_BF16)
    hi, mid, lo = _split3(x)
    c = _dot(tri, hi) + _dot(tri, mid) + _dot(tri, lo) + carry_ref[0:1, :]
    carry_ref[0:1, :] = c[ts - 1:ts, :]
    pieces = _packed_pieces(c * LOG2E, n_heads)
    for n, out_ref in enumerate((qea_ref, qeb_ref, ke_ref)):
        out_ref[...] = (_dot(pieces, w_ref[n]) + ones_ref[n]).astype(out_ref.dtype)


def _cumsum_bias(x, batch, seq, d, ts):
    ns = seq // ts
    w, ones = _bias_placement(d)
    row_spec = pl.BlockSpec((ts, d), lambda b, s: (b * ns + s, 0))
    out = jax.ShapeDtypeStruct((batch * seq, d), _BF16)
    return pl.pallas_call(
        functools.partial(_cumsum_kernel, n_heads=d // HEAD_DIM),
        out_shape=(out, out, out),
        grid=(batch, ns),
        in_specs=[pl.BlockSpec((ts, LANES), lambda b, s: (b * ns + s, 0)),
                  pl.BlockSpec(w.shape, lambda b, s: (0, 0, 0)), pl.BlockSpec(ones.shape, lambda b, s: (0, 0, 0))],
        out_specs=(row_spec, row_spec, row_spec),
        scratch_shapes=[pltpu.VMEM((8, LANES), _F32)],
        compiler_params=_params(2),
        name="cumsum",
    )(x, w, ones)


def _flash_items(nq, blk):
    items = []
    for i in range(nq):
        for j in range(i // 2):
            items.append((i, 2 * j * blk, 2 * blk, False))
        if i % 2 == 1:
            items.append((i, (i - 1) * blk, 2 * blk, True))
        else:
            items.append((i, i * blk, blk, True))
    return items


def _causal_flash(query_maps, k_refs, value_maps, finish, nq, blk, scratch):
    s_bufs, m_ref, acc_ref = scratch[:2], scratch[2], scratch[3]
    items = _flash_items(nq, blk)

    def produce(item, s_ref):
        i, start, width, _ = item
        parts = [r[start:start + width, :] for r in k_refs]
        kb = parts[0] if len(parts) == 1 else jnp.concatenate(parts, axis=1)
        for e, qm in enumerate(query_maps(i)):
            s_ref[e, :, :width] = _dot_nt(qm, kb)

    def consume(item, s_ref, first):
        i, start, width, masked = item
        vbs = value_maps(start, width)
        for e in range(2):
            s = s_ref[e, :, :width]
            if masked:
                row = lax.broadcasted_iota(jnp.int32, (blk, width), 0)
                col = lax.broadcasted_iota(jnp.int32, (blk, width), 1)
                s = jnp.where(col - row <= i * blk - start, s, -jnp.inf)
            m_new = jnp.max(s, axis=1, keepdims=True)
            if not first:
                m_old = m_ref[e]
                m_new = jnp.maximum(m_old, m_new)
            pv = _dot(jnp.exp2((s - m_new).astype(_BF16)), vbs[e])
            acc_ref[e] = pv if first else acc_ref[e] * jnp.exp2(m_old - m_new) + pv
            m_ref[e] = m_new

    produce(items[0], s_bufs[0])
    for t, item in enumerate(items):
        if t + 1 < len(items):
            produce(items[t + 1], s_bufs[(t + 1) % 2])
        consume(item, s_bufs[t % 2], first=(t == 0 or items[t - 1][0] != item[0]))
        if t + 1 == len(items) or items[t + 1][0] != item[0]:
            finish(item[0], acc_ref[0], acc_ref[1])


def _flash_scratch(blk, n_out):
    return [pltpu.VMEM((2, blk, 2 * blk), _F32), pltpu.VMEM((2, blk, 2 * blk), _F32),
            pltpu.VMEM((2, blk, 1), _F32), pltpu.VMEM((2, blk, n_out), _F32)]


def _split_maps(q, lo):
    zero = jnp.zeros_like(q)
    return jnp.where(lo, q, zero), jnp.where(lo, zero, q)


def _fox_attn_kernel(q_ref, qea_ref, qeb_ref, k_ref, ke_ref, v_ref, o_ref, *scratch, blk):
    lo = lax.broadcasted_iota(jnp.int32, (blk, LANES), 1) < HEAD_DIM

    def query_maps(i):
        rows = slice(i * blk, (i + 1) * blk)
        qa, qb = _split_maps(q_ref[rows, :], lo)
        return jnp.concatenate([qa, qea_ref[rows, :]], axis=1), jnp.concatenate([qb, qeb_ref[rows, :]], axis=1)

    def value_maps(start, width):
        vb = v_ref[start:start + width, :]
        keep = lax.broadcasted_iota(jnp.int32, (width, LANES), 1) < HEAD_DIM
        one = jnp.ones_like(vb)
        return jnp.where(keep, vb, one), jnp.where(keep, one, vb)

    def finish(i, acc_a, acc_b):
        out_a = acc_a / pltpu.roll(acc_a, HEAD_DIM, axis=1)
        out_b = acc_b / pltpu.roll(acc_b, HEAD_DIM, axis=1)
        o_ref[i * blk:(i + 1) * blk, :] = jnp.where(lo, out_a, out_b).astype(o_ref.dtype)

    _causal_flash(query_maps, (k_ref, ke_ref), value_maps, finish, q_ref.shape[0] // blk, blk, scratch)


def _fox_attn(qkv, qea, qeb, ke, batch, seq, d):
    blk = min(ATTN_BLOCK, seq)
    pairs = d // LANES
    spec = lambda col0: pl.BlockSpec((seq, LANES), lambda b, h: (b, col0 + h))
    return pl.pallas_call(
        functools.partial(_fox_attn_kernel, blk=blk),
        out_shape=jax.ShapeDtypeStruct((batch * seq, d), _BF16),
        grid=(batch, pairs),
        in_specs=[spec(0), spec(0), spec(0), spec(pairs), spec(0), spec(2 * pairs)],
        out_specs=spec(0),
        scratch_shapes=_flash_scratch(blk, LANES),
        compiler_params=_params(2),
        name="fox_attn",
    )(qkv, qea, qeb, qkv, ke, qkv)


def _fox_mixer(h, g, w_in, b_f, w_out, batch, seq):
    t, d = h.shape
    n_heads = d // HEAD_DIM
    w_qkv = w_in[:, :3 * d].astype(_BF16)
    w_f = jnp.pad(w_in[:, 3 * d:], ((0, 0), (0, LANES - n_heads))).astype(_BF16)
    b_pad = jnp.pad(b_f, (0, LANES - n_heads)).reshape(1, LANES)
    qkv, log_f = _fox_proj(h, g, w_qkv, w_f, b_pad)
    qea, qeb, ke = _cumsum_bias(log_f, batch, seq, d, min(ATTN_BLOCK, seq))
    return _fox_attn(qkv, qea, qeb, ke, batch, seq, d), w_out.astype(_BF16)


def _diff_attn_kernel(q_ref, k_ref, v_ref, lam_ref, g_ref, o_ref, *scratch, blk, lam_init):
    lo = lax.broadcasted_iota(jnp.int32, (blk, LANES), 1) < HEAD_DIM
    lp = lam_ref[...]
    lam = (jnp.exp(jnp.sum(lp[0:1] * lp[1:2], axis=1, keepdims=True))
           - jnp.exp(jnp.sum(lp[2:3] * lp[3:4], axis=1, keepdims=True)) + lam_init)

    def query_maps(i):
        return _split_maps(q_ref[i * blk:(i + 1) * blk, :], lo)

    def value_maps(start, width):
        vb = v_ref[start:start + width, :]
        vb = jnp.concatenate([vb, jnp.ones_like(vb)], axis=1)
        return vb, vb

    def finish(i, acc_1, acc_2):
        o = acc_1[:, :LANES] / acc_1[:, LANES:] - lam * (acc_2[:, :LANES] / acc_2[:, LANES:])
        o = _rms_norm(o, g_ref[...], DIFF_SUBLN_EPS) * (1.0 - lam_init)
        o_ref[i * blk:(i + 1) * blk, :] = o.astype(o_ref.dtype)

    _causal_flash(query_maps, (k_ref,), value_maps, finish, q_ref.shape[0] // blk, blk, scratch)


def _diff_attn(proj, lam_params, subln_g, batch, seq, d, lam_init):
    blk = min(ATTN_BLOCK, seq)
    heads = d // LANES
    spec = lambda col0: pl.BlockSpec((seq, LANES), lambda b, h: (b, col0 + h))
    return pl.pallas_call(
        functools.partial(_diff_attn_kernel, blk=blk, lam_init=lam_init),
        out_shape=jax.ShapeDtypeStruct((batch * seq, d), _BF16),
        grid=(batch, heads),
        in_specs=[spec(0), spec(heads), spec(2 * heads),
                  pl.BlockSpec(lam_params.shape, lambda b, h: (0, 0)), pl.BlockSpec((1, LANES), lambda b, h: (0, 0))],
        out_specs=spec(0),
        scratch_shapes=_flash_scratch(blk, 2 * LANES),
        compiler_params=_params(2),
        name="diff_attn",
    )(proj, proj, proj, lam_params, subln_g.reshape(1, LANES))


def _diff_mixer(h, g, w_in, lam_params, subln_g, w_out, rope, batch, seq, layer_idx):
    t, d = h.shape
    lam_init = 0.8 - 0.6 * math.exp(-0.3 * layer_idx)
    proj = _proj(h, g, w_in.astype(_BF16), rope, d, 2 * d, seq)
    return _diff_attn(proj, lam_params, subln_g, batch, seq, d, lam_init), w_out.astype(_BF16)


def _dil_attn_kernel(q_ref, kp_ref, kc_ref, vp_ref, vc_ref, o_ref, lse_ref, kcat_ref, vcat_ref, s_ref, p_ref, *,
                     n_steps, n_pairs):
    band = DIL_BAND
    n_heads = 2 * n_pairs
    kcat_ref[0:band, :] = kp_ref[...]
    kcat_ref[band:, :] = kc_ref[...]
    vcat_ref[0:band, :] = vp_ref[...]
    vcat_ref[band:, :] = vc_ref[...]
    row = lax.broadcasted_iota(jnp.int32, (band, 2 * band), 0)
    col = lax.broadcasted_iota(jnp.int32, (band, 2 * band), 1)
    in_band = jnp.logical_and(col <= row + band, col >= row + band - n_steps)
    first_valid = jnp.logical_and(in_band, jnp.logical_or(col >= band, pl.program_id(1) > 0))
    lane = lax.broadcasted_iota(jnp.int32, (band, LANES), 1)
    lo = lane < HEAD_DIM
    lo_v = lax.broadcasted_iota(jnp.int32, (2 * band, LANES), 1) < HEAD_DIM
    for r in range(q_ref.shape[0] // band):
        rows = slice(r * band, (r + 1) * band)
        win = slice(r * band, (r + 2) * band)
        valid = first_valid if r == 0 else in_band
        for hp in range(n_pairs):
            sl = slice(hp * LANES, (hp + 1) * LANES)
            q = q_ref[rows, sl]
            zero = jnp.zeros_like(q)
            kw = kcat_ref[win, sl]
            for e, qm in enumerate((jnp.where(lo, q, zero), jnp.where(lo, zero, q))):
                s_ref[(2 * hp + e) * band:(2 * hp + e + 1) * band, :] = jnp.where(valid, _dot_nt(qm, kw), -jnp.inf)
        s = s_ref[...]
        m = jnp.max(s, axis=1, keepdims=True)
        p_ref[...] = jnp.exp2((s - m).astype(_BF16))
        m_tile = jnp.zeros((band, LANES), _F32)
        l_tile = jnp.ones((band, LANES), _F32)
        for hp in range(n_pairs):
            sl = slice(hp * LANES, (hp + 1) * LANES)
            vw = vcat_ref[win, sl]
            one = jnp.ones_like(vw)
            ha, hb = slice((2 * hp) * band, (2 * hp + 1) * band), slice((2 * hp + 1) * band, (2 * hp + 2) * band)
            pv_a = _dot(p_ref[ha, :], jnp.where(lo_v, vw, one))
            pv_b = _dot(p_ref[hb, :], jnp.where(lo_v, one, vw))
            l_a, l_b = pltpu.roll(pv_a, HEAD_DIM, axis=1), pltpu.roll(pv_b, HEAD_DIM, axis=1)
            o_ref[rows, sl] = (jnp.where(lo, pv_a, pv_b) / jnp.where(lo, l_a, l_b)).astype(o_ref.dtype)
            m_tile = jnp.where(lane == 2 * hp, m[ha], jnp.where(lane == 2 * hp + 1, m[hb], m_tile))
            l_tile = jnp.where(lane == 2 * hp, l_a, jnp.where(lane == 2 * hp + 1, pv_b, l_tile))
        lse_ref[rows, :] = (m_tile + jnp.log2(l_tile)) * (1.0 / LOG2E)


def _dil_attn(q, q_col, k, k_col, v, v_col, n_seq, n_sub, d, n_steps):
    assert n_steps <= DIL_BAND
    rb = min(ROW_TILE, n_sub)
    nb = n_sub // rb
    bands = rb // DIL_BAND
    n_heads = d // HEAD_DIM
    cur = lambda col: (lambda x, n: (x * nb + n, col))
    prev = lambda col: (lambda x, n: ((x * nb + n) * bands - jnp.where(n > 0, 1, 0), col))
    return pl.pallas_call(
        functools.partial(_dil_attn_kernel, n_steps=n_steps, n_pairs=d // LANES),
        out_shape=(jax.ShapeDtypeStruct((n_seq * n_sub, d), _BF16),
                   jax.ShapeDtypeStruct((n_seq * n_sub, LANES), _F32)),
        grid=(n_seq, nb),
        in_specs=[pl.BlockSpec((rb, d), cur(q_col)),
                  pl.BlockSpec((DIL_BAND, d), prev(k_col)), pl.BlockSpec((rb, d), cur(k_col)),
                  pl.BlockSpec((DIL_BAND, d), prev(v_col)), pl.BlockSpec((rb, d), cur(v_col))],
        out_specs=(pl.BlockSpec((rb, d), cur(0)), pl.BlockSpec((rb, LANES), cur(0))),
        scratch_shapes=[pltpu.VMEM((rb + DIL_BAND, d), _BF16), pltpu.VMEM((rb + DIL_BAND, d), _BF16),
                        pltpu.VMEM((n_heads * DIL_BAND, 2 * DIL_BAND), _F32),
                        pltpu.VMEM((n_heads * DIL_BAND, 2 * DIL_BAND), _BF16)],
        compiler_params=_params(2),
        name="dil_attn",
    )(q, k, k, v, v)


def _dil_proj_kernel(h_ref, g_ref, w_ref, ra_ref, rb_ref, rc_ref, *refs, d_model, dilations):
    out_refs, ybuf = refs[:len(dilations)], refs[len(dilations)]
    n_groups = len(dilations)
    tm = h_ref.shape[0]
    xn = _rms_norm(h_ref[...], g_ref[...], NORM_EPS).astype(_BF16)
    half = ROT_DIM // 2
    n_slab = 0
    for c in range(0, w_ref.shape[1], LANES * 2):
        y = _dot(xn, w_ref[:, c:c + 2 * LANES])
        for s in range(0, 2 * LANES, LANES):
            src, col = divmod(c + s, d_model)
            ys = y[:, s:s + LANES]
            if src <= n_groups:
                ys = (ys * ra_ref[...] + pltpu.roll(ys, LANES - half, axis=1) * rb_ref[...]
                      + pltpu.roll(ys, half, axis=1) * rc_ref[...])
            if src < n_groups:
                ys = ys * QK_SCALE
            dests = [(src, 0)] if src < n_groups else [(g, src - n_groups + 1) for g in range(n_groups)]
            slab = ybuf.at[n_slab % ybuf.shape[0]]
            n_slab += 1
            if any(dilations[g] > 1 for g, _ in dests):
                slab[...] = ys
            for g, blk in dests:
                dil = dilations[g]
                lanes = slice(blk * d_model + col, blk * d_model + col + LANES)
                if dil == 1:
                    out_refs[g][0, :, lanes] = ys.astype(_BF16)
                else:
                    for r in range(dil):
                        out_refs[g][r, :, lanes] = slab[pl.ds(r, tm // dil, stride=dil), :].astype(_BF16)


def _dil_proj(h, g, w, rope, batch, seq, dilations):
    t, d = h.shape
    tm = min(ROW_TILE, seq)
    nblk = seq // tm
    rope_spec = pl.BlockSpec((tm, LANES), lambda i: (i % nblk, 0))
    out_shapes = tuple(jax.ShapeDtypeStruct((batch, dil, seq // dil, 3 * d), _BF16) for dil in dilations)
    out_specs = tuple(pl.BlockSpec((None, dil, tm // dil, 3 * d), lambda i: (i // nblk, 0, i % nblk, 0))
                      for dil in dilations)
    return pl.pallas_call(
        functools.partial(_dil_proj_kernel, d_model=d, dilations=dilations),
        out_shape=out_shapes,
        grid=(t // tm,),
        in_specs=[_row_spec(tm, d), _const_spec((1, d)), _const_spec(w.shape), rope_spec, rope_spec, rope_spec],
        out_specs=out_specs,
        scratch_shapes=[pltpu.VMEM((4, tm, LANES), _F32)],
        compiler_params=_params(1),
        name="dil_proj",
    )(h, g.reshape(1, d), w, *rope)


def _dil_merge_kernel(h_ref, *refs, dilations):
    n = len(dilations)
    o_refs, l_refs = refs[:n], refs[n:2 * n]
    e_ref, w_ref, out_ref, obuf, lbuf, acc_ref = refs[2 * n:]
    tm, d = h_ref.shape

    def by_position(src_ref, dil, lanes, buf):
        if dil == 1:
            return src_ref[0, :, lanes].astype(_F32)
        for r in range(dil):
            buf[pl.ds(r, tm // dil, stride=dil), :] = src_ref[r, :, lanes].astype(_F32)
        return buf[...]

    lses = [by_position(l_refs[gi], dil, slice(0, LANES), lbuf.at[gi]) for gi, dil in enumerate(dilations)]
    m = functools.reduce(jnp.maximum, lses)
    es = [jnp.exp(l - m) for l in lses]
    den = functools.reduce(lambda a, b: a + b, es)
    wides = [_dot(_packed_pieces(e / den, d // HEAD_DIM), e_ref[...]) for e in es]
    for c in range(0, d, LANES):
        lanes = slice(c, c + LANES)
        acc = jnp.zeros((tm, LANES), _F32)
        for gi, dil in enumerate(dilations):
            acc = acc + wides[gi][:, lanes] * by_position(o_refs[gi], dil, lanes, obuf.at[(c // LANES) % 2, gi])
        acc_ref[:, lanes] = acc.astype(_BF16)
    out_ref[...] = h_ref[...] + _dot(acc_ref[...], w_ref[...])


def _dil_merge(h, outs, lses, w_out, batch, seq, dilations):
    t, d = h.shape
    tm = min(ROW_TILE, seq)
    nblk = seq // tm
    n_heads = d // HEAD_DIM
    piece_row = jnp.arange(LANES)
    expand = ((piece_row % n_heads)[:, None] == (jnp.arange(d) // HEAD_DIM)[None, :]) & (piece_row < 3 * n_heads)[:, None]
    res_spec = lambda dil, n: pl.BlockSpec((None, dil, tm // dil, n), lambda i: (i // nblk, 0, i % nblk, 0))
    n = len(dilations)
    return pl.pallas_call(
        functools.partial(_dil_merge_kernel, dilations=dilations),
        out_shape=jax.ShapeDtypeStruct((t, d), _F32),
        grid=(t // tm,),
        in_specs=([_row_spec(tm, d)] + [res_spec(dil, d) for dil in dilations] + [res_spec(dil, LANES) for dil in dilations]
                  + [_const_spec((LANES, d)), _const_spec(w_out.shape)]),
        out_specs=_row_spec(tm, d),
        scratch_shapes=[pltpu.VMEM((2, n, tm, LANES), _F32), pltpu.VMEM((n, tm, LANES), _F32),
                        pltpu.VMEM((tm, d), _BF16)],
        compiler_params=_params(1),
        name="dil_merge",
    )(h, *outs, *lses, expand.astype(_BF16), w_out)


def _dil_mixer(h, g, w_in, w_out, rope, batch, seq):
    t, d = h.shape
    dilations = tuple(dil for _, dil in DIL_CONFIGS)
    projs = _dil_proj(h, g, w_in.astype(_BF16), rope, batch, seq, dilations)
    outs, lses = [], []
    for (window, dil), proj in zip(DIL_CONFIGS, projs):
        assert seq % (dil * DIL_BAND) == 0
        proj = proj.reshape(t, 3 * d)
        o, lse = _dil_attn(proj, 0, proj, 1, proj, 2, batch * dil, seq // dil, d, window // dil)
        outs.append(o.reshape(batch, dil, seq // dil, d))
        lses.append(lse.reshape(batch, dil, seq // dil, LANES))
    return _dil_merge(h, outs, lses, w_out.astype(_BF16), batch, seq, dilations)


def _gelu(x):
    return 0.5 * x * (1.0 + lax.erf(x * math.sqrt(0.5)))


def _sgu_kernel(h_ref, g_ref, win_ref, nv_ref, ws_ref, bs_ref, wout_ref, o_ref, u_ref, v_ref, y_ref, *, chunk):
    x = h_ref[...]
    tm = x.shape[0]
    half = wout_ref.shape[0]
    gdim = half // SGU_GROUPS
    xn = _rms_norm(x, g_ref[...], NORM_EPS).astype(_BF16)
    ssq = jnp.zeros((tm, 1), _F32)
    for c in range(0, half, chunk):
        u_ref[:, c:c + chunk] = _gelu(_dot(xn, win_ref[:, c:c + chunk]))
        vv = _gelu(_dot(xn, win_ref[:, half + c:half + c + chunk]))
        v_ref[:, c:c + chunk] = vv
        ssq = ssq + jnp.sum(vv * vv, axis=1, keepdims=True)
    rstd = lax.rsqrt(ssq / half + NORM_EPS)
    row = lax.broadcasted_iota(jnp.int32, (SGU_CHUNK, SGU_CHUNK), 0)
    col = lax.broadcasted_iota(jnp.int32, (SGU_CHUNK, SGU_CHUNK), 1)
    for gi in range(SGU_GROUPS):
        cols = slice(gi * gdim, (gi + 1) * gdim)
        w = jnp.where(col <= row, ws_ref[gi], jnp.zeros((), _BF16))
        bias = bs_ref[:, gi:gi + 1]
        for r in range(0, tm, SGU_CHUNK):
            rows = slice(r, r + SGU_CHUNK)
            vn = (v_ref[rows, cols] * rstd[rows] * nv_ref[:, cols]).astype(_BF16)
            y_ref[rows, cols] = (u_ref[rows, cols] * (_dot(w, vn) + bias)).astype(_BF16)
    o_ref[...] = x + _dot(y_ref[...], wout_ref[...])


def _sgu_mixer(h, g, w_in, norm_v, w_s, b_s, w_out):
    t, d = h.shape
    half = w_out.shape[0]
    tm = min(ROW_TILE, t)
    assert tm % SGU_CHUNK == 0 and (half // SGU_GROUPS) % LANES == 0
    return pl.pallas_call(
        functools.partial(_sgu_kernel, chunk=_col_chunk(half)),
        out_shape=jax.ShapeDtypeStruct((t, d), _F32),
        grid=(t // tm,),
        in_specs=[_row_spec(tm, d), _const_spec((1, d)), _const_spec(w_in.shape), _const_spec((1, half)),
                  _const_spec(w_s.shape), _const_spec((SGU_CHUNK, SGU_GROUPS)), _const_spec(w_out.shape)],
        out_specs=_row_spec(tm, d),
        scratch_shapes=[pltpu.VMEM((tm, half), _F32), pltpu.VMEM((tm, half), _F32), pltpu.VMEM((tm, half), _BF16)],
        compiler_params=_params(1),
        name="sgu",
    )(h, g.reshape(1, d), w_in.astype(_BF16), norm_v.reshape(1, half), w_s.astype(_BF16), jnp.transpose(b_s),
      w_out.astype(_BF16))


def kernel(x, p, norm_ffn1, w_ffn1_in, w_ffn1_out, norm_mix, norm_ffn2, w_ffn2_in, w_ffn2_out, norm_ple, w_ple_gate, b_ple_gate, w_ple_proj, fox_w_in, fox_b_f, fox_w_out, dil_w_in, dil_w_out, diff_w_in, diff_lambda, diff_subln, diff_w_out, sgu_w_in, sgu_norm_v, sgu_w_s, sgu_b_s, sgu_w_out, norm_final):
    batch, seq, d = x.shape
    depth = p.shape[0]
    t = batch * seq
    h = x.reshape(t, d)
    rope = _rope_tables(seq)
    n_mixers = 4
    ffn1_in, ffn1_out = w_ffn1_in.astype(_BF16), w_ffn1_out.astype(_BF16)
    ffn2_in, ffn2_out = w_ffn2_in.astype(_BF16), w_ffn2_out.astype(_BF16)
    ple_gate, ple_proj = w_ple_gate.astype(_BF16), w_ple_proj.astype(_BF16)
    for i in range(depth):
        kind, j = i % n_mixers, i // n_mixers
        h = _ffn(h, norm_ffn1[i], ffn1_in, ffn1_out, i)
        attn = None
        if kind == 0:
            attn = _fox_mixer(h, norm_mix[i], fox_w_in[j], fox_b_f[j], fox_w_out[j], batch, seq)
        elif kind == 1:
            h = _dil_mixer(h, norm_mix[i], dil_w_in[j], dil_w_out[j], rope, batch, seq)
        elif kind == 2:
            attn = _diff_mixer(h, norm_mix[i], diff_w_in[j], diff_lambda[j], diff_subln[j], diff_w_out[j], rope,
                               batch, seq, i)
        else:
            h = _sgu_mixer(h, norm_mix[i], sgu_w_in[j], sgu_norm_v[j], sgu_w_s[j], sgu_b_s[j], sgu_w_out[j])
        h = _post_mixer(h, attn, norm_ffn2[i], ffn2_in, ffn2_out, p.reshape(depth, t, p.shape[-1]), i, norm_ple[i],
                        ple_gate, b_ple_gate[i], ple_proj, norm_final, final_norm=(i == depth - 1))
    return h.reshape(batch, seq, d)
```

```python
import functools
import math

import jax
import jax.numpy as jnp
import numpy as np
from jax import lax
from jax.experimental import pallas as pl
from jax.experimental.pallas import tpu as pltpu

_F32 = jnp.float32
_BF16 = jnp.bfloat16

HEAD_DIM = 64
ROPE_THETA = 500000.0
ROT_DIM = HEAD_DIM // 4
NORM_EPS = 1e-6
DIL_CONFIGS = ((128, 1), (512, 4), (2048, 16))
DIL_BAND = 128
DIFF_SUBLN_EPS = 1e-5
SGU_GROUPS = 8
SGU_CHUNK = 128

LANES = 128
VMEM_LIMIT_BYTES = 56 * 1024 * 1024
ROW_TILE = 512
ATTN_BLOCK = 512
COL_CHUNK = 256
LOG2E = math.log2(math.e)
QK_SCALE = HEAD_DIM ** -0.5 * LOG2E


def _params(n_axes):
    return pltpu.CompilerParams(dimension_semantics=("arbitrary",) * n_axes,
                                vmem_limit_bytes=VMEM_LIMIT_BYTES)


def _col_chunk(n, target=COL_CHUNK):
    best = LANES
    for c in range(LANES, target + 1, LANES):
        if n % c == 0:
            best = c
    assert n % best == 0, (n, best)
    return best


def _row_spec(tm, n):
    return pl.BlockSpec((tm, n), lambda i: (i, 0))


def _const_spec(shape):
    return pl.BlockSpec(shape, lambda i: (0,) * len(shape))


def _rms_norm(x, g, eps):
    return x * lax.rsqrt(jnp.mean(x * x, axis=-1, keepdims=True) + eps) * g


def _dot(a, b):
    return jnp.dot(a, b, preferred_element_type=_F32)


def _dot_nt(a, b):
    return lax.dot_general(a, b, (((1,), (1,)), ((), ())), preferred_element_type=_F32)


def _split3(x):
    hi = x.astype(_BF16)
    r1 = x - hi.astype(_F32)
    mid = r1.astype(_BF16)
    lo = (r1 - mid.astype(_F32)).astype(_BF16)
    return hi, mid, lo


def _packed_pieces(x, n_heads):
    assert 3 * n_heads <= LANES
    lane = lax.broadcasted_iota(jnp.int32, x.shape, 1)
    packed = jnp.zeros_like(x)
    for n, piece in enumerate(_split3(x)):
        piece = jnp.where(lane < n_heads, piece.astype(_F32), 0.0)
        packed = packed + (pltpu.roll(piece, n * n_heads, axis=1) if n else piece)
    return packed.astype(_BF16)


def _ffn_kernel(h_ref, g_ref, win_ref, wout_ref, o_ref, hid_ref, *, d_ff, chunk):
    x = h_ref[...]
    xn = _rms_norm(x, g_ref[...], NORM_EPS).astype(_BF16)
    for c in range(0, d_ff, chunk):
        gate = _dot(xn, win_ref[:, c:c + chunk])
        up = _dot(xn, win_ref[:, d_ff + c:d_ff + c + chunk])
        hid_ref[:, c:c + chunk] = (gate * jax.nn.sigmoid(gate) * up).astype(_BF16)
    o_ref[...] = x + 0.5 * _dot(hid_ref[...], wout_ref[...])


def _layer_spec(stacked, layer):
    zeros = (0,) * (stacked.ndim - 1)
    return pl.BlockSpec((None,) + stacked.shape[1:], lambda i: (layer,) + zeros, pipeline_mode=pl.Buffered(1))


def _ffn(h, g, w_in, w_out, layer):
    t, d = h.shape
    d_ff = w_out.shape[1]
    tm = min(ROW_TILE, t)
    return pl.pallas_call(
        functools.partial(_ffn_kernel, d_ff=d_ff, chunk=_col_chunk(d_ff)),
        out_shape=jax.ShapeDtypeStruct((t, d), _F32),
        grid=(t // tm,),
        in_specs=[_row_spec(tm, d), _const_spec((1, d)), _layer_spec(w_in, layer), _layer_spec(w_out, layer)],
        out_specs=_row_spec(tm, d),
        scratch_shapes=[pltpu.VMEM((tm, d_ff), _BF16)],
        compiler_params=_params(1),
        name="ffn",
    )(h, g.reshape(1, d), w_in, w_out)


def _post_kernel(*refs, has_attn, final_norm, d_ff, chunk):
    h_ref = refs[0]
    o_ref, wo_ref = refs[1:3] if has_attn else (None, None)
    (gf2_ref, win_ref, wout_ref, p_ref, gp_ref, wg_ref, bg_ref, wp_ref, gfin_ref,
     out_ref, hid_ref) = refs[3 if has_attn else 1:]
    x = h_ref[...]
    if has_attn:
        x = x + _dot(o_ref[...], wo_ref[...])
    xn = _rms_norm(x, gf2_ref[...], NORM_EPS).astype(_BF16)
    for c in range(0, d_ff, chunk):
        gate = _dot(xn, win_ref[:, c:c + chunk])
        up = _dot(xn, win_ref[:, d_ff + c:d_ff + c + chunk])
        hid_ref[:, c:c + chunk] = (gate * jax.nn.sigmoid(gate) * up).astype(_BF16)
    x = x + 0.5 * _dot(hid_ref[...], wout_ref[...])
    xn = _rms_norm(x, gp_ref[...], NORM_EPS).astype(_BF16)
    gate = jax.nn.sigmoid(_dot(xn, wg_ref[...]) + bg_ref[...])
    x = x + gate * _dot(p_ref[...].astype(_BF16), wp_ref[...])
    if final_norm:
        x = _rms_norm(x, gfin_ref[...], NORM_EPS)
    out_ref[...] = x


def _post_mixer(h, attn, g_ffn, w_in, w_out, p, layer, g_ple, w_gate, b_gate, w_proj, g_final, final_norm):
    t, d = h.shape
    d_ff = w_out.shape[1]
    dp = p.shape[2]
    tm = min(ROW_TILE, t)
    args, specs = [h], [_row_spec(tm, d)]
    if attn is not None:
        args += list(attn)
        specs += [_row_spec(tm, attn[0].shape[1]), _const_spec(attn[1].shape)]
    args += [g_ffn.reshape(1, d), w_in, w_out, p, g_ple.reshape(1, d), w_gate, b_gate.reshape(1, d), w_proj,
             g_final.reshape(1, d)]
    specs += [_const_spec((1, d)), _layer_spec(w_in, layer), _layer_spec(w_out, layer),
              pl.BlockSpec((None, tm, dp), lambda i: (layer, i, 0)),
              _const_spec((1, d)), _layer_spec(w_gate, layer), _const_spec((1, d)), _layer_spec(w_proj, layer),
              _const_spec((1, d))]
    return pl.pallas_call(
        functools.partial(_post_kernel, has_attn=attn is not None, final_norm=final_norm, d_ff=d_ff,
                          chunk=_col_chunk(d_ff)),
        out_shape=jax.ShapeDtypeStruct((t, d), _F32),
        grid=(t // tm,),
        in_specs=specs,
        out_specs=_row_spec(tm, d),
        scratch_shapes=[pltpu.VMEM((tm, d_ff), _BF16)],
        compiler_params=_params(1),
        name="post_mixer",
    )(*args)


def _rope_tables(s):
    half = ROT_DIM // 2
    inv_freq = ROPE_THETA ** (-jnp.arange(0, ROT_DIM, 2, dtype=_F32) / ROT_DIM)
    ang = jnp.arange(s, dtype=jnp.int32).astype(_F32)[:, None] * inv_freq[None, :]
    cos, sin = jnp.cos(ang), jnp.sin(ang)
    m = jnp.arange(LANES) % HEAD_DIM
    idx = m % half
    first, second = m < half, (m >= half) & (m < ROT_DIM)
    a = jnp.where((first | second)[None, :], cos[:, idx], 1.0)
    b = jnp.where(first[None, :], -sin[:, idx], 0.0)
    c = jnp.where(second[None, :], sin[:, idx], 0.0)
    return a.astype(_F32), b.astype(_F32), c.astype(_F32)


def _proj_kernel(h_ref, g_ref, w_ref, ra_ref, rb_ref, rc_ref, o_ref, *, n_query, n_rope, chunk):
    xn = _rms_norm(h_ref[...], g_ref[...], NORM_EPS).astype(_BF16)
    half = ROT_DIM // 2
    for c in range(0, w_ref.shape[1], chunk):
        y = _dot(xn, w_ref[:, c:c + chunk])
        if c < n_rope:
            for s in range(0, chunk, LANES):
                ys = y[:, s:s + LANES]
                ys = (ys * ra_ref[...] + pltpu.roll(ys, LANES - half, axis=1) * rb_ref[...]
                      + pltpu.roll(ys, half, axis=1) * rc_ref[...])
                if c < n_query:
                    ys = ys * QK_SCALE
                o_ref[:, c + s:c + s + LANES] = ys.astype(o_ref.dtype)
        else:
            o_ref[:, c:c + chunk] = y.astype(o_ref.dtype)


def _proj(h, g, w, rope, n_query, n_rope, seq):
    t, d = h.shape
    n = w.shape[1]
    tm = min(ROW_TILE, seq)
    chunk = _col_chunk(n)
    assert n_rope % chunk == 0 and n_query % chunk == 0 and n_query <= n_rope
    nblk = seq // tm
    rope_spec = pl.BlockSpec((tm, LANES), lambda i: (i % nblk, 0))
    return pl.pallas_call(
        functools.partial(_proj_kernel, n_query=n_query, n_rope=n_rope, chunk=chunk),
        out_shape=jax.ShapeDtypeStruct((t, n), _BF16),
        grid=(t // tm,),
        in_specs=[_row_spec(tm, d), _const_spec((1, d)), _const_spec(w.shape), rope_spec, rope_spec, rope_spec],
        out_specs=_row_spec(tm, n),
        compiler_params=_params(1),
        name="proj",
    )(h, g.reshape(1, d), w, *rope)


def _fox_proj_kernel(h_ref, g_ref, w_ref, wf_ref, bf_ref, qkv_ref, lf_ref, *, n_query, chunk):
    xn = _rms_norm(h_ref[...], g_ref[...], NORM_EPS).astype(_BF16)
    for c in range(0, w_ref.shape[1], chunk):
        y = _dot(xn, w_ref[:, c:c + chunk])
        if c < n_query:
            y = y * QK_SCALE
        qkv_ref[:, c:c + chunk] = y.astype(qkv_ref.dtype)
    z = _dot(xn, wf_ref[...]) + bf_ref[...]
    lf_ref[...] = jnp.minimum(z, 0.0) - jnp.log1p(jnp.exp(-jnp.abs(z)))


def _fox_proj(h, g, w_qkv, w_f, b_f):
    t, d = h.shape
    n = w_qkv.shape[1]
    tm = min(ROW_TILE, t)
    return pl.pallas_call(
        functools.partial(_fox_proj_kernel, n_query=d, chunk=_col_chunk(n)),
        out_shape=(jax.ShapeDtypeStruct((t, n), _BF16), jax.ShapeDtypeStruct((t, LANES), _F32)),
        grid=(t // tm,),
        in_specs=[_row_spec(tm, d), _const_spec((1, d)), _const_spec(w_qkv.shape), _const_spec(w_f.shape),
                  _const_spec((1, LANES))],
        out_specs=(_row_spec(tm, n), _row_spec(tm, LANES)),
        compiler_params=_params(1),
        name="fox_proj",
    )(h, g.reshape(1, d), w_qkv, w_f, b_f)


def _bias_placement(d):
    pairs = d // LANES
    n_heads = 2 * pairs
    w = np.zeros((3, LANES, d), np.float32)
    ones = np.zeros((3, 1, d), np.float32)
    for hp in range(pairs):
        base = hp * LANES
        for p in range(3):
            w[0, p * n_heads + 2 * hp, base + p] = 1.0
            w[1, p * n_heads + 2 * hp + 1, base + 6 + p] = 1.0
            w[2, p * n_heads + 2 * hp, base + 3 + p] = -1.0
            w[2, p * n_heads + 2 * hp + 1, base + 9 + p] = -1.0
            ones[0, 0, base + 3 + p] = 1.0
            ones[1, 0, base + 9 + p] = 1.0
            ones[2, 0, base + p] = 1.0
            ones[2, 0, base + 6 + p] = 1.0
    return jnp.asarray(w, _BF16), jnp.asarray(ones, _F32)


def _cumsum_kernel(x_ref, w_ref, ones_ref, qea_ref, qeb_ref, ke_ref, carry_ref, *, n_heads):
    @pl.when(pl.program_id(1) == 0)
    def _():
        carry_ref[...] = jnp.zeros_like(carry_ref)

    x = x_ref[...]
    ts = x.shape[0]
    row = lax.broadcasted_iota(jnp.int32, (ts, ts), 0)
    col = lax.broadcasted_iota(jnp.int32, (ts, ts), 1)
    tri = jnp.where(col <= row, 1.0, 0.0).astype(_BF16)
    hi, mid, lo = _split3(x)
    c = _dot(tri, hi) + _dot(tri, mid) + _dot(tri, lo) + carry_ref[0:1, :]
    carry_ref[0:1, :] = c[ts - 1:ts, :]
    pieces = _packed_pieces(c * LOG2E, n_heads)
    for n, out_ref in enumerate((qea_ref, qeb_ref, ke_ref)):
        out_ref[...] = (_dot(pieces, w_ref[n]) + ones_ref[n]).astype(out_ref.dtype)


def _cumsum_bias(x, batch, seq, d, ts):
    ns = seq // ts
    w, ones = _bias_placement(d)
    row_spec = pl.BlockSpec((ts, d), lambda b, s: (b * ns + s, 0))
    out = jax.ShapeDtypeStruct((batch * seq, d), _BF16)
    return pl.pallas_call(
        functools.partial(_cumsum_kernel, n_heads=d // HEAD_DIM),
        out_shape=(out, out, out),
        grid=(batch, ns),
        in_specs=[pl.BlockSpec((ts, LANES), lambda b, s: (b * ns + s, 0)),
                  pl.BlockSpec(w.shape, lambda b, s: (0, 0, 0)), pl.BlockSpec(ones.shape, lambda b, s: (0, 0, 0))],
        out_specs=(row_spec, row_spec, row_spec),
        scratch_shapes=[pltpu.VMEM((8, LANES), _F32)],
        compiler_params=_params(2),
        name="cumsum",
    )(x, w, ones)


def _flash_items(nq, blk):
    items = []
    for i in range(nq):
        for j in range(i // 2):
            items.append((i, 2 * j * blk, 2 * blk, False))
        if i % 2 == 1:
            items.append((i, (i - 1) * blk, 2 * blk, True))
        else:
            items.append((i, i * blk, blk, True))
    return items


def _causal_flash(query_maps, k_refs, value_maps, finish, nq, blk, scratch):
    s_bufs, m_ref, acc_ref = scratch[:2], scratch[2], scratch[3]
    items = _flash_items(nq, blk)

    def produce(item, s_ref):
        i, start, width, _ = item
        parts = [r[start:start + width, :] for r in k_refs]
        kb = parts[0] if len(parts) == 1 else jnp.concatenate(parts, axis=1)
        for e, qm in enumerate(query_maps(i)):
            s_ref[e, :, :width] = _dot_nt(qm, kb)

    def consume(item, s_ref, first):
        i, start, width, masked = item
        vbs = value_maps(start, width)
        for e in range(2):
            s = s_ref[e, :, :width]
            if masked:
                row = lax.broadcasted_iota(jnp.int32, (blk, width), 0)
                col = lax.broadcasted_iota(jnp.int32, (blk, width), 1)
                s = jnp.where(col - row <= i * blk - start, s, -jnp.inf)
            m_new = jnp.max(s, axis=1, keepdims=True)
            if not first:
                m_old = m_ref[e]
                m_new = jnp.maximum(m_old, m_new)
            pv = _dot(jnp.exp2((s - m_new).astype(_BF16)), vbs[e])
            acc_ref[e] = pv if first else acc_ref[e] * jnp.exp2(m_old - m_new) + pv
            m_ref[e] = m_new

    produce(items[0], s_bufs[0])
    for t, item in enumerate(items):
        if t + 1 < len(items):
            produce(items[t + 1], s_bufs[(t + 1) % 2])
        consume(item, s_bufs[t % 2], first=(t == 0 or items[t - 1][0] != item[0]))
        if t + 1 == len(items) or items[t + 1][0] != item[0]:
            finish(item[0], acc_ref[0], acc_ref[1])


def _flash_scratch(blk, n_out):
    return [pltpu.VMEM((2, blk, 2 * blk), _F32), pltpu.VMEM((2, blk, 2 * blk), _F32),
            pltpu.VMEM((2, blk, 1), _F32), pltpu.VMEM((2, blk, n_out), _F32)]


def _split_maps(q, lo):
    zero = jnp.zeros_like(q)
    return jnp.where(lo, q, zero), jnp.where(lo, zero, q)


def _fox_attn_kernel(q_ref, qea_ref, qeb_ref, k_ref, ke_ref, v_ref, o_ref, *scratch, blk):
    lo = lax.broadcasted_iota(jnp.int32, (blk, LANES), 1) < HEAD_DIM

    def query_maps(i):
        rows = slice(i * blk, (i + 1) * blk)
        qa, qb = _split_maps(q_ref[rows, :], lo)
        return jnp.concatenate([qa, qea_ref[rows, :]], axis=1), jnp.concatenate([qb, qeb_ref[rows, :]], axis=1)

    def value_maps(start, width):
        vb = v_ref[start:start + width, :]
        keep = lax.broadcasted_iota(jnp.int32, (width, LANES), 1) < HEAD_DIM
        one = jnp.ones_like(vb)
        return jnp.where(keep, vb, one), jnp.where(keep, one, vb)

    def finish(i, acc_a, acc_b):
        out_a = acc_a / pltpu.roll(acc_a, HEAD_DIM, axis=1)
        out_b = acc_b / pltpu.roll(acc_b, HEAD_DIM, axis=1)
        o_ref[i * blk:(i + 1) * blk, :] = jnp.where(lo, out_a, out_b).astype(o_ref.dtype)

    _causal_flash(query_maps, (k_ref, ke_ref), value_maps, finish, q_ref.shape[0] // blk, blk, scratch)


def _fox_attn(qkv, qea, qeb, ke, batch, seq, d):
    blk = min(ATTN_BLOCK, seq)
    pairs = d // LANES
    spec = lambda col0: pl.BlockSpec((seq, LANES), lambda b, h: (b, col0 + h))
    return pl.pallas_call(
        functools.partial(_fox_attn_kernel, blk=blk),
        out_shape=jax.ShapeDtypeStruct((batch * seq, d), _BF16),
        grid=(batch, pairs),
        in_specs=[spec(0), spec(0), spec(0), spec(pairs), spec(0), spec(2 * pairs)],
        out_specs=spec(0),
        scratch_shapes=_flash_scratch(blk, LANES),
        compiler_params=_params(2),
        name="fox_attn",
    )(qkv, qea, qeb, qkv, ke, qkv)


def _fox_mixer(h, g, w_in, b_f, w_out, batch, seq):
    t, d = h.shape
    n_heads = d // HEAD_DIM
    w_qkv = w_in[:, :3 * d].astype(_BF16)
    w_f = jnp.pad(w_in[:, 3 * d:], ((0, 0), (0, LANES - n_heads))).astype(_BF16)
    b_pad = jnp.pad(b_f, (0, LANES - n_heads)).reshape(1, LANES)
    qkv, log_f = _fox_proj(h, g, w_qkv, w_f, b_pad)
    qea, qeb, ke = _cumsum_bias(log_f, batch, seq, d, min(ATTN_BLOCK, seq))
    return _fox_attn(qkv, qea, qeb, ke, batch, seq, d), w_out.astype(_BF16)


def _diff_attn_kernel(q_ref, k_ref, v_ref, lam_ref, g_ref, o_ref, *scratch, blk, lam_init):
    lo = lax.broadcasted_iota(jnp.int32, (blk, LANES), 1) < HEAD_DIM
    lp = lam_ref[...]
    lam = (jnp.exp(jnp.sum(lp[0:1] * lp[1:2], axis=1, keepdims=True))
           - jnp.exp(jnp.sum(lp[2:3] * lp[3:4], axis=1, keepdims=True)) + lam_init)

    def query_maps(i):
        return _split_maps(q_ref[i * blk:(i + 1) * blk, :], lo)

    def value_maps(start, width):
        vb = v_ref[start:start + width, :]
        vb = jnp.concatenate([vb, jnp.ones_like(vb)], axis=1)
        return vb, vb

    def finish(i, acc_1, acc_2):
        o = acc_1[:, :LANES] / acc_1[:, LANES:] - lam * (acc_2[:, :LANES] / acc_2[:, LANES:])
        o = _rms_norm(o, g_ref[...], DIFF_SUBLN_EPS) * (1.0 - lam_init)
        o_ref[i * blk:(i + 1) * blk, :] = o.astype(o_ref.dtype)

    _causal_flash(query_maps, (k_ref,), value_maps, finish, q_ref.shape[0] // blk, blk, scratch)


def _diff_attn(proj, lam_params, subln_g, batch, seq, d, lam_init):
    blk = min(ATTN_BLOCK, seq)
    heads = d // LANES
    spec = lambda col0: pl.BlockSpec((seq, LANES), lambda b, h: (b, col0 + h))
    return pl.pallas_call(
        functools.partial(_diff_attn_kernel, blk=blk, lam_init=lam_init),
        out_shape=jax.ShapeDtypeStruct((batch * seq, d), _BF16),
        grid=(batch, heads),
        in_specs=[spec(0), spec(heads), spec(2 * heads),
                  pl.BlockSpec(lam_params.shape, lambda b, h: (0, 0)), pl.BlockSpec((1, LANES), lambda b, h: (0, 0))],
        out_specs=spec(0),
        scratch_shapes=_flash_scratch(blk, 2 * LANES),
        compiler_params=_params(2),
        name="diff_attn",
    )(proj, proj, proj, lam_params, subln_g.reshape(1, LANES))


def _diff_mixer(h, g, w_in, lam_params, subln_g, w_out, rope, batch, seq, layer_idx):
    t, d = h.shape
    lam_init = 0.8 - 0.6 * math.exp(-0.3 * layer_idx)
    proj = _proj(h, g, w_in.astype(_BF16), rope, d, 2 * d, seq)
    return _diff_attn(proj, lam_params, subln_g, batch, seq, d, lam_init), w_out.astype(_BF16)


def _dil_attn_kernel(q_ref, kp_ref, kc_ref, vp_ref, vc_ref, o_ref, lse_ref, kcat_ref, vcat_ref, s_ref, p_ref, *,
                     n_steps, n_pairs):
    band = DIL_BAND
    n_heads = 2 * n_pairs
    kcat_ref[0:band, :] = kp_ref[...]
    kcat_ref[band:, :] = kc_ref[...]
    vcat_ref[0:band, :] = vp_ref[...]
    vcat_ref[band:, :] = vc_ref[...]
    row = lax.broadcasted_iota(jnp.int32, (band, 2 * band), 0)
    col = lax.broadcasted_iota(jnp.int32, (band, 2 * band), 1)
    in_band = jnp.logical_and(col <= row + band, col >= row + band - n_steps)
    first_valid = jnp.logical_and(in_band, jnp.logical_or(col >= band, pl.program_id(1) > 0))
    lane = lax.broadcasted_iota(jnp.int32, (band, LANES), 1)
    lo = lane < HEAD_DIM
    lo_v = lax.broadcasted_iota(jnp.int32, (2 * band, LANES), 1) < HEAD_DIM
    for r in range(q_ref.shape[0] // band):
        rows = slice(r * band, (r + 1) * band)
        win = slice(r * band, (r + 2) * band)
        valid = first_valid if r == 0 else in_band
        for hp in range(n_pairs):
            sl = slice(hp * LANES, (hp + 1) * LANES)
            q = q_ref[rows, sl]
            zero = jnp.zeros_like(q)
            kw = kcat_ref[win, sl]
            for e, qm in enumerate((jnp.where(lo, q, zero), jnp.where(lo, zero, q))):
                s_ref[(2 * hp + e) * band:(2 * hp + e + 1) * band, :] = jnp.where(valid, _dot_nt(qm, kw), -jnp.inf)
        s = s_ref[...]
        m = jnp.max(s, axis=1, keepdims=True)
        p_ref[...] = jnp.exp2((s - m).astype(_BF16))
        m_tile = jnp.zeros((band, LANES), _F32)
        l_tile = jnp.ones((band, LANES), _F32)
        for hp in range(n_pairs):
            sl = slice(hp * LANES, (hp + 1) * LANES)
            vw = vcat_ref[win, sl]
            one = jnp.ones_like(vw)
            ha, hb = slice((2 * hp) * band, (2 * hp + 1) * band), slice((2 * hp + 1) * band, (2 * hp + 2) * band)
            pv_a = _dot(p_ref[ha, :], jnp.where(lo_v, vw, one))
            pv_b = _dot(p_ref[hb, :], jnp.where(lo_v, one, vw))
            l_a, l_b = pltpu.roll(pv_a, HEAD_DIM, axis=1), pltpu.roll(pv_b, HEAD_DIM, axis=1)
            o_ref[rows, sl] = (jnp.where(lo, pv_a, pv_b) / jnp.where(lo, l_a, l_b)).astype(o_ref.dtype)
            m_tile = jnp.where(lane == 2 * hp, m[ha], jnp.where(lane == 2 * hp + 1, m[hb], m_tile))
            l_tile = jnp.where(lane == 2 * hp, l_a, jnp.where(lane == 2 * hp + 1, pv_b, l_tile))
        lse_ref[rows, :] = (m_tile + jnp.log2(l_tile)) * (1.0 / LOG2E)


def _dil_attn(q, q_col, k, k_col, v, v_col, n_seq, n_sub, d, n_steps):
    assert n_steps <= DIL_BAND
    rb = min(ROW_TILE, n_sub)
    nb = n_sub // rb
    bands = rb // DIL_BAND
    n_heads = d // HEAD_DIM
    cur = lambda col: (lambda x, n: (x * nb + n, col))
    prev = lambda col: (lambda x, n: ((x * nb + n) * bands - jnp.where(n > 0, 1, 0), col))
    return pl.pallas_call(
        functools.partial(_dil_attn_kernel, n_steps=n_steps, n_pairs=d // LANES),
        out_shape=(jax.ShapeDtypeStruct((n_seq * n_sub, d), _BF16),
                   jax.ShapeDtypeStruct((n_seq * n_sub, LANES), _F32)),
        grid=(n_seq, nb),
        in_specs=[pl.BlockSpec((rb, d), cur(q_col)),
                  pl.BlockSpec((DIL_BAND, d), prev(k_col)), pl.BlockSpec((rb, d), cur(k_col)),
                  pl.BlockSpec((DIL_BAND, d), prev(v_col)), pl.BlockSpec((rb, d), cur(v_col))],
        out_specs=(pl.BlockSpec((rb, d), cur(0)), pl.BlockSpec((rb, LANES), cur(0))),
        scratch_shapes=[pltpu.VMEM((rb + DIL_BAND, d), _BF16), pltpu.VMEM((rb + DIL_BAND, d), _BF16),
                        pltpu.VMEM((n_heads * DIL_BAND, 2 * DIL_BAND), _F32),
                        pltpu.VMEM((n_heads * DIL_BAND, 2 * DIL_BAND), _BF16)],
        compiler_params=_params(2),
        name="dil_attn",
    )(q, k, k, v, v)


def _dil_proj_kernel(h_ref, g_ref, w_ref, ra_ref, rb_ref, rc_ref, *refs, d_model, dilations):
    out_refs, ybuf, qbuf = refs[:len(dilations)], refs[len(dilations)], refs[len(dilations) + 1]
    n_groups = len(dilations)
    tm = h_ref.shape[0]
    xn = _rms_norm(h_ref[...], g_ref[...], NORM_EPS).astype(_BF16)
    half = ROT_DIM // 2
    n_slab = 0
    step = qbuf.shape[1]
    for c in range(0, w_ref.shape[1], LANES * 2):
        y = _dot(xn, w_ref[:, c:c + 2 * LANES])
        for s in range(0, 2 * LANES, LANES):
            src, col = divmod(c + s, d_model)
            ys = y[:, s:s + LANES]
            if src <= n_groups:
                ys = (ys * ra_ref[...] + pltpu.roll(ys, LANES - half, axis=1) * rb_ref[...]
                      + pltpu.roll(ys, half, axis=1) * rc_ref[...])
            if src < n_groups:
                ys = ys * QK_SCALE
            dests = [(src, 0)] if src < n_groups else [(g, src - n_groups + 1) for g in range(n_groups)]
            slab, quarter = ybuf.at[n_slab % ybuf.shape[0]], qbuf.at[n_slab % qbuf.shape[0]]
            n_slab += 1
            strides = {dilations[g] for g, _ in dests} - {1}
            if strides:
                slab[...] = ys
            if any(dil % step == 0 for dil in strides):
                firsts = [slab[pl.ds(r, tm // step, stride=step), :] for r in range(step)]
                if any(dil > step for dil in strides):
                    for r in range(step):
                        quarter[r] = firsts[r]
            for g, blk in dests:
                dil = dilations[g]
                lanes = slice(blk * d_model + col, blk * d_model + col + LANES)
                for r in range(dil):
                    if dil == 1:
                        rows = ys
                    elif dil == step:
                        rows = firsts[r]
                    elif dil % step == 0:
                        rows = quarter[r % step, pl.ds(r // step, tm // dil, stride=dil // step), :]
                    else:
                        rows = slab[pl.ds(r, tm // dil, stride=dil), :]
                    out_refs[g][r, :, lanes] = rows.astype(_BF16)


def _dil_proj(h, g, w, rope, batch, seq, dilations):
    t, d = h.shape
    tm = min(ROW_TILE, seq)
    nblk = seq // tm
    rope_spec = pl.BlockSpec((tm, LANES), lambda i: (i % nblk, 0))
    out_shapes = tuple(jax.ShapeDtypeStruct((batch, dil, seq // dil, 3 * d), _BF16) for dil in dilations)
    out_specs = tuple(pl.BlockSpec((None, dil, tm // dil, 3 * d), lambda i: (i // nblk, 0, i % nblk, 0))
                      for dil in dilations)
    return pl.pallas_call(
        functools.partial(_dil_proj_kernel, d_model=d, dilations=dilations),
        out_shape=out_shapes,
        grid=(t // tm,),
        in_specs=[_row_spec(tm, d), _const_spec((1, d)), _const_spec(w.shape), rope_spec, rope_spec, rope_spec],
        out_specs=out_specs,
        scratch_shapes=[pltpu.VMEM((4, tm, LANES), _F32), pltpu.VMEM((4, 4, tm // 4, LANES), _F32)],
        compiler_params=_params(1),
        name="dil_proj",
    )(h, g.reshape(1, d), w, *rope)


def _dil_merge_kernel(h_ref, *refs, dilations):
    n = len(dilations)
    o_refs, l_refs = refs[:n], refs[n:2 * n]
    e_ref, w_ref, out_ref, obuf, lbuf, acc_ref = refs[2 * n:]
    tm, d = h_ref.shape

    def by_position(src_ref, dil, lanes, buf):
        if dil == 1:
            return src_ref[0, :, lanes].astype(_F32)
        for r in range(dil):
            buf[pl.ds(r, tm // dil, stride=dil), :] = src_ref[r, :, lanes].astype(_F32)
        return buf[...]

    lses = [by_position(l_refs[gi], dil, slice(0, LANES), lbuf.at[gi]) for gi, dil in enumerate(dilations)]
    m = functools.reduce(jnp.maximum, lses)
    es = [jnp.exp(l - m) for l in lses]
    den = functools.reduce(lambda a, b: a + b, es)
    wides = [_dot(_packed_pieces(e / den, d // HEAD_DIM), e_ref[...]) for e in es]
    for c in range(0, d, LANES):
        lanes = slice(c, c + LANES)
        acc = jnp.zeros((tm, LANES), _F32)
        for gi, dil in enumerate(dilations):
            acc = acc + wides[gi][:, lanes] * by_position(o_refs[gi], dil, lanes, obuf.at[(c // LANES) % 2, gi])
        acc_ref[:, lanes] = acc.astype(_BF16)
    out_ref[...] = h_ref[...] + _dot(acc_ref[...], w_ref[...])


def _dil_merge(h, outs, lses, w_out, batch, seq, dilations):
    t, d = h.shape
    tm = min(ROW_TILE, seq)
    nblk = seq // tm
    n_heads = d // HEAD_DIM
    piece_row = jnp.arange(LANES)
    expand = ((piece_row % n_heads)[:, None] == (jnp.arange(d) // HEAD_DIM)[None, :]) & (piece_row < 3 * n_heads)[:, None]
    res_spec = lambda dil, n: pl.BlockSpec((None, dil, tm // dil, n), lambda i: (i // nblk, 0, i % nblk, 0))
    n = len(dilations)
    return pl.pallas_call(
        functools.partial(_dil_merge_kernel, dilations=dilations),
        out_shape=jax.ShapeDtypeStruct((t, d), _F32),
        grid=(t // tm,),
        in_specs=([_row_spec(tm, d)] + [res_spec(dil, d) for dil in dilations] + [res_spec(dil, LANES) for dil in dilations]
                  + [_const_spec((LANES, d)), _const_spec(w_out.shape)]),
        out_specs=_row_spec(tm, d),
        scratch_shapes=[pltpu.VMEM((2, n, tm, LANES), _F32), pltpu.VMEM((n, tm, LANES), _F32),
                        pltpu.VMEM((tm, d), _BF16)],
        compiler_params=_params(1),
        name="dil_merge",
    )(h, *outs, *lses, expand.astype(_BF16), w_out)


def _dil_mixer(h, g, w_in, w_out, rope, batch, seq):
    t, d = h.shape
    dilations = tuple(dil for _, dil in DIL_CONFIGS)
    projs = _dil_proj(h, g, w_in.astype(_BF16), rope, batch, seq, dilations)
    outs, lses = [], []
    for (window, dil), proj in zip(DIL_CONFIGS, projs):
        assert seq % (dil * DIL_BAND) == 0
        proj = proj.reshape(t, 3 * d)
        o, lse = _dil_attn(proj, 0, proj, 1, proj, 2, batch * dil, seq // dil, d, window // dil)
        outs.append(o.reshape(batch, dil, seq // dil, d))
        lses.append(lse.reshape(batch, dil, seq // dil, LANES))
    return _dil_merge(h, outs, lses, w_out.astype(_BF16), batch, seq, dilations)


def _gelu(x):
    return 0.5 * x * (1.0 + lax.erf(x * math.sqrt(0.5)))


def _sgu_kernel(h_ref, g_ref, win_ref, nv_ref, ws_ref, bs_ref, wout_ref, o_ref, u_ref, v_ref, y_ref, *, chunk):
    x = h_ref[...]
    tm = x.shape[0]
    half = wout_ref.shape[0]
    gdim = half // SGU_GROUPS
    xn = _rms_norm(x, g_ref[...], NORM_EPS).astype(_BF16)
    ssq = jnp.zeros((tm, 1), _F32)
    for c in range(0, half, chunk):
        u_ref[:, c:c + chunk] = _gelu(_dot(xn, win_ref[:, c:c + chunk]))
        vv = _gelu(_dot(xn, win_ref[:, half + c:half + c + chunk]))
        v_ref[:, c:c + chunk] = vv
        ssq = ssq + jnp.sum(vv * vv, axis=1, keepdims=True)
    rstd = lax.rsqrt(ssq / half + NORM_EPS)
    row = lax.broadcasted_iota(jnp.int32, (SGU_CHUNK, SGU_CHUNK), 0)
    col = lax.broadcasted_iota(jnp.int32, (SGU_CHUNK, SGU_CHUNK), 1)
    for gi in range(SGU_GROUPS):
        cols = slice(gi * gdim, (gi + 1) * gdim)
        w = jnp.where(col <= row, ws_ref[gi], jnp.zeros((), _BF16))
        bias = bs_ref[:, gi:gi + 1]
        for r in range(0, tm, SGU_CHUNK):
            rows = slice(r, r + SGU_CHUNK)
            vn = (v_ref[rows, cols] * rstd[rows] * nv_ref[:, cols]).astype(_BF16)
            y_ref[rows, cols] = (u_ref[rows, cols] * (_dot(w, vn) + bias)).astype(_BF16)
    o_ref[...] = x + _dot(y_ref[...], wout_ref[...])


def _sgu_mixer(h, g, w_in, norm_v, w_s, b_s, w_out):
    t, d = h.shape
    half = w_out.shape[0]
    tm = min(ROW_TILE, t)
    assert tm % SGU_CHUNK == 0 and (half // SGU_GROUPS) % LANES == 0
    return pl.pallas_call(
        functools.partial(_sgu_kernel, chunk=_col_chunk(half)),
        out_shape=jax.ShapeDtypeStruct((t, d), _F32),
        grid=(t // tm,),
        in_specs=[_row_spec(tm, d), _const_spec((1, d)), _const_spec(w_in.shape), _const_spec((1, half)),
                  _const_spec(w_s.shape), _const_spec((SGU_CHUNK, SGU_GROUPS)), _const_spec(w_out.shape)],
        out_specs=_row_spec(tm, d),
        scratch_shapes=[pltpu.VMEM((tm, half), _F32), pltpu.VMEM((tm, half), _F32), pltpu.VMEM((tm, half), _BF16)],
        compiler_params=_params(1),
        name="sgu",
    )(h, g.reshape(1, d), w_in.astype(_BF16), norm_v.reshape(1, half), w_s.astype(_BF16), jnp.transpose(b_s),
      w_out.astype(_BF16))


def kernel(x, p, norm_ffn1, w_ffn1_in, w_ffn1_out, norm_mix, norm_ffn2, w_ffn2_in, w_ffn2_out, norm_ple, w_ple_gate, b_ple_gate, w_ple_proj, fox_w_in, fox_b_f, fox_w_out, dil_w_in, dil_w_out, diff_w_in, diff_lambda, diff_subln, diff_w_out, sgu_w_in, sgu_norm_v, sgu_w_s, sgu_b_s, sgu_w_out, norm_final):
    batch, seq, d = x.shape
    depth = p.shape[0]
    t = batch * seq
    h = x.reshape(t, d)
    rope = _rope_tables(seq)
    n_mixers = 4
    ffn1_in, ffn1_out = w_ffn1_in.astype(_BF16), w_ffn1_out.astype(_BF16)
    ffn2_in, ffn2_out = w_ffn2_in.astype(_BF16), w_ffn2_out.astype(_BF16)
    ple_gate, ple_proj = w_ple_gate.astype(_BF16), w_ple_proj.astype(_BF16)
    for i in range(depth):
        kind, j = i % n_mixers, i // n_mixers
        h = _ffn(h, norm_ffn1[i], ffn1_in, ffn1_out, i)
        attn = None
        if kind == 0:
            attn = _fox_mixer(h, norm_mix[i], fox_w_in[j], fox_b_f[j], fox_w_out[j], batch, seq)
        elif kind == 1:
            h = _dil_mixer(h, norm_mix[i], dil_w_in[j], dil_w_out[j], rope, batch, seq)
        elif kind == 2:
            attn = _diff_mixer(h, norm_mix[i], diff_w_in[j], diff_lambda[j], diff_subln[j], diff_w_out[j], rope,
                               batch, seq, i)
        else:
            h = _sgu_mixer(h, norm_mix[i], sgu_w_in[j], sgu_norm_v[j], sgu_w_s[j], sgu_b_s[j], sgu_w_out[j])
        h = _post_mixer(h, attn, norm_ffn2[i], ffn2_in, ffn2_out, p.reshape(depth, t, p.shape[-1]), i, norm_ple[i],
                        ple_gate, b_ple_gate[i], ple_proj, norm_final, final_norm=(i == depth - 1))
    return h.reshape(batch, seq, d)
```

```python
import functools
import math

import jax
import jax.numpy as jnp
import numpy as np
from jax import lax
from jax.experimental import pallas as pl
from jax.experimental.pallas import tpu as pltpu

_F32 = jnp.float32
_BF16 = jnp.bfloat16

HEAD_DIM = 64
ROPE_THETA = 500000.0
ROT_DIM = HEAD_DIM // 4
NORM_EPS = 1e-6
DIL_CONFIGS = ((128, 1), (512, 4), (2048, 16))
DIL_BAND = 128
DIFF_SUBLN_EPS = 1e-5
SGU_GROUPS = 8
SGU_CHUNK = 128

LANES = 128
VMEM_LIMIT_BYTES = 56 * 1024 * 1024
ROW_TILE = 512
ATTN_BLOCK = 512
COL_CHUNK = 256
LOG2E = math.log2(math.e)
QK_SCALE = HEAD_DIM ** -0.5 * LOG2E


def _params(n_axes):
    return pltpu.CompilerParams(dimension_semantics=("arbitrary",) * n_axes,
                                vmem_limit_bytes=VMEM_LIMIT_BYTES)


def _col_chunk(n, target=COL_CHUNK):
    best = LANES
    for c in range(LANES, target + 1, LANES):
        if n % c == 0:
            best = c
    assert n % best == 0, (n, best)
    return best


def _row_spec(tm, n):
    return pl.BlockSpec((tm, n), lambda i: (i, 0))


def _const_spec(shape):
    return pl.BlockSpec(shape, lambda i: (0,) * len(shape))


def _rms_norm(x, g, eps):
    return x * lax.rsqrt(jnp.mean(x * x, axis=-1, keepdims=True) + eps) * g


def _dot(a, b):
    return jnp.dot(a, b, preferred_element_type=_F32)


def _dot_nt(a, b):
    return lax.dot_general(a, b, (((1,), (1,)), ((), ())), preferred_element_type=_F32)


def _split3(x):
    hi = x.astype(_BF16)
    r1 = x - hi.astype(_F32)
    mid = r1.astype(_BF16)
    lo = (r1 - mid.astype(_F32)).astype(_BF16)
    return hi, mid, lo


def _packed_pieces(x, n_heads):
    assert 3 * n_heads <= LANES
    lane = lax.broadcasted_iota(jnp.int32, x.shape, 1)
    packed = jnp.zeros_like(x)
    for n, piece in enumerate(_split3(x)):
        piece = jnp.where(lane < n_heads, piece.astype(_F32), 0.0)
        packed = packed + (pltpu.roll(piece, n * n_heads, axis=1) if n else piece)
    return packed.astype(_BF16)


def _ffn_kernel(h_ref, g_ref, win_ref, wout_ref, o_ref, hid_ref, *, d_ff, chunk):
    x = h_ref[...]
    xn = _rms_norm(x, g_ref[...], NORM_EPS).astype(_BF16)
    for c in range(0, d_ff, chunk):
        gate = _dot(xn, win_ref[:, c:c + chunk])
        up = _dot(xn, win_ref[:, d_ff + c:d_ff + c + chunk])
        hid_ref[:, c:c + chunk] = (gate * jax.nn.sigmoid(gate) * up).astype(_BF16)
    o_ref[...] = x + 0.5 * _dot(hid_ref[...], wout_ref[...])


def _layer_spec(stacked, layer):
    zeros = (0,) * (stacked.ndim - 1)
    return pl.BlockSpec((None,) + stacked.shape[1:], lambda i: (layer,) + zeros, pipeline_mode=pl.Buffered(1))


def _ffn(h, g, w_in, w_out, layer):
    t, d = h.shape
    d_ff = w_out.shape[1]
    tm = min(ROW_TILE, t)
    return pl.pallas_call(
        functools.partial(_ffn_kernel, d_ff=d_ff, chunk=_col_chunk(d_ff)),
        out_shape=jax.ShapeDtypeStruct((t, d), _F32),
        grid=(t // tm,),
        in_specs=[_row_spec(tm, d), _const_spec((1, d)), _layer_spec(w_in, layer), _layer_spec(w_out, layer)],
        out_specs=_row_spec(tm, d),
        scratch_shapes=[pltpu.VMEM((tm, d_ff), _BF16)],
        compiler_params=_params(1),
        name="ffn",
    )(h, g.reshape(1, d), w_in, w_out)


def _post_kernel(*refs, has_attn, final_norm, d_ff, chunk):
    h_ref = refs[0]
    o_ref, wo_ref = refs[1:3] if has_attn else (None, None)
    (gf2_ref, win_ref, wout_ref, p_ref, gp_ref, wg_ref, bg_ref, wp_ref, gfin_ref,
     out_ref, hid_ref) = refs[3 if has_attn else 1:]
    x = h_ref[...]
    if has_attn:
        x = x + _dot(o_ref[...], wo_ref[...])
    xn = _rms_norm(x, gf2_ref[...], NORM_EPS).astype(_BF16)
    for c in range(0, d_ff, chunk):
        gate = _dot(xn, win_ref[:, c:c + chunk])
        up = _dot(xn, win_ref[:, d_ff + c:d_ff + c + chunk])
        hid_ref[:, c:c + chunk] = (gate * jax.nn.sigmoid(gate) * up).astype(_BF16)
    x = x + 0.5 * _dot(hid_ref[...], wout_ref[...])
    xn = _rms_norm(x, gp_ref[...], NORM_EPS).astype(_BF16)
    gate = jax.nn.sigmoid(_dot(xn, wg_ref[...]) + bg_ref[...])
    x = x + gate * _dot(p_ref[...].astype(_BF16), wp_ref[...])
    if final_norm:
        x = _rms_norm(x, gfin_ref[...], NORM_EPS)
    out_ref[...] = x


def _post_mixer(h, attn, g_ffn, w_in, w_out, p, layer, g_ple, w_gate, b_gate, w_proj, g_final, final_norm):
    t, d = h.shape
    d_ff = w_out.shape[1]
    dp = p.shape[2]
    tm = min(ROW_TILE, t)
    args, specs = [h], [_row_spec(tm, d)]
    if attn is not None:
        args += list(attn)
        specs += [_row_spec(tm, attn[0].shape[1]), _const_spec(attn[1].shape)]
    args += [g_ffn.reshape(1, d), w_in, w_out, p, g_ple.reshape(1, d), w_gate, b_gate.reshape(1, d), w_proj,
             g_final.reshape(1, d)]
    specs += [_const_spec((1, d)), _layer_spec(w_in, layer), _layer_spec(w_out, layer),
              pl.BlockSpec((None, tm, dp), lambda i: (layer, i, 0)),
              _const_spec((1, d)), _layer_spec(w_gate, layer), _const_spec((1, d)), _layer_spec(w_proj, layer),
              _const_spec((1, d))]
    return pl.pallas_call(
        functools.partial(_post_kernel, has_attn=attn is not None, final_norm=final_norm, d_ff=d_ff,
                          chunk=_col_chunk(d_ff)),
        out_shape=jax.ShapeDtypeStruct((t, d), _F32),
        grid=(t // tm,),
        in_specs=specs,
        out_specs=_row_spec(tm, d),
        scratch_shapes=[pltpu.VMEM((tm, d_ff), _BF16)],
        compiler_params=_params(1),
        name="post_mixer",
    )(*args)


def _rope_tables(s):
    half = ROT_DIM // 2
    inv_freq = ROPE_THETA ** (-jnp.arange(0, ROT_DIM, 2, dtype=_F32) / ROT_DIM)
    ang = jnp.arange(s, dtype=jnp.int32).astype(_F32)[:, None] * inv_freq[None, :]
    cos, sin = jnp.cos(ang), jnp.sin(ang)
    m = jnp.arange(LANES) % HEAD_DIM
    idx = m % half
    first, second = m < half, (m >= half) & (m < ROT_DIM)
    a = jnp.where((first | second)[None, :], cos[:, idx], 1.0)
    b = jnp.where(first[None, :], -sin[:, idx], 0.0)
    c = jnp.where(second[None, :], sin[:, idx], 0.0)
    return a.astype(_F32), b.astype(_F32), c.astype(_F32)


def _proj_kernel(h_ref, g_ref, w_ref, ra_ref, rb_ref, rc_ref, o_ref, *, n_query, n_rope, chunk):
    xn = _rms_norm(h_ref[...], g_ref[...], NORM_EPS).astype(_BF16)
    half = ROT_DIM // 2
    for c in range(0, w_ref.shape[1], chunk):
        y = _dot(xn, w_ref[:, c:c + chunk])
        if c < n_rope:
            for s in range(0, chunk, LANES):
                ys = y[:, s:s + LANES]
                ys = (ys * ra_ref[...] + pltpu.roll(ys, LANES - half, axis=1) * rb_ref[...]
                      + pltpu.roll(ys, half, axis=1) * rc_ref[...])
                if c < n_query:
                    ys = ys * QK_SCALE
                o_ref[:, c + s:c + s + LANES] = ys.astype(o_ref.dtype)
        else:
            o_ref[:, c:c + chunk] = y.astype(o_ref.dtype)


def _proj(h, g, w, rope, n_query, n_rope, seq):
    t, d = h.shape
    n = w.shape[1]
    tm = min(ROW_TILE, seq)
    chunk = _col_chunk(n)
    assert n_rope % chunk == 0 and n_query % chunk == 0 and n_query <= n_rope
    nblk = seq // tm
    rope_spec = pl.BlockSpec((tm, LANES), lambda i: (i % nblk, 0))
    return pl.pallas_call(
        functools.partial(_proj_kernel, n_query=n_query, n_rope=n_rope, chunk=chunk),
        out_shape=jax.ShapeDtypeStruct((t, n), _BF16),
        grid=(t // tm,),
        in_specs=[_row_spec(tm, d), _const_spec((1, d)), _const_spec(w.shape), rope_spec, rope_spec, rope_spec],
        out_specs=_row_spec(tm, n),
        compiler_params=_params(1),
        name="proj",
    )(h, g.reshape(1, d), w, *rope)


def _fox_proj_kernel(h_ref, g_ref, w_ref, wf_ref, bf_ref, qkv_ref, lf_ref, *, n_query, chunk):
    xn = _rms_norm(h_ref[...], g_ref[...], NORM_EPS).astype(_BF16)
    for c in range(0, w_ref.shape[1], chunk):
        y = _dot(xn, w_ref[:, c:c + chunk])
        if c < n_query:
            y = y * QK_SCALE
        qkv_ref[:, c:c + chunk] = y.astype(qkv_ref.dtype)
    z = _dot(xn, wf_ref[...]) + bf_ref[...]
    lf_ref[...] = jnp.minimum(z, 0.0) - jnp.log1p(jnp.exp(-jnp.abs(z)))


def _fox_proj(h, g, w_qkv, w_f, b_f):
    t, d = h.shape
    n = w_qkv.shape[1]
    tm = min(ROW_TILE, t)
    return pl.pallas_call(
        functools.partial(_fox_proj_kernel, n_query=d, chunk=_col_chunk(n)),
        out_shape=(jax.ShapeDtypeStruct((t, n), _BF16), jax.ShapeDtypeStruct((t, LANES), _F32)),
        grid=(t // tm,),
        in_specs=[_row_spec(tm, d), _const_spec((1, d)), _const_spec(w_qkv.shape), _const_spec(w_f.shape),
                  _const_spec((1, LANES))],
        out_specs=(_row_spec(tm, n), _row_spec(tm, LANES)),
        compiler_params=_params(1),
        name="fox_proj",
    )(h, g.reshape(1, d), w_qkv, w_f, b_f)


def _bias_placement(d):
    pairs = d // LANES
    n_heads = 2 * pairs
    w = np.zeros((3, LANES, d), np.float32)
    ones = np.zeros((3, 1, d), np.float32)
    for hp in range(pairs):
        base = hp * LANES
        for p in range(3):
            w[0, p * n_heads + 2 * hp, base + p] = 1.0
            w[1, p * n_heads + 2 * hp + 1, base + 6 + p] = 1.0
            w[2, p * n_heads + 2 * hp, base + 3 + p] = -1.0
            w[2, p * n_heads + 2 * hp + 1, base + 9 + p] = -1.0
            ones[0, 0, base + 3 + p] = 1.0
            ones[1, 0, base + 9 + p] = 1.0
            ones[2, 0, base + p] = 1.0
            ones[2, 0, base + 6 + p] = 1.0
    return jnp.asarray(w, _BF16), jnp.asarray(ones, _F32)


def _cumsum_kernel(x_ref, w_ref, ones_ref, qea_ref, qeb_ref, ke_ref, carry_ref, *, n_heads):
    @pl.when(pl.program_id(1) == 0)
    def _():
        carry_ref[...] = jnp.zeros_like(carry_ref)

    x = x_ref[...]
    ts = x.shape[0]
    row = lax.broadcasted_iota(jnp.int32, (ts, ts), 0)
    col = lax.broadcasted_iota(jnp.int32, (ts, ts), 1)
    tri = jnp.where(col <= row, 1.0, 0.0).astype(_BF16)
    hi, mid, lo = _split3(x)
    c = _dot(tri, hi) + _dot(tri, mid) + _dot(tri, lo) + carry_ref[0:1, :]
    carry_ref[0:1, :] = c[ts - 1:ts, :]
    pieces = _packed_pieces(c * LOG2E, n_heads)
    for n, out_ref in enumerate((qea_ref, qeb_ref, ke_ref)):
        out_ref[...] = (_dot(pieces, w_ref[n]) + ones_ref[n]).astype(out_ref.dtype)


def _cumsum_bias(x, batch, seq, d, ts):
    ns = seq // ts
    w, ones = _bias_placement(d)
    row_spec = pl.BlockSpec((ts, d), lambda b, s: (b * ns + s, 0))
    out = jax.ShapeDtypeStruct((batch * seq, d), _BF16)
    return pl.pallas_call(
        functools.partial(_cumsum_kernel, n_heads=d // HEAD_DIM),
        out_shape=(out, out, out),
        grid=(batch, ns),
        in_specs=[pl.BlockSpec((ts, LANES), lambda b, s: (b * ns + s, 0)),
                  pl.BlockSpec(w.shape, lambda b, s: (0, 0, 0)), pl.BlockSpec(ones.shape, lambda b, s: (0, 0, 0))],
        out_specs=(row_spec, row_spec, row_spec),
        scratch_shapes=[pltpu.VMEM((8, LANES), _F32)],
        compiler_params=_params(2),
        name="cumsum",
    )(x, w, ones)


def _flash_items(nq, blk):
    items = []
    for i in range(nq):
        for j in range(i // 2):
            items.append((i, 2 * j * blk, 2 * blk, False))
        if i % 2 == 1:
            items.append((i, (i - 1) * blk, 2 * blk, True))
        else:
            items.append((i, i * blk, blk, True))
    return items


def _causal_flash(query_maps, k_refs, value_maps, finish, nq, blk, scratch):
    s_bufs, m_ref, acc_ref = scratch[:-2], scratch[-2], scratch[-1]
    ahead = len(s_bufs) - 1
    items = _flash_items(nq, blk)

    def produce(item, s_ref):
        i, start, width, _ = item
        parts = [r[start:start + width, :] for r in k_refs]
        kb = parts[0] if len(parts) == 1 else jnp.concatenate(parts, axis=1)
        for e, qm in enumerate(query_maps(i)):
            s_ref[e, :, :width] = _dot_nt(qm, kb)

    def consume(item, s_ref, first):
        i, start, width, masked = item
        vbs = value_maps(start, width)
        for e in range(2):
            s = s_ref[e, :, :width]
            if masked:
                row = lax.broadcasted_iota(jnp.int32, (blk, width), 0)
                col = lax.broadcasted_iota(jnp.int32, (blk, width), 1)
                s = jnp.where(col - row <= i * blk - start, s, -jnp.inf)
            m_new = jnp.max(s, axis=1, keepdims=True)
            if not first:
                m_old = m_ref[e]
                m_new = jnp.maximum(m_old, m_new)
            pv = _dot(jnp.exp2((s - m_new).astype(_BF16)), vbs[e])
            acc_ref[e] = pv if first else acc_ref[e] * jnp.exp2(m_old - m_new) + pv
            m_ref[e] = m_new

    for t in range(ahead):
        produce(items[t], s_bufs[t])
    for t, item in enumerate(items):
        if t + ahead < len(items):
            produce(items[t + ahead], s_bufs[(t + ahead) % len(s_bufs)])
        consume(item, s_bufs[t % len(s_bufs)], first=(t == 0 or items[t - 1][0] != item[0]))
        if t + 1 == len(items) or items[t + 1][0] != item[0]:
            finish(item[0], acc_ref[0], acc_ref[1])


def _flash_scratch(blk, n_out, n_score_bufs=2):
    return ([pltpu.VMEM((2, blk, 2 * blk), _F32)] * n_score_bufs
            + [pltpu.VMEM((2, blk, 1), _F32), pltpu.VMEM((2, blk, n_out), _F32)])


def _split_maps(q, lo):
    zero = jnp.zeros_like(q)
    return jnp.where(lo, q, zero), jnp.where(lo, zero, q)


def _fox_attn_kernel(q_ref, qea_ref, qeb_ref, k_ref, ke_ref, v_ref, o_ref, *scratch, blk):
    lo = lax.broadcasted_iota(jnp.int32, (blk, LANES), 1) < HEAD_DIM

    def query_maps(i):
        rows = slice(i * blk, (i + 1) * blk)
        qa, qb = _split_maps(q_ref[rows, :], lo)
        return jnp.concatenate([qa, qea_ref[rows, :]], axis=1), jnp.concatenate([qb, qeb_ref[rows, :]], axis=1)

    def value_maps(start, width):
        vb = v_ref[start:start + width, :]
        keep = lax.broadcasted_iota(jnp.int32, (width, LANES), 1) < HEAD_DIM
        one = jnp.ones_like(vb)
        return jnp.where(keep, vb, one), jnp.where(keep, one, vb)

    def finish(i, acc_a, acc_b):
        out_a = acc_a / pltpu.roll(acc_a, HEAD_DIM, axis=1)
        out_b = acc_b / pltpu.roll(acc_b, HEAD_DIM, axis=1)
        o_ref[i * blk:(i + 1) * blk, :] = jnp.where(lo, out_a, out_b).astype(o_ref.dtype)

    _causal_flash(query_maps, (k_ref, ke_ref), value_maps, finish, q_ref.shape[0] // blk, blk, scratch)


def _fox_attn(qkv, qea, qeb, ke, batch, seq, d):
    blk = min(ATTN_BLOCK, seq)
    pairs = d // LANES
    spec = lambda col0: pl.BlockSpec((seq, LANES), lambda b, h: (b, col0 + h))
    return pl.pallas_call(
        functools.partial(_fox_attn_kernel, blk=blk),
        out_shape=jax.ShapeDtypeStruct((batch * seq, d), _BF16),
        grid=(batch, pairs),
        in_specs=[spec(0), spec(0), spec(0), spec(pairs), spec(0), spec(2 * pairs)],
        out_specs=spec(0),
        scratch_shapes=_flash_scratch(blk, LANES),
        compiler_params=_params(2),
        name="fox_attn",
    )(qkv, qea, qeb, qkv, ke, qkv)


def _fox_mixer(h, g, w_in, b_f, w_out, batch, seq):
    t, d = h.shape
    n_heads = d // HEAD_DIM
    w_qkv = w_in[:, :3 * d].astype(_BF16)
    w_f = jnp.pad(w_in[:, 3 * d:], ((0, 0), (0, LANES - n_heads))).astype(_BF16)
    b_pad = jnp.pad(b_f, (0, LANES - n_heads)).reshape(1, LANES)
    qkv, log_f = _fox_proj(h, g, w_qkv, w_f, b_pad)
    qea, qeb, ke = _cumsum_bias(log_f, batch, seq, d, min(ATTN_BLOCK, seq))
    return _fox_attn(qkv, qea, qeb, ke, batch, seq, d), w_out.astype(_BF16)


def _diff_attn_kernel(q_ref, k_ref, v_ref, lam_ref, g_ref, o_ref, *scratch, blk, lam_init):
    lo = lax.broadcasted_iota(jnp.int32, (blk, LANES), 1) < HEAD_DIM
    lp = lam_ref[...]
    lam = (jnp.exp(jnp.sum(lp[0:1] * lp[1:2], axis=1, keepdims=True))
           - jnp.exp(jnp.sum(lp[2:3] * lp[3:4], axis=1, keepdims=True)) + lam_init)

    def query_maps(i):
        return _split_maps(q_ref[i * blk:(i + 1) * blk, :], lo)

    def value_maps(start, width):
        vb = v_ref[start:start + width, :]
        vb = jnp.concatenate([vb, jnp.ones_like(vb)], axis=1)
        return vb, vb

    def finish(i, acc_1, acc_2):
        o = acc_1[:, :LANES] / acc_1[:, LANES:] - lam * (acc_2[:, :LANES] / acc_2[:, LANES:])
        o = _rms_norm(o, g_ref[...], DIFF_SUBLN_EPS) * (1.0 - lam_init)
        o_ref[i * blk:(i + 1) * blk, :] = o.astype(o_ref.dtype)

    _causal_flash(query_maps, (k_ref,), value_maps, finish, q_ref.shape[0] // blk, blk, scratch)


def _diff_attn(proj, lam_params, subln_g, batch, seq, d, lam_init):
    blk = min(ATTN_BLOCK, seq)
    heads = d // LANES
    spec = lambda col0: pl.BlockSpec((seq, LANES), lambda b, h: (b, col0 + h))
    return pl.pallas_call(
        functools.partial(_diff_attn_kernel, blk=blk, lam_init=lam_init),
        out_shape=jax.ShapeDtypeStruct((batch * seq, d), _BF16),
        grid=(batch, heads),
        in_specs=[spec(0), spec(heads), spec(2 * heads),
                  pl.BlockSpec(lam_params.shape, lambda b, h: (0, 0)), pl.BlockSpec((1, LANES), lambda b, h: (0, 0))],
        out_specs=spec(0),
        scratch_shapes=_flash_scratch(blk, 2 * LANES, 3),
        compiler_params=_params(2),
        name="diff_attn",
    )(proj, proj, proj, lam_params, subln_g.reshape(1, LANES))


def _diff_mixer(h, g, w_in, lam_params, subln_g, w_out, rope, batch, seq, layer_idx):
    t, d = h.shape
    lam_init = 0.8 - 0.6 * math.exp(-0.3 * layer_idx)
    proj = _proj(h, g, w_in.astype(_BF16), rope, d, 2 * d, seq)
    return _diff_attn(proj, lam_params, subln_g, batch, seq, d, lam_init), w_out.astype(_BF16)


def _dil_attn_kernel(q_ref, kp_ref, kc_ref, vp_ref, vc_ref, o_ref, lse_ref, kcat_ref, vcat_ref, s_ref, p_ref, *,
                     n_steps, n_pairs):
    band = DIL_BAND
    n_heads = 2 * n_pairs
    kcat_ref[0:band, :] = kp_ref[...]
    kcat_ref[band:, :] = kc_ref[...]
    vcat_ref[0:band, :] = vp_ref[...]
    vcat_ref[band:, :] = vc_ref[...]
    row = lax.broadcasted_iota(jnp.int32, (band, 2 * band), 0)
    col = lax.broadcasted_iota(jnp.int32, (band, 2 * band), 1)
    in_band = jnp.logical_and(col <= row + band, col >= row + band - n_steps)
    first_valid = jnp.logical_and(in_band, jnp.logical_or(col >= band, pl.program_id(1) > 0))
    lane = lax.broadcasted_iota(jnp.int32, (band, LANES), 1)
    lo = lane < HEAD_DIM
    lo_v = lax.broadcasted_iota(jnp.int32, (2 * band, LANES), 1) < HEAD_DIM
    for r in range(q_ref.shape[0] // band):
        rows = slice(r * band, (r + 1) * band)
        win = slice(r * band, (r + 2) * band)
        valid = first_valid if r == 0 else in_band
        for hp in range(n_pairs):
            sl = slice(hp * LANES, (hp + 1) * LANES)
            q = q_ref[rows, sl]
            zero = jnp.zeros_like(q)
            kw = kcat_ref[win, sl]
            for e, qm in enumerate((jnp.where(lo, q, zero), jnp.where(lo, zero, q))):
                s_ref[(2 * hp + e) * band:(2 * hp + e + 1) * band, :] = jnp.where(valid, _dot_nt(qm, kw), -jnp.inf)
        s = s_ref[...]
        m = jnp.max(s, axis=1, keepdims=True)
        p_ref[...] = jnp.exp2((s - m).astype(_BF16))
        m_tile = jnp.zeros((band, LANES), _F32)
        l_tile = jnp.ones((band, LANES), _F32)
        for hp in range(n_pairs):
            sl = slice(hp * LANES, (hp + 1) * LANES)
            vw = vcat_ref[win, sl]
            one = jnp.ones_like(vw)
            ha, hb = slice((2 * hp) * band, (2 * hp + 1) * band), slice((2 * hp + 1) * band, (2 * hp + 2) * band)
            pv_a = _dot(p_ref[ha, :], jnp.where(lo_v, vw, one))
            pv_b = _dot(p_ref[hb, :], jnp.where(lo_v, one, vw))
            l_a, l_b = pltpu.roll(pv_a, HEAD_DIM, axis=1), pltpu.roll(pv_b, HEAD_DIM, axis=1)
            o_ref[rows, sl] = (jnp.where(lo, pv_a, pv_b) / jnp.where(lo, l_a, l_b)).astype(o_ref.dtype)
            m_tile = jnp.where(lane == 2 * hp, m[ha], jnp.where(lane == 2 * hp + 1, m[hb], m_tile))
            l_tile = jnp.where(lane == 2 * hp, l_a, jnp.where(lane == 2 * hp + 1, pv_b, l_tile))
        lse_ref[rows, :] = (m_tile + jnp.log2(l_tile)) * (1.0 / LOG2E)


def _dil_attn(q, q_col, k, k_col, v, v_col, n_seq, n_sub, d, n_steps):
    assert n_steps <= DIL_BAND
    rb = min(ROW_TILE, n_sub)
    nb = n_sub // rb
    bands = rb // DIL_BAND
    n_heads = d // HEAD_DIM
    cur = lambda col: (lambda x, n: (x * nb + n, col))
    prev = lambda col: (lambda x, n: ((x * nb + n) * bands - jnp.where(n > 0, 1, 0), col))
    return pl.pallas_call(
        functools.partial(_dil_attn_kernel, n_steps=n_steps, n_pairs=d // LANES),
        out_shape=(jax.ShapeDtypeStruct((n_seq * n_sub, d), _BF16),
                   jax.ShapeDtypeStruct((n_seq * n_sub, LANES), _F32)),
        grid=(n_seq, nb),
        in_specs=[pl.BlockSpec((rb, d), cur(q_col)),
                  pl.BlockSpec((DIL_BAND, d), prev(k_col)), pl.BlockSpec((rb, d), cur(k_col)),
                  pl.BlockSpec((DIL_BAND, d), prev(v_col)), pl.BlockSpec((rb, d), cur(v_col))],
        out_specs=(pl.BlockSpec((rb, d), cur(0)), pl.BlockSpec((rb, LANES), cur(0))),
        scratch_shapes=[pltpu.VMEM((rb + DIL_BAND, d), _BF16), pltpu.VMEM((rb + DIL_BAND, d), _BF16),
                        pltpu.VMEM((n_heads * DIL_BAND, 2 * DIL_BAND), _F32),
                        pltpu.VMEM((n_heads * DIL_BAND, 2 * DIL_BAND), _BF16)],
        compiler_params=_params(2),
        name="dil_attn",
    )(q, k, k, v, v)


def _dil_proj_kernel(h_ref, g_ref, w_ref, ra_ref, rb_ref, rc_ref, *refs, d_model, dilations):
    out_refs, ybuf, qbuf = refs[:len(dilations)], refs[len(dilations)], refs[len(dilations) + 1]
    n_groups = len(dilations)
    tm = h_ref.shape[0]
    xn = _rms_norm(h_ref[...], g_ref[...], NORM_EPS).astype(_BF16)
    half = ROT_DIM // 2
    n_slab = 0
    step = qbuf.shape[1]
    for c in range(0, w_ref.shape[1], LANES * 2):
        y = _dot(xn, w_ref[:, c:c + 2 * LANES])
        for s in range(0, 2 * LANES, LANES):
            src, col = divmod(c + s, d_model)
            ys = y[:, s:s + LANES]
            if src <= n_groups:
                ys = (ys * ra_ref[...] + pltpu.roll(ys, LANES - half, axis=1) * rb_ref[...]
                      + pltpu.roll(ys, half, axis=1) * rc_ref[...])
            if src < n_groups:
                ys = ys * QK_SCALE
            dests = [(src, 0)] if src < n_groups else [(g, src - n_groups + 1) for g in range(n_groups)]
            slab, quarter = ybuf.at[n_slab % ybuf.shape[0]], qbuf.at[n_slab % qbuf.shape[0]]
            n_slab += 1
            strides = {dilations[g] for g, _ in dests} - {1}
            if strides:
                slab[...] = ys
            if any(dil % step == 0 for dil in strides):
                firsts = [slab[pl.ds(r, tm // step, stride=step), :] for r in range(step)]
                if any(dil > step for dil in strides):
                    for r in range(step):
                        quarter[r] = firsts[r]
            for g, blk in dests:
                dil = dilations[g]
                lanes = slice(blk * d_model + col, blk * d_model + col + LANES)
                for r in range(dil):
                    if dil == 1:
                        rows = ys
                    elif dil == step:
                        rows = firsts[r]
                    elif dil % step == 0:
                        rows = quarter[r % step, pl.ds(r // step, tm // dil, stride=dil // step), :]
                    else:
                        rows = slab[pl.ds(r, tm // dil, stride=dil), :]
                    out_refs[g][r, :, lanes] = rows.astype(_BF16)


def _dil_proj(h, g, w, rope, batch, seq, dilations):
    t, d = h.shape
    tm = min(ROW_TILE, seq)
    nblk = seq // tm
    rope_spec = pl.BlockSpec((tm, LANES), lambda i: (i % nblk, 0))
    out_shapes = tuple(jax.ShapeDtypeStruct((batch, dil, seq // dil, 3 * d), _BF16) for dil in dilations)
    out_specs = tuple(pl.BlockSpec((None, dil, tm // dil, 3 * d), lambda i: (i // nblk, 0, i % nblk, 0))
                      for dil in dilations)
    return pl.pallas_call(
        functools.partial(_dil_proj_kernel, d_model=d, dilations=dilations),
        out_shape=out_shapes,
        grid=(t // tm,),
        in_specs=[_row_spec(tm, d), _const_spec((1, d)), _const_spec(w.shape), rope_spec, rope_spec, rope_spec],
        out_specs=out_specs,
        scratch_shapes=[pltpu.VMEM((4, tm, LANES), _F32), pltpu.VMEM((4, 4, tm // 4, LANES), _F32)],
        compiler_params=_params(1),
        name="dil_proj",
    )(h, g.reshape(1, d), w, *rope)


def _dil_merge_kernel(h_ref, *refs, dilations):
    n = len(dilations)
    o_refs, l_refs = refs[:n], refs[n:2 * n]
    e_ref, w_ref, out_ref, obuf, lbuf, acc_ref = refs[2 * n:]
    tm, d = h_ref.shape

    def by_position(src_ref, dil, lanes, buf):
        if dil == 1:
            return src_ref[0, :, lanes].astype(_F32)
        for r in range(dil):
            buf[pl.ds(r, tm // dil, stride=dil), :] = src_ref[r, :, lanes].astype(_F32)
        return buf[...]

    lses = [by_position(l_refs[gi], dil, slice(0, LANES), lbuf.at[gi]) for gi, dil in enumerate(dilations)]
    m = functools.reduce(jnp.maximum, lses)
    es = [jnp.exp(l - m) for l in lses]
    den = functools.reduce(lambda a, b: a + b, es)
    wides = [_dot(_packed_pieces(e / den, d // HEAD_DIM), e_ref[...]) for e in es]
    for c in range(0, d, LANES):
        lanes = slice(c, c + LANES)
        acc = jnp.zeros((tm, LANES), _F32)
        for gi, dil in enumerate(dilations):
            acc = acc + wides[gi][:, lanes] * by_position(o_refs[gi], dil, lanes, obuf.at[(c // LANES) % 2, gi])
        acc_ref[:, lanes] = acc.astype(_BF16)
    out_ref[...] = h_ref[...] + _dot(acc_ref[...], w_ref[...])


def _dil_merge(h, outs, lses, w_out, batch, seq, dilations):
    t, d = h.shape
    tm = min(ROW_TILE, seq)
    nblk = seq // tm
    n_heads = d // HEAD_DIM
    piece_row = jnp.arange(LANES)
    expand = ((piece_row % n_heads)[:, None] == (jnp.arange(d) // HEAD_DIM)[None, :]) & (piece_row < 3 * n_heads)[:, None]
    res_spec = lambda dil, n: pl.BlockSpec((None, dil, tm // dil, n), lambda i: (i // nblk, 0, i % nblk, 0))
    n = len(dilations)
    return pl.pallas_call(
        functools.partial(_dil_merge_kernel, dilations=dilations),
        out_shape=jax.ShapeDtypeStruct((t, d), _F32),
        grid=(t // tm,),
        in_specs=([_row_spec(tm, d)] + [res_spec(dil, d) for dil in dilations] + [res_spec(dil, LANES) for dil in dilations]
                  + [_const_spec((LANES, d)), _const_spec(w_out.shape)]),
        out_specs=_row_spec(tm, d),
        scratch_shapes=[pltpu.VMEM((2, n, tm, LANES), _F32), pltpu.VMEM((n, tm, LANES), _F32),
                        pltpu.VMEM((tm, d), _BF16)],
        compiler_params=_params(1),
        name="dil_merge",
    )(h, *outs, *lses, expand.astype(_BF16), w_out)


def _dil_mixer(h, g, w_in, w_out, rope, batch, seq):
    t, d = h.shape
    dilations = tuple(dil for _, dil in DIL_CONFIGS)
    projs = _dil_proj(h, g, w_in.astype(_BF16), rope, batch, seq, dilations)
    outs, lses = [], []
    for (window, dil), proj in zip(DIL_CONFIGS, projs):
        assert seq % (dil * DIL_BAND) == 0
        proj = proj.reshape(t, 3 * d)
        o, lse = _dil_attn(proj, 0, proj, 1, proj, 2, batch * dil, seq // dil, d, window // dil)
        outs.append(o.reshape(batch, dil, seq // dil, d))
        lses.append(lse.reshape(batch, dil, seq // dil, LANES))
    return _dil_merge(h, outs, lses, w_out.astype(_BF16), batch, seq, dilations)


def _gelu(x):
    return 0.5 * x * (1.0 + lax.erf(x * math.sqrt(0.5)))


def _sgu_kernel(h_ref, g_ref, win_ref, nv_ref, ws_ref, bs_ref, wout_ref, o_ref, u_ref, v_ref, y_ref, *, chunk):
    x = h_ref[...]
    tm = x.shape[0]
    half = wout_ref.shape[0]
    gdim = half // SGU_GROUPS
    xn = _rms_norm(x, g_ref[...], NORM_EPS).astype(_BF16)
    ssq = jnp.zeros((tm, 1), _F32)
    for c in range(0, half, chunk):
        u_ref[:, c:c + chunk] = _gelu(_dot(xn, win_ref[:, c:c + chunk]))
        vv = _gelu(_dot(xn, win_ref[:, half + c:half + c + chunk]))
        v_ref[:, c:c + chunk] = vv
        ssq = ssq + jnp.sum(vv * vv, axis=1, keepdims=True)
    rstd = lax.rsqrt(ssq / half + NORM_EPS)
    row = lax.broadcasted_iota(jnp.int32, (SGU_CHUNK, SGU_CHUNK), 0)
    col = lax.broadcasted_iota(jnp.int32, (SGU_CHUNK, SGU_CHUNK), 1)
    for gi in range(SGU_GROUPS):
        cols = slice(gi * gdim, (gi + 1) * gdim)
        w = jnp.where(col <= row, ws_ref[gi], jnp.zeros((), _BF16))
        bias = bs_ref[:, gi:gi + 1]
        for r in range(0, tm, SGU_CHUNK):
            rows = slice(r, r + SGU_CHUNK)
            vn = (v_ref[rows, cols] * rstd[rows] * nv_ref[:, cols]).astype(_BF16)
            y_ref[rows, cols] = (u_ref[rows, cols] * (_dot(w, vn) + bias)).astype(_BF16)
    o_ref[...] = x + _dot(y_ref[...], wout_ref[...])


def _sgu_mixer(h, g, w_in, norm_v, w_s, b_s, w_out):
    t, d = h.shape
    half = w_out.shape[0]
    tm = min(ROW_TILE, t)
    assert tm % SGU_CHUNK == 0 and (half // SGU_GROUPS) % LANES == 0
    return pl.pallas_call(
        functools.partial(_sgu_kernel, chunk=_col_chunk(half)),
        out_shape=jax.ShapeDtypeStruct((t, d), _F32),
        grid=(t // tm,),
        in_specs=[_row_spec(tm, d), _const_spec((1, d)), _const_spec(w_in.shape), _const_spec((1, half)),
                  _const_spec(w_s.shape), _const_spec((SGU_CHUNK, SGU_GROUPS)), _const_spec(w_out.shape)],
        out_specs=_row_spec(tm, d),
        scratch_shapes=[pltpu.VMEM((tm, half), _F32), pltpu.VMEM((tm, half), _F32), pltpu.VMEM((tm, half), _BF16)],
        compiler_params=_params(1),
        name="sgu",
    )(h, g.reshape(1, d), w_in.astype(_BF16), norm_v.reshape(1, half), w_s.astype(_BF16), jnp.transpose(b_s),
      w_out.astype(_BF16))


def kernel(x, p, norm_ffn1, w_ffn1_in, w_ffn1_out, norm_mix, norm_ffn2, w_ffn2_in, w_ffn2_out, norm_ple, w_ple_gate, b_ple_gate, w_ple_proj, fox_w_in, fox_b_f, fox_w_out, dil_w_in, dil_w_out, diff_w_in, diff_lambda, diff_subln, diff_w_out, sgu_w_in, sgu_norm_v, sgu_w_s, sgu_b_s, sgu_w_out, norm_final):
    batch, seq, d = x.shape
    depth = p.shape[0]
    t = batch * seq
    h = x.reshape(t, d)
    rope = _rope_tables(seq)
    n_mixers = 4
    ffn1_in, ffn1_out = w_ffn1_in.astype(_BF16), w_ffn1_out.astype(_BF16)
    ffn2_in, ffn2_out = w_ffn2_in.astype(_BF16), w_ffn2_out.astype(_BF16)
    ple_gate, ple_proj = w_ple_gate.astype(_BF16), w_ple_proj.astype(_BF16)
    for i in range(depth):
        kind, j = i % n_mixers, i // n_mixers
        h = _ffn(h, norm_ffn1[i], ffn1_in, ffn1_out, i)
        attn = None
        if kind == 0:
            attn = _fox_mixer(h, norm_mix[i], fox_w_in[j], fox_b_f[j], fox_w_out[j], batch, seq)
        elif kind == 1:
            h = _dil_mixer(h, norm_mix[i], dil_w_in[j], dil_w_out[j], rope, batch, seq)
        elif kind == 2:
            attn = _diff_mixer(h, norm_mix[i], diff_w_in[j], diff_lambda[j], diff_subln[j], diff_w_out[j], rope,
                               batch, seq, i)
        else:
            h = _sgu_mixer(h, norm_mix[i], sgu_w_in[j], sgu_norm_v[j], sgu_w_s[j], sgu_b_s[j], sgu_w_out[j])
        h = _post_mixer(h, attn, norm_ffn2[i], ffn2_in, ffn2_out, p.reshape(depth, t, p.shape[-1]), i, norm_ple[i],
                        ple_gate, b_ple_gate[i], ple_proj, norm_final, final_norm=(i == depth - 1))
    return h.reshape(batch, seq, d)
```

```python
import functools
import math

import jax
import jax.numpy as jnp
import numpy as np
from jax import lax
from jax.experimental import pallas as pl
from jax.experimental.pallas import tpu as pltpu

_F32 = jnp.float32
_BF16 = jnp.bfloat16

HEAD_DIM = 64
ROPE_THETA = 500000.0
ROT_DIM = HEAD_DIM // 4
NORM_EPS = 1e-6
DIL_CONFIGS = ((128, 1), (512, 4), (2048, 16))
DIL_BAND = 128
DIFF_SUBLN_EPS = 1e-5
SGU_GROUPS = 8
SGU_CHUNK = 128

LANES = 128
VMEM_LIMIT_BYTES = 56 * 1024 * 1024
ROW_TILE = 512
ATTN_BLOCK = 512
COL_CHUNK = 256
LOG2E = math.log2(math.e)
QK_SCALE = HEAD_DIM ** -0.5 * LOG2E


def _params(n_axes):
    return pltpu.CompilerParams(dimension_semantics=("arbitrary",) * n_axes,
                                vmem_limit_bytes=VMEM_LIMIT_BYTES)


def _col_chunk(n, target=COL_CHUNK):
    best = LANES
    for c in range(LANES, target + 1, LANES):
        if n % c == 0:
            best = c
    assert n % best == 0, (n, best)
    return best


def _row_spec(tm, n):
    return pl.BlockSpec((tm, n), lambda i: (i, 0))


def _const_spec(shape):
    return pl.BlockSpec(shape, lambda i: (0,) * len(shape))


def _rms_norm(x, g, eps):
    return x * lax.rsqrt(jnp.mean(x * x, axis=-1, keepdims=True) + eps) * g


def _dot(a, b):
    return jnp.dot(a, b, preferred_element_type=_F32)


def _dot_nt(a, b):
    return lax.dot_general(a, b, (((1,), (1,)), ((), ())), preferred_element_type=_F32)


def _split3(x):
    hi = x.astype(_BF16)
    r1 = x - hi.astype(_F32)
    mid = r1.astype(_BF16)
    lo = (r1 - mid.astype(_F32)).astype(_BF16)
    return hi, mid, lo


def _packed_pieces(x, n_heads):
    assert 3 * n_heads <= LANES
    lane = lax.broadcasted_iota(jnp.int32, x.shape, 1)
    packed = jnp.zeros_like(x)
    for n, piece in enumerate(_split3(x)):
        piece = jnp.where(lane < n_heads, piece.astype(_F32), 0.0)
        packed = packed + (pltpu.roll(piece, n * n_heads, axis=1) if n else piece)
    return packed.astype(_BF16)


def _ffn_kernel(h_ref, g_ref, win_ref, wout_ref, o_ref, hid_ref, *, d_ff, chunk):
    x = h_ref[...]
    xn = _rms_norm(x, g_ref[...], NORM_EPS).astype(_BF16)
    for c in range(0, d_ff, chunk):
        gate = _dot(xn, win_ref[:, c:c + chunk])
        up = _dot(xn, win_ref[:, d_ff + c:d_ff + c + chunk])
        hid_ref[:, c:c + chunk] = (gate * jax.nn.sigmoid(gate) * up).astype(_BF16)
    o_ref[...] = x + 0.5 * _dot(hid_ref[...], wout_ref[...])


def _layer_spec(stacked, layer):
    zeros = (0,) * (stacked.ndim - 1)
    return pl.BlockSpec((None,) + stacked.shape[1:], lambda i: (layer,) + zeros, pipeline_mode=pl.Buffered(1))


def _ffn(h, g, w_in, w_out, layer):
    t, d = h.shape
    d_ff = w_out.shape[1]
    tm = min(ROW_TILE, t)
    return pl.pallas_call(
        functools.partial(_ffn_kernel, d_ff=d_ff, chunk=_col_chunk(d_ff)),
        out_shape=jax.ShapeDtypeStruct((t, d), _F32),
        grid=(t // tm,),
        in_specs=[_row_spec(tm, d), _const_spec((1, d)), _layer_spec(w_in, layer), _layer_spec(w_out, layer)],
        out_specs=_row_spec(tm, d),
        scratch_shapes=[pltpu.VMEM((tm, d_ff), _BF16)],
        compiler_params=_params(1),
        name="ffn",
    )(h, g.reshape(1, d), w_in, w_out)


def _post_kernel(*refs, has_attn, final_norm, d_ff, chunk):
    h_ref = refs[0]
    o_ref, wo_ref = refs[1:3] if has_attn else (None, None)
    (gf2_ref, win_ref, wout_ref, p_ref, gp_ref, wg_ref, bg_ref, wp_ref, gfin_ref,
     out_ref, hid_ref) = refs[3 if has_attn else 1:]
    x = h_ref[...]
    if has_attn:
        x = x + _dot(o_ref[...], wo_ref[...])
    xn = _rms_norm(x, gf2_ref[...], NORM_EPS).astype(_BF16)
    for c in range(0, d_ff, chunk):
        gate = _dot(xn, win_ref[:, c:c + chunk])
        up = _dot(xn, win_ref[:, d_ff + c:d_ff + c + chunk])
        hid_ref[:, c:c + chunk] = (gate * jax.nn.sigmoid(gate) * up).astype(_BF16)
    x = x + 0.5 * _dot(hid_ref[...], wout_ref[...])
    xn = _rms_norm(x, gp_ref[...], NORM_EPS).astype(_BF16)
    gate = jax.nn.sigmoid(_dot(xn, wg_ref[...]) + bg_ref[...])
    x = x + gate * _dot(p_ref[...].astype(_BF16), wp_ref[...])
    if final_norm:
        x = _rms_norm(x, gfin_ref[...], NORM_EPS)
    out_ref[...] = x


def _post_mixer(h, attn, g_ffn, w_in, w_out, p, layer, g_ple, w_gate, b_gate, w_proj, g_final, final_norm):
    t, d = h.shape
    d_ff = w_out.shape[1]
    dp = p.shape[2]
    tm = min(ROW_TILE, t)
    args, specs = [h], [_row_spec(tm, d)]
    if attn is not None:
        args += list(attn)
        specs += [_row_spec(tm, attn[0].shape[1]), _const_spec(attn[1].shape)]
    args += [g_ffn.reshape(1, d), w_in, w_out, p, g_ple.reshape(1, d), w_gate, b_gate.reshape(1, d), w_proj,
             g_final.reshape(1, d)]
    specs += [_const_spec((1, d)), _layer_spec(w_in, layer), _layer_spec(w_out, layer),
              pl.BlockSpec((None, tm, dp), lambda i: (layer, i, 0)),
              _const_spec((1, d)), _layer_spec(w_gate, layer), _const_spec((1, d)), _layer_spec(w_proj, layer),
              _const_spec((1, d))]
    return pl.pallas_call(
        functools.partial(_post_kernel, has_attn=attn is not None, final_norm=final_norm, d_ff=d_ff,
                          chunk=_col_chunk(d_ff)),
        out_shape=jax.ShapeDtypeStruct((t, d), _F32),
        grid=(t // tm,),
        in_specs=specs,
        out_specs=_row_spec(tm, d),
        scratch_shapes=[pltpu.VMEM((tm, d_ff), _BF16)],
        compiler_params=_params(1),
        name="post_mixer",
    )(*args)


def _rope_tables(s):
    half = ROT_DIM // 2
    inv_freq = ROPE_THETA ** (-jnp.arange(0, ROT_DIM, 2, dtype=_F32) / ROT_DIM)
    ang = jnp.arange(s, dtype=jnp.int32).astype(_F32)[:, None] * inv_freq[None, :]
    cos, sin = jnp.cos(ang), jnp.sin(ang)
    m = jnp.arange(LANES) % HEAD_DIM
    idx = m % half
    first, second = m < half, (m >= half) & (m < ROT_DIM)
    a = jnp.where((first | second)[None, :], cos[:, idx], 1.0)
    b = jnp.where(first[None, :], -sin[:, idx], 0.0)
    c = jnp.where(second[None, :], sin[:, idx], 0.0)
    return a.astype(_F32), b.astype(_F32), c.astype(_F32)


def _proj_kernel(h_ref, g_ref, w_ref, ra_ref, rb_ref, rc_ref, o_ref, *, n_query, n_rope, chunk):
    xn = _rms_norm(h_ref[...], g_ref[...], NORM_EPS).astype(_BF16)
    half = ROT_DIM // 2
    for c in range(0, w_ref.shape[1], chunk):
        y = _dot(xn, w_ref[:, c:c + chunk])
        if c < n_rope:
            for s in range(0, chunk, LANES):
                ys = y[:, s:s + LANES]
                ys = (ys * ra_ref[...] + pltpu.roll(ys, LANES - half, axis=1) * rb_ref[...]
                      + pltpu.roll(ys, half, axis=1) * rc_ref[...])
                if c < n_query:
                    ys = ys * QK_SCALE
                o_ref[:, c + s:c + s + LANES] = ys.astype(o_ref.dtype)
        else:
            o_ref[:, c:c + chunk] = y.astype(o_ref.dtype)


def _proj(h, g, w, rope, n_query, n_rope, seq):
    t, d = h.shape
    n = w.shape[1]
    tm = min(ROW_TILE, seq)
    chunk = _col_chunk(n)
    assert n_rope % chunk == 0 and n_query % chunk == 0 and n_query <= n_rope
    nblk = seq // tm
    rope_spec = pl.BlockSpec((tm, LANES), lambda i: (i % nblk, 0))
    return pl.pallas_call(
        functools.partial(_proj_kernel, n_query=n_query, n_rope=n_rope, chunk=chunk),
        out_shape=jax.ShapeDtypeStruct((t, n), _BF16),
        grid=(t // tm,),
        in_specs=[_row_spec(tm, d), _const_spec((1, d)), _const_spec(w.shape), rope_spec, rope_spec, rope_spec],
        out_specs=_row_spec(tm, n),
        compiler_params=_params(1),
        name="proj",
    )(h, g.reshape(1, d), w, *rope)


def _fox_proj_kernel(h_ref, g_ref, w_ref, wf_ref, bf_ref, qkv_ref, lf_ref, *, n_query, chunk):
    xn = _rms_norm(h_ref[...], g_ref[...], NORM_EPS).astype(_BF16)
    for c in range(0, w_ref.shape[1], chunk):
        y = _dot(xn, w_ref[:, c:c + chunk])
        if c < n_query:
            y = y * QK_SCALE
        qkv_ref[:, c:c + chunk] = y.astype(qkv_ref.dtype)
    z = _dot(xn, wf_ref[...]) + bf_ref[...]
    lf_ref[...] = jnp.minimum(z, 0.0) - jnp.log1p(jnp.exp(-jnp.abs(z)))


def _fox_proj(h, g, w_qkv, w_f, b_f):
    t, d = h.shape
    n = w_qkv.shape[1]
    tm = min(ROW_TILE, t)
    return pl.pallas_call(
        functools.partial(_fox_proj_kernel, n_query=d, chunk=_col_chunk(n)),
        out_shape=(jax.ShapeDtypeStruct((t, n), _BF16), jax.ShapeDtypeStruct((t, LANES), _F32)),
        grid=(t // tm,),
        in_specs=[_row_spec(tm, d), _const_spec((1, d)), _const_spec(w_qkv.shape), _const_spec(w_f.shape),
                  _const_spec((1, LANES))],
        out_specs=(_row_spec(tm, n), _row_spec(tm, LANES)),
        compiler_params=_params(1),
        name="fox_proj",
    )(h, g.reshape(1, d), w_qkv, w_f, b_f)


def _bias_placement(d):
    pairs = d // LANES
    n_heads = 2 * pairs
    w = np.zeros((3, LANES, d), np.float32)
    ones = np.zeros((3, 1, d), np.float32)
    for hp in range(pairs):
        base = hp * LANES
        for p in range(3):
            w[0, p * n_heads + 2 * hp, base + p] = 1.0
            w[1, p * n_heads + 2 * hp + 1, base + 6 + p] = 1.0
            w[2, p * n_heads + 2 * hp, base + 3 + p] = -1.0
            w[2, p * n_heads + 2 * hp + 1, base + 9 + p] = -1.0
            ones[0, 0, base + 3 + p] = 1.0
            ones[1, 0, base + 9 + p] = 1.0
            ones[2, 0, base + p] = 1.0
            ones[2, 0, base + 6 + p] = 1.0
    return jnp.asarray(w, _BF16), jnp.asarray(ones, _F32)


def _cumsum_kernel(x_ref, w_ref, ones_ref, qea_ref, qeb_ref, ke_ref, carry_ref, *, n_heads):
    @pl.when(pl.program_id(1) == 0)
    def _():
        carry_ref[...] = jnp.zeros_like(carry_ref)

    x = x_ref[...]
    ts = x.shape[0]
    row = lax.broadcasted_iota(jnp.int32, (ts, ts), 0)
    col = lax.broadcasted_iota(jnp.int32, (ts, ts), 1)
    tri = jnp.where(col <= row, 1.0, 0.0).astype(_BF16)
    hi, mid, lo = _split3(x)
    c = _dot(tri, hi) + _dot(tri, mid) + _dot(tri, lo) + carry_ref[0:1, :]
    carry_ref[0:1, :] = c[ts - 1:ts, :]
    pieces = _packed_pieces(c * LOG2E, n_heads)
    for n, out_ref in enumerate((qea_ref, qeb_ref, ke_ref)):
        out_ref[...] = (_dot(pieces, w_ref[n]) + ones_ref[n]).astype(out_ref.dtype)


def _cumsum_bias(x, batch, seq, d, ts):
    ns = seq // ts
    w, ones = _bias_placement(d)
    row_spec = pl.BlockSpec((ts, d), lambda b, s: (b * ns + s, 0))
    out = jax.ShapeDtypeStruct((batch * seq, d), _BF16)
    return pl.pallas_call(
        functools.partial(_cumsum_kernel, n_heads=d // HEAD_DIM),
        out_shape=(out, out, out),
        grid=(batch, ns),
        in_specs=[pl.BlockSpec((ts, LANES), lambda b, s: (b * ns + s, 0)),
                  pl.BlockSpec(w.shape, lambda b, s: (0, 0, 0)), pl.BlockSpec(ones.shape, lambda b, s: (0, 0, 0))],
        out_specs=(row_spec, row_spec, row_spec),
        scratch_shapes=[pltpu.VMEM((8, LANES), _F32)],
        compiler_params=_params(2),
        name="cumsum",
    )(x, w, ones)


def _flash_items(nq, blk):
    items = []
    for i in range(nq):
        for j in range(i // 2):
            items.append((i, 2 * j * blk, 2 * blk, False))
        if i % 2 == 1:
            items.append((i, (i - 1) * blk, 2 * blk, True))
        else:
            items.append((i, i * blk, blk, True))
    return items


def _causal_flash(query_maps, k_refs, value_maps, finish, nq, blk, scratch):
    s_bufs, m_ref, acc_ref = scratch[:-2], scratch[-2], scratch[-1]
    ahead = len(s_bufs) - 1
    items = _flash_items(nq, blk)

    def produce(item, s_ref):
        i, start, width, _ = item
        parts = [r[start:start + width, :] for r in k_refs]
        kb = parts[0] if len(parts) == 1 else jnp.concatenate(parts, axis=1)
        for e, qm in enumerate(query_maps(i)):
            s_ref[e, :, :width] = _dot_nt(qm, kb)

    def consume(item, s_ref, first):
        i, start, width, masked = item
        vbs = value_maps(start, width)
        for e in range(2):
            s = s_ref[e, :, :width]
            if masked:
                row = lax.broadcasted_iota(jnp.int32, (blk, width), 0)
                col = lax.broadcasted_iota(jnp.int32, (blk, width), 1)
                s = jnp.where(col - row <= i * blk - start, s, -jnp.inf)
            m_new = jnp.max(s, axis=1, keepdims=True)
            if not first:
                m_old = m_ref[e]
                m_new = jnp.maximum(m_old, m_new)
            pv = _dot(jnp.exp2((s - m_new).astype(_BF16)), vbs[e])
            acc_ref[e] = pv if first else acc_ref[e] * jnp.exp2(m_old - m_new) + pv
            m_ref[e] = m_new

    for t in range(ahead):
        produce(items[t], s_bufs[t])
    for t, item in enumerate(items):
        if t + ahead < len(items):
            produce(items[t + ahead], s_bufs[(t + ahead) % len(s_bufs)])
        consume(item, s_bufs[t % len(s_bufs)], first=(t == 0 or items[t - 1][0] != item[0]))
        if t + 1 == len(items) or items[t + 1][0] != item[0]:
            finish(item[0], acc_ref[0], acc_ref[1])


def _flash_scratch(blk, n_out, n_score_bufs=2):
    return ([pltpu.VMEM((2, blk, 2 * blk), _F32)] * n_score_bufs
            + [pltpu.VMEM((2, blk, 1), _F32), pltpu.VMEM((2, blk, n_out), _F32)])


def _split_maps(q, lo):
    zero = jnp.zeros_like(q)
    return jnp.where(lo, q, zero), jnp.where(lo, zero, q)


def _fox_attn_kernel(q_ref, qea_ref, qeb_ref, k_ref, ke_ref, v_ref, o_ref, *scratch, blk):
    lo = lax.broadcasted_iota(jnp.int32, (blk, LANES), 1) < HEAD_DIM

    def query_maps(i):
        rows = slice(i * blk, (i + 1) * blk)
        qa, qb = _split_maps(q_ref[rows, :], lo)
        return jnp.concatenate([qa, qea_ref[rows, :]], axis=1), jnp.concatenate([qb, qeb_ref[rows, :]], axis=1)

    def value_maps(start, width):
        vb = v_ref[start:start + width, :]
        keep = lax.broadcasted_iota(jnp.int32, (width, LANES), 1) < HEAD_DIM
        one = jnp.ones_like(vb)
        return jnp.where(keep, vb, one), jnp.where(keep, one, vb)

    def finish(i, acc_a, acc_b):
        out_a = acc_a / pltpu.roll(acc_a, HEAD_DIM, axis=1)
        out_b = acc_b / pltpu.roll(acc_b, HEAD_DIM, axis=1)
        o_ref[i * blk:(i + 1) * blk, :] = jnp.where(lo, out_a, out_b).astype(o_ref.dtype)

    _causal_flash(query_maps, (k_ref, ke_ref), value_maps, finish, q_ref.shape[0] // blk, blk, scratch)


def _fox_attn(qkv, qea, qeb, ke, batch, seq, d):
    blk = min(ATTN_BLOCK, seq)
    pairs = d // LANES
    spec = lambda col0: pl.BlockSpec((seq, LANES), lambda b, h: (b, col0 + h))
    return pl.pallas_call(
        functools.partial(_fox_attn_kernel, blk=blk),
        out_shape=jax.ShapeDtypeStruct((batch * seq, d), _BF16),
        grid=(batch, pairs),
        in_specs=[spec(0), spec(0), spec(0), spec(pairs), spec(0), spec(2 * pairs)],
        out_specs=spec(0),
        scratch_shapes=_flash_scratch(blk, LANES, 3),
        compiler_params=_params(2),
        name="fox_attn",
    )(qkv, qea, qeb, qkv, ke, qkv)


def _fox_mixer(h, g, w_in, b_f, w_out, batch, seq):
    t, d = h.shape
    n_heads = d // HEAD_DIM
    w_qkv = w_in[:, :3 * d].astype(_BF16)
    w_f = jnp.pad(w_in[:, 3 * d:], ((0, 0), (0, LANES - n_heads))).astype(_BF16)
    b_pad = jnp.pad(b_f, (0, LANES - n_heads)).reshape(1, LANES)
    qkv, log_f = _fox_proj(h, g, w_qkv, w_f, b_pad)
    qea, qeb, ke = _cumsum_bias(log_f, batch, seq, d, min(ATTN_BLOCK, seq))
    return _fox_attn(qkv, qea, qeb, ke, batch, seq, d), w_out.astype(_BF16)


def _diff_attn_kernel(q_ref, k_ref, v_ref, lam_ref, g_ref, o_ref, *scratch, blk, lam_init):
    lo = lax.broadcasted_iota(jnp.int32, (blk, LANES), 1) < HEAD_DIM
    lp = lam_ref[...]
    lam = (jnp.exp(jnp.sum(lp[0:1] * lp[1:2], axis=1, keepdims=True))
           - jnp.exp(jnp.sum(lp[2:3] * lp[3:4], axis=1, keepdims=True)) + lam_init)

    def query_maps(i):
        return _split_maps(q_ref[i * blk:(i + 1) * blk, :], lo)

    def value_maps(start, width):
        vb = v_ref[start:start + width, :]
        vb = jnp.concatenate([vb, jnp.ones_like(vb)], axis=1)
        return vb, vb

    def finish(i, acc_1, acc_2):
        o = acc_1[:, :LANES] / acc_1[:, LANES:] - lam * (acc_2[:, :LANES] / acc_2[:, LANES:])
        o = _rms_norm(o, g_ref[...], DIFF_SUBLN_EPS) * (1.0 - lam_init)
        o_ref[i * blk:(i + 1) * blk, :] = o.astype(o_ref.dtype)

    _causal_flash(query_maps, (k_ref,), value_maps, finish, q_ref.shape[0] // blk, blk, scratch)


def _diff_attn(proj, lam_params, subln_g, batch, seq, d, lam_init):
    blk = min(ATTN_BLOCK, seq)
    heads = d // LANES
    spec = lambda col0: pl.BlockSpec((seq, LANES), lambda b, h: (b, col0 + h))
    return pl.pallas_call(
        functools.partial(_diff_attn_kernel, blk=blk, lam_init=lam_init),
        out_shape=jax.ShapeDtypeStruct((batch * seq, d), _BF16),
        grid=(batch, heads),
        in_specs=[spec(0), spec(heads), spec(2 * heads),
                  pl.BlockSpec(lam_params.shape, lambda b, h: (0, 0)), pl.BlockSpec((1, LANES), lambda b, h: (0, 0))],
        out_specs=spec(0),
        scratch_shapes=_flash_scratch(blk, 2 * LANES, 4),
        compiler_params=_params(2),
        name="diff_attn",
    )(proj, proj, proj, lam_params, subln_g.reshape(1, LANES))


def _diff_mixer(h, g, w_in, lam_params, subln_g, w_out, rope, batch, seq, layer_idx):
    t, d = h.shape
    lam_init = 0.8 - 0.6 * math.exp(-0.3 * layer_idx)
    proj = _proj(h, g, w_in.astype(_BF16), rope, d, 2 * d, seq)
    return _diff_attn(proj, lam_params, subln_g, batch, seq, d, lam_init), w_out.astype(_BF16)


def _dil_attn_kernel(q_ref, kp_ref, kc_ref, vp_ref, vc_ref, o_ref, lse_ref, kcat_ref, vcat_ref, s_ref, p_ref, *,
                     n_steps, n_pairs):
    band = DIL_BAND
    n_heads = 2 * n_pairs
    kcat_ref[0:band, :] = kp_ref[...]
    kcat_ref[band:, :] = kc_ref[...]
    vcat_ref[0:band, :] = vp_ref[...]
    vcat_ref[band:, :] = vc_ref[...]
    row = lax.broadcasted_iota(jnp.int32, (band, 2 * band), 0)
    col = lax.broadcasted_iota(jnp.int32, (band, 2 * band), 1)
    in_band = jnp.logical_and(col <= row + band, col >= row + band - n_steps)
    first_valid = jnp.logical_and(in_band, jnp.logical_or(col >= band, pl.program_id(1) > 0))
    lane = lax.broadcasted_iota(jnp.int32, (band, LANES), 1)
    lo = lane < HEAD_DIM
    lo_v = lax.broadcasted_iota(jnp.int32, (2 * band, LANES), 1) < HEAD_DIM
    for r in range(q_ref.shape[0] // band):
        rows = slice(r * band, (r + 1) * band)
        win = slice(r * band, (r + 2) * band)
        valid = first_valid if r == 0 else in_band
        for hp in range(n_pairs):
            sl = slice(hp * LANES, (hp + 1) * LANES)
            q = q_ref[rows, sl]
            zero = jnp.zeros_like(q)
            kw = kcat_ref[win, sl]
            for e, qm in enumerate((jnp.where(lo, q, zero), jnp.where(lo, zero, q))):
                s_ref[(2 * hp + e) * band:(2 * hp + e + 1) * band, :] = jnp.where(valid, _dot_nt(qm, kw), -jnp.inf)
        s = s_ref[...]
        m = jnp.max(s, axis=1, keepdims=True)
        p_ref[...] = jnp.exp2((s - m).astype(_BF16))
        m_tile = jnp.zeros((band, LANES), _F32)
        l_tile = jnp.ones((band, LANES), _F32)
        for hp in range(n_pairs):
            sl = slice(hp * LANES, (hp + 1) * LANES)
            vw = vcat_ref[win, sl]
            one = jnp.ones_like(vw)
            ha, hb = slice((2 * hp) * band, (2 * hp + 1) * band), slice((2 * hp + 1) * band, (2 * hp + 2) * band)
            pv_a = _dot(p_ref[ha, :], jnp.where(lo_v, vw, one))
            pv_b = _dot(p_ref[hb, :], jnp.where(lo_v, one, vw))
            l_a, l_b = pltpu.roll(pv_a, HEAD_DIM, axis=1), pltpu.roll(pv_b, HEAD_DIM, axis=1)
            o_ref[rows, sl] = (jnp.where(lo, pv_a, pv_b) / jnp.where(lo, l_a, l_b)).astype(o_ref.dtype)
            m_tile = jnp.where(lane == 2 * hp, m[ha], jnp.where(lane == 2 * hp + 1, m[hb], m_tile))
            l_tile = jnp.where(lane == 2 * hp, l_a, jnp.where(lane == 2 * hp + 1, pv_b, l_tile))
        lse_ref[rows, :] = (m_tile + jnp.log2(l_tile)) * (1.0 / LOG2E)


def _dil_attn(q, q_col, k, k_col, v, v_col, n_seq, n_sub, d, n_steps):
    assert n_steps <= DIL_BAND
    rb = min(ROW_TILE, n_sub)
    nb = n_sub // rb
    bands = rb // DIL_BAND
    n_heads = d // HEAD_DIM
    cur = lambda col: (lambda x, n: (x * nb + n, col))
    prev = lambda col: (lambda x, n: ((x * nb + n) * bands - jnp.where(n > 0, 1, 0), col))
    return pl.pallas_call(
        functools.partial(_dil_attn_kernel, n_steps=n_steps, n_pairs=d // LANES),
        out_shape=(jax.ShapeDtypeStruct((n_seq * n_sub, d), _BF16),
                   jax.ShapeDtypeStruct((n_seq * n_sub, LANES), _F32)),
        grid=(n_seq, nb),
        in_specs=[pl.BlockSpec((rb, d), cur(q_col)),
                  pl.BlockSpec((DIL_BAND, d), prev(k_col)), pl.BlockSpec((rb, d), cur(k_col)),
                  pl.BlockSpec((DIL_BAND, d), prev(v_col)), pl.BlockSpec((rb, d), cur(v_col))],
        out_specs=(pl.BlockSpec((rb, d), cur(0)), pl.BlockSpec((rb, LANES), cur(0))),
        scratch_shapes=[pltpu.VMEM((rb + DIL_BAND, d), _BF16), pltpu.VMEM((rb + DIL_BAND, d), _BF16),
                        pltpu.VMEM((n_heads * DIL_BAND, 2 * DIL_BAND), _F32),
                        pltpu.VMEM((n_heads * DIL_BAND, 2 * DIL_BAND), _BF16)],
        compiler_params=_params(2),
        name="dil_attn",
    )(q, k, k, v, v)


def _dil_proj_kernel(h_ref, g_ref, w_ref, ra_ref, rb_ref, rc_ref, *refs, d_model, dilations):
    out_refs, ybuf, qbuf = refs[:len(dilations)], refs[len(dilations)], refs[len(dilations) + 1]
    n_groups = len(dilations)
    tm = h_ref.shape[0]
    xn = _rms_norm(h_ref[...], g_ref[...], NORM_EPS).astype(_BF16)
    half = ROT_DIM // 2
    n_slab = 0
    step = qbuf.shape[1]
    for c in range(0, w_ref.shape[1], LANES * 2):
        y = _dot(xn, w_ref[:, c:c + 2 * LANES])
        for s in range(0, 2 * LANES, LANES):
            src, col = divmod(c + s, d_model)
            ys = y[:, s:s + LANES]
            if src <= n_groups:
                ys = (ys * ra_ref[...] + pltpu.roll(ys, LANES - half, axis=1) * rb_ref[...]
                      + pltpu.roll(ys, half, axis=1) * rc_ref[...])
            if src < n_groups:
                ys = ys * QK_SCALE
            dests = [(src, 0)] if src < n_groups else [(g, src - n_groups + 1) for g in range(n_groups)]
            slab, quarter = ybuf.at[n_slab % ybuf.shape[0]], qbuf.at[n_slab % qbuf.shape[0]]
            n_slab += 1
            strides = {dilations[g] for g, _ in dests} - {1}
            if strides:
                slab[...] = ys
            if any(dil % step == 0 for dil in strides):
                firsts = [slab[pl.ds(r, tm // step, stride=step), :] for r in range(step)]
                if any(dil > step for dil in strides):
                    for r in range(step):
                        quarter[r] = firsts[r]
            for g, blk in dests:
                dil = dilations[g]
                lanes = slice(blk * d_model + col, blk * d_model + col + LANES)
                for r in range(dil):
                    if dil == 1:
                        rows = ys
                    elif dil == step:
                        rows = firsts[r]
                    elif dil % step == 0:
                        rows = quarter[r % step, pl.ds(r // step, tm // dil, stride=dil // step), :]
                    else:
                        rows = slab[pl.ds(r, tm // dil, stride=dil), :]
                    out_refs[g][r, :, lanes] = rows.astype(_BF16)


def _dil_proj(h, g, w, rope, batch, seq, dilations):
    t, d = h.shape
    tm = min(ROW_TILE, seq)
    nblk = seq // tm
    rope_spec = pl.BlockSpec((tm, LANES), lambda i: (i % nblk, 0))
    out_shapes = tuple(jax.ShapeDtypeStruct((batch, dil, seq // dil, 3 * d), _BF16) for dil in dilations)
    out_specs = tuple(pl.BlockSpec((None, dil, tm // dil, 3 * d), lambda i: (i // nblk, 0, i % nblk, 0))
                      for dil in dilations)
    return pl.pallas_call(
        functools.partial(_dil_proj_kernel, d_model=d, dilations=dilations),
        out_shape=out_shapes,
        grid=(t // tm,),
        in_specs=[_row_spec(tm, d), _const_spec((1, d)), _const_spec(w.shape), rope_spec, rope_spec, rope_spec],
        out_specs=out_specs,
        scratch_shapes=[pltpu.VMEM((4, tm, LANES), _F32), pltpu.VMEM((4, 4, tm // 4, LANES), _F32)],
        compiler_params=_params(1),
        name="dil_proj",
    )(h, g.reshape(1, d), w, *rope)


def _dil_merge_kernel(h_ref, *refs, dilations):
    n = len(dilations)
    o_refs, l_refs = refs[:n], refs[n:2 * n]
    e_ref, w_ref, out_ref, obuf, lbuf, acc_ref = refs[2 * n:]
    tm, d = h_ref.shape

    def by_position(src_ref, dil, lanes, buf):
        if dil == 1:
            return src_ref[0, :, lanes].astype(_F32)
        for r in range(dil):
            buf[pl.ds(r, tm // dil, stride=dil), :] = src_ref[r, :, lanes].astype(_F32)
        return buf[...]

    lses = [by_position(l_refs[gi], dil, slice(0, LANES), lbuf.at[gi]) for gi, dil in enumerate(dilations)]
    m = functools.reduce(jnp.maximum, lses)
    es = [jnp.exp(l - m) for l in lses]
    den = functools.reduce(lambda a, b: a + b, es)
    wides = [_dot(_packed_pieces(e / den, d // HEAD_DIM), e_ref[...]) for e in es]
    for c in range(0, d, LANES):
        lanes = slice(c, c + LANES)
        acc = jnp.zeros((tm, LANES), _F32)
        for gi, dil in enumerate(dilations):
            acc = acc + wides[gi][:, lanes] * by_position(o_refs[gi], dil, lanes, obuf.at[(c // LANES) % 2, gi])
        acc_ref[:, lanes] = acc.astype(_BF16)
    out_ref[...] = h_ref[...] + _dot(acc_ref[...], w_ref[...])


def _dil_merge(h, outs, lses, w_out, batch, seq, dilations):
    t, d = h.shape
    tm = min(ROW_TILE, seq)
    nblk = seq // tm
    n_heads = d // HEAD_DIM
    piece_row = jnp.arange(LANES)
    expand = ((piece_row % n_heads)[:, None] == (jnp.arange(d) // HEAD_DIM)[None, :]) & (piece_row < 3 * n_heads)[:, None]
    res_spec = lambda dil, n: pl.BlockSpec((None, dil, tm // dil, n), lambda i: (i // nblk, 0, i % nblk, 0))
    n = len(dilations)
    return pl.pallas_call(
        functools.partial(_dil_merge_kernel, dilations=dilations),
        out_shape=jax.ShapeDtypeStruct((t, d), _F32),
        grid=(t // tm,),
        in_specs=([_row_spec(tm, d)] + [res_spec(dil, d) for dil in dilations] + [res_spec(dil, LANES) for dil in dilations]
                  + [_const_spec((LANES, d)), _const_spec(w_out.shape)]),
        out_specs=_row_spec(tm, d),
        scratch_shapes=[pltpu.VMEM((2, n, tm, LANES), _F32), pltpu.VMEM((n, tm, LANES), _F32),
                        pltpu.VMEM((tm, d), _BF16)],
        compiler_params=_params(1),
        name="dil_merge",
    )(h, *outs, *lses, expand.astype(_BF16), w_out)


def _dil_mixer(h, g, w_in, w_out, rope, batch, seq):
    t, d = h.shape
    dilations = tuple(dil for _, dil in DIL_CONFIGS)
    projs = _dil_proj(h, g, w_in.astype(_BF16), rope, batch, seq, dilations)
    outs, lses = [], []
    for (window, dil), proj in zip(DIL_CONFIGS, projs):
        assert seq % (dil * DIL_BAND) == 0
        proj = proj.reshape(t, 3 * d)
        o, lse = _dil_attn(proj, 0, proj, 1, proj, 2, batch * dil, seq // dil, d, window // dil)
        outs.append(o.reshape(batch, dil, seq // dil, d))
        lses.append(lse.reshape(batch, dil, seq // dil, LANES))
    return _dil_merge(h, outs, lses, w_out.astype(_BF16), batch, seq, dilations)


def _gelu(x):
    return 0.5 * x * (1.0 + lax.erf(x * math.sqrt(0.5)))


def _sgu_kernel(h_ref, g_ref, win_ref, nv_ref, ws_ref, bs_ref, wout_ref, o_ref, u_ref, v_ref, y_ref, *, chunk):
    x = h_ref[...]
    tm = x.shape[0]
    half = wout_ref.shape[0]
    gdim = half // SGU_GROUPS
    xn = _rms_norm(x, g_ref[...], NORM_EPS).astype(_BF16)
    ssq = jnp.zeros((tm, 1), _F32)
    for c in range(0, half, chunk):
        u_ref[:, c:c + chunk] = _gelu(_dot(xn, win_ref[:, c:c + chunk]))
        vv = _gelu(_dot(xn, win_ref[:, half + c:half + c + chunk]))
        v_ref[:, c:c + chunk] = vv
        ssq = ssq + jnp.sum(vv * vv, axis=1, keepdims=True)
    rstd = lax.rsqrt(ssq / half + NORM_EPS)
    row = lax.broadcasted_iota(jnp.int32, (SGU_CHUNK, SGU_CHUNK), 0)
    col = lax.broadcasted_iota(jnp.int32, (SGU_CHUNK, SGU_CHUNK), 1)
    for gi in range(SGU_GROUPS):
        cols = slice(gi * gdim, (gi + 1) * gdim)
        w = jnp.where(col <= row, ws_ref[gi], jnp.zeros((), _BF16))
        bias = bs_ref[:, gi:gi + 1]
        for r in range(0, tm, SGU_CHUNK):
            rows = slice(r, r + SGU_CHUNK)
            vn = (v_ref[rows, cols] * rstd[rows] * nv_ref[:, cols]).astype(_BF16)
            y_ref[rows, cols] = (u_ref[rows, cols] * (_dot(w, vn) + bias)).astype(_BF16)
    o_ref[...] = x + _dot(y_ref[...], wout_ref[...])


def _sgu_mixer(h, g, w_in, norm_v, w_s, b_s, w_out):
    t, d = h.shape
    half = w_out.shape[0]
    tm = min(ROW_TILE, t)
    assert tm % SGU_CHUNK == 0 and (half // SGU_GROUPS) % LANES == 0
    return pl.pallas_call(
        functools.partial(_sgu_kernel, chunk=_col_chunk(half)),
        out_shape=jax.ShapeDtypeStruct((t, d), _F32),
        grid=(t // tm,),
        in_specs=[_row_spec(tm, d), _const_spec((1, d)), _const_spec(w_in.shape), _const_spec((1, half)),
                  _const_spec(w_s.shape), _const_spec((SGU_CHUNK, SGU_GROUPS)), _const_spec(w_out.shape)],
        out_specs=_row_spec(tm, d),
        scratch_shapes=[pltpu.VMEM((tm, half), _F32), pltpu.VMEM((tm, half), _F32), pltpu.VMEM((tm, half), _BF16)],
        compiler_params=_params(1),
        name="sgu",
    )(h, g.reshape(1, d), w_in.astype(_BF16), norm_v.reshape(1, half), w_s.astype(_BF16), jnp.transpose(b_s),
      w_out.astype(_BF16))


def kernel(x, p, norm_ffn1, w_ffn1_in, w_ffn1_out, norm_mix, norm_ffn2, w_ffn2_in, w_ffn2_out, norm_ple, w_ple_gate, b_ple_gate, w_ple_proj, fox_w_in, fox_b_f, fox_w_out, dil_w_in, dil_w_out, diff_w_in, diff_lambda, diff_subln, diff_w_out, sgu_w_in, sgu_norm_v, sgu_w_s, sgu_b_s, sgu_w_out, norm_final):
    batch, seq, d = x.shape
    depth = p.shape[0]
    t = batch * seq
    h = x.reshape(t, d)
    rope = _rope_tables(seq)
    n_mixers = 4
    ffn1_in, ffn1_out = w_ffn1_in.astype(_BF16), w_ffn1_out.astype(_BF16)
    ffn2_in, ffn2_out = w_ffn2_in.astype(_BF16), w_ffn2_out.astype(_BF16)
    ple_gate, ple_proj = w_ple_gate.astype(_BF16), w_ple_proj.astype(_BF16)
    for i in range(depth):
        kind, j = i % n_mixers, i // n_mixers
        h = _ffn(h, norm_ffn1[i], ffn1_in, ffn1_out, i)
        attn = None
        if kind == 0:
            attn = _fox_mixer(h, norm_mix[i], fox_w_in[j], fox_b_f[j], fox_w_out[j], batch, seq)
        elif kind == 1:
            h = _dil_mixer(h, norm_mix[i], dil_w_in[j], dil_w_out[j], rope, batch, seq)
        elif kind == 2:
            attn = _diff_mixer(h, norm_mix[i], diff_w_in[j], diff_lambda[j], diff_subln[j], diff_w_out[j], rope,
                               batch, seq, i)
        else:
            h = _sgu_mixer(h, norm_mix[i], sgu_w_in[j], sgu_norm_v[j], sgu_w_s[j], sgu_b_s[j], sgu_w_out[j])
        h = _post_mixer(h, attn, norm_ffn2[i], ffn2_in, ffn2_out, p.reshape(depth, t, p.shape[-1]), i, norm_ple[i],
                        ple_gate, b_ple_gate[i], ple_proj, norm_final, final_norm=(i == depth - 1))
    return h.reshape(batch, seq, d)
```

```python
import functools
import math

import jax
import jax.numpy as jnp
import numpy as np
from jax import lax
from jax.experimental import pallas as pl
from jax.experimental.pallas import tpu as pltpu

_F32 = jnp.float32
_BF16 = jnp.bfloat16

HEAD_DIM = 64
ROPE_THETA = 500000.0
ROT_DIM = HEAD_DIM // 4
NORM_EPS = 1e-6
DIL_CONFIGS = ((128, 1), (512, 4), (2048, 16))
DIL_BAND = 128
DIFF_SUBLN_EPS = 1e-5
SGU_GROUPS = 8
SGU_CHUNK = 128

LANES = 128
VMEM_LIMIT_BYTES = 56 * 1024 * 1024
ROW_TILE = 512
ATTN_BLOCK = 512
COL_CHUNK = 256
LOG2E = math.log2(math.e)
QK_SCALE = HEAD_DIM ** -0.5 * LOG2E


def _params(n_axes):
    return pltpu.CompilerParams(dimension_semantics=("arbitrary",) * n_axes,
                                vmem_limit_bytes=VMEM_LIMIT_BYTES)


def _col_chunk(n, target=COL_CHUNK):
    best = LANES
    for c in range(LANES, target + 1, LANES):
        if n % c == 0:
            best = c
    assert n % best == 0, (n, best)
    return best


def _row_spec(tm, n):
    return pl.BlockSpec((tm, n), lambda i: (i, 0))


def _const_spec(shape):
    return pl.BlockSpec(shape, lambda i: (0,) * len(shape))


def _rms_norm(x, g, eps):
    return x * lax.rsqrt(jnp.mean(x * x, axis=-1, keepdims=True) + eps) * g


def _dot(a, b):
    return jnp.dot(a, b, preferred_element_type=_F32)


def _dot_nt(a, b):
    return lax.dot_general(a, b, (((1,), (1,)), ((), ())), preferred_element_type=_F32)


def _split3(x):
    hi = x.astype(_BF16)
    r1 = x - hi.astype(_F32)
    mid = r1.astype(_BF16)
    lo = (r1 - mid.astype(_F32)).astype(_BF16)
    return hi, mid, lo


def _packed_pieces(x, n_heads):
    assert 3 * n_heads <= LANES
    lane = lax.broadcasted_iota(jnp.int32, x.shape, 1)
    packed = jnp.zeros_like(x)
    for n, piece in enumerate(_split3(x)):
        piece = jnp.where(lane < n_heads, piece.astype(_F32), 0.0)
        packed = packed + (pltpu.roll(piece, n * n_heads, axis=1) if n else piece)
    return packed.astype(_BF16)


def _ffn_kernel(h_ref, g_ref, win_ref, wout_ref, o_ref, hid_ref, *, d_ff, chunk):
    x = h_ref[...]
    xn = _rms_norm(x, g_ref[...], NORM_EPS).astype(_BF16)
    for c in range(0, d_ff, chunk):
        gate = _dot(xn, win_ref[:, c:c + chunk])
        up = _dot(xn, win_ref[:, d_ff + c:d_ff + c + chunk])
        hid_ref[:, c:c + chunk] = (gate * jax.nn.sigmoid(gate) * up).astype(_BF16)
    o_ref[...] = x + 0.5 * _dot(hid_ref[...], wout_ref[...])


def _layer_spec(stacked, layer):
    zeros = (0,) * (stacked.ndim - 1)
    return pl.BlockSpec((None,) + stacked.shape[1:], lambda i: (layer,) + zeros, pipeline_mode=pl.Buffered(1))


def _ffn(h, g, w_in, w_out, layer):
    t, d = h.shape
    d_ff = w_out.shape[1]
    tm = min(ROW_TILE, t)
    return pl.pallas_call(
        functools.partial(_ffn_kernel, d_ff=d_ff, chunk=_col_chunk(d_ff)),
        out_shape=jax.ShapeDtypeStruct((t, d), _F32),
        grid=(t // tm,),
        in_specs=[_row_spec(tm, d), _const_spec((1, d)), _layer_spec(w_in, layer), _layer_spec(w_out, layer)],
        out_specs=_row_spec(tm, d),
        scratch_shapes=[pltpu.VMEM((tm, d_ff), _BF16)],
        compiler_params=_params(1),
        name="ffn",
    )(h, g.reshape(1, d), w_in, w_out)


def _post_kernel(*refs, has_attn, final_norm, d_ff, chunk):
    h_ref = refs[0]
    o_ref, wo_ref = refs[1:3] if has_attn else (None, None)
    (gf2_ref, win_ref, wout_ref, p_ref, gp_ref, wg_ref, bg_ref, wp_ref, gfin_ref,
     out_ref, hid_ref) = refs[3 if has_attn else 1:]
    x = h_ref[...]
    if has_attn:
        x = x + _dot(o_ref[...], wo_ref[...])
    xn = _rms_norm(x, gf2_ref[...], NORM_EPS).astype(_BF16)
    for c in range(0, d_ff, chunk):
        gate = _dot(xn, win_ref[:, c:c + chunk])
        up = _dot(xn, win_ref[:, d_ff + c:d_ff + c + chunk])
        hid_ref[:, c:c + chunk] = (gate * jax.nn.sigmoid(gate) * up).astype(_BF16)
    x = x + 0.5 * _dot(hid_ref[...], wout_ref[...])
    xn = _rms_norm(x, gp_ref[...], NORM_EPS).astype(_BF16)
    gate = jax.nn.sigmoid(_dot(xn, wg_ref[...]) + bg_ref[...])
    x = x + gate * _dot(p_ref[...].astype(_BF16), wp_ref[...])
    if final_norm:
        x = _rms_norm(x, gfin_ref[...], NORM_EPS)
    out_ref[...] = x


def _post_mixer(h, attn, g_ffn, w_in, w_out, p, layer, g_ple, w_gate, b_gate, w_proj, g_final, final_norm):
    t, d = h.shape
    d_ff = w_out.shape[1]
    dp = p.shape[2]
    tm = min(ROW_TILE, t)
    args, specs = [h], [_row_spec(tm, d)]
    if attn is not None:
        args += list(attn)
        specs += [_row_spec(tm, attn[0].shape[1]), _const_spec(attn[1].shape)]
    args += [g_ffn.reshape(1, d), w_in, w_out, p, g_ple.reshape(1, d), w_gate, b_gate.reshape(1, d), w_proj,
             g_final.reshape(1, d)]
    specs += [_const_spec((1, d)), _layer_spec(w_in, layer), _layer_spec(w_out, layer),
              pl.BlockSpec((None, tm, dp), lambda i: (layer, i, 0)),
              _const_spec((1, d)), _layer_spec(w_gate, layer), _const_spec((1, d)), _layer_spec(w_proj, layer),
              _const_spec((1, d))]
    return pl.pallas_call(
        functools.partial(_post_kernel, has_attn=attn is not None, final_norm=final_norm, d_ff=d_ff,
                          chunk=_col_chunk(d_ff)),
        out_shape=jax.ShapeDtypeStruct((t, d), _F32),
        grid=(t // tm,),
        in_specs=specs,
        out_specs=_row_spec(tm, d),
        scratch_shapes=[pltpu.VMEM((tm, d_ff), _BF16)],
        compiler_params=_params(1),
        name="post_mixer",
    )(*args)


def _rope_tables(s):
    half = ROT_DIM // 2
    inv_freq = ROPE_THETA ** (-jnp.arange(0, ROT_DIM, 2, dtype=_F32) / ROT_DIM)
    ang = jnp.arange(s, dtype=jnp.int32).astype(_F32)[:, None] * inv_freq[None, :]
    cos, sin = jnp.cos(ang), jnp.sin(ang)
    m = jnp.arange(LANES) % HEAD_DIM
    idx = m % half
    first, second = m < half, (m >= half) & (m < ROT_DIM)
    a = jnp.where((first | second)[None, :], cos[:, idx], 1.0)
    b = jnp.where(first[None, :], -sin[:, idx], 0.0)
    c = jnp.where(second[None, :], sin[:, idx], 0.0)
    return a.astype(_F32), b.astype(_F32), c.astype(_F32)


def _proj_kernel(h_ref, g_ref, w_ref, ra_ref, rb_ref, rc_ref, o_ref, *, n_query, n_rope, chunk):
    xn = _rms_norm(h_ref[...], g_ref[...], NORM_EPS).astype(_BF16)
    half = ROT_DIM // 2
    for c in range(0, w_ref.shape[1], chunk):
        y = _dot(xn, w_ref[:, c:c + chunk])
        if c < n_rope:
            for s in range(0, chunk, LANES):
                ys = y[:, s:s + LANES]
                ys = (ys * ra_ref[...] + pltpu.roll(ys, LANES - half, axis=1) * rb_ref[...]
                      + pltpu.roll(ys, half, axis=1) * rc_ref[...])
                if c < n_query:
                    ys = ys * QK_SCALE
                o_ref[:, c + s:c + s + LANES] = ys.astype(o_ref.dtype)
        else:
            o_ref[:, c:c + chunk] = y.astype(o_ref.dtype)


def _proj(h, g, w, rope, n_query, n_rope, seq):
    t, d = h.shape
    n = w.shape[1]
    tm = min(ROW_TILE, seq)
    chunk = _col_chunk(n)
    assert n_rope % chunk == 0 and n_query % chunk == 0 and n_query <= n_rope
    nblk = seq // tm
    rope_spec = pl.BlockSpec((tm, LANES), lambda i: (i % nblk, 0))
    return pl.pallas_call(
        functools.partial(_proj_kernel, n_query=n_query, n_rope=n_rope, chunk=chunk),
        out_shape=jax.ShapeDtypeStruct((t, n), _BF16),
        grid=(t // tm,),
        in_specs=[_row_spec(tm, d), _const_spec((1, d)), _const_spec(w.shape), rope_spec, rope_spec, rope_spec],
        out_specs=_row_spec(tm, n),
        compiler_params=_params(1),
        name="proj",
    )(h, g.reshape(1, d), w, *rope)


def _fox_proj_kernel(h_ref, g_ref, w_ref, wf_ref, bf_ref, qkv_ref, lf_ref, *, n_query, chunk):
    xn = _rms_norm(h_ref[...], g_ref[...], NORM_EPS).astype(_BF16)
    for c in range(0, w_ref.shape[1], chunk):
        y = _dot(xn, w_ref[:, c:c + chunk])
        if c < n_query:
            y = y * QK_SCALE
        qkv_ref[:, c:c + chunk] = y.astype(qkv_ref.dtype)
    z = _dot(xn, wf_ref[...]) + bf_ref[...]
    lf_ref[...] = jnp.minimum(z, 0.0) - jnp.log1p(jnp.exp(-jnp.abs(z)))


def _fox_proj(h, g, w_qkv, w_f, b_f):
    t, d = h.shape
    n = w_qkv.shape[1]
    tm = min(ROW_TILE, t)
    return pl.pallas_call(
        functools.partial(_fox_proj_kernel, n_query=d, chunk=_col_chunk(n)),
        out_shape=(jax.ShapeDtypeStruct((t, n), _BF16), jax.ShapeDtypeStruct((t, LANES), _F32)),
        grid=(t // tm,),
        in_specs=[_row_spec(tm, d), _const_spec((1, d)), _const_spec(w_qkv.shape), _const_spec(w_f.shape),
                  _const_spec((1, LANES))],
        out_specs=(_row_spec(tm, n), _row_spec(tm, LANES)),
        compiler_params=_params(1),
        name="fox_proj",
    )(h, g.reshape(1, d), w_qkv, w_f, b_f)


def _bias_placement(d):
    pairs = d // LANES
    n_heads = 2 * pairs
    w = np.zeros((3, LANES, d), np.float32)
    ones = np.zeros((3, 1, d), np.float32)
    for hp in range(pairs):
        base = hp * LANES
        for p in range(3):
            w[0, p * n_heads + 2 * hp, base + p] = 1.0
            w[1, p * n_heads + 2 * hp + 1, base + 6 + p] = 1.0
            w[2, p * n_heads + 2 * hp, base + 3 + p] = -1.0
            w[2, p * n_heads + 2 * hp + 1, base + 9 + p] = -1.0
            ones[0, 0, base + 3 + p] = 1.0
            ones[1, 0, base + 9 + p] = 1.0
            ones[2, 0, base + p] = 1.0
            ones[2, 0, base + 6 + p] = 1.0
    return jnp.asarray(w, _BF16), jnp.asarray(ones, _F32)


def _cumsum_kernel(x_ref, w_ref, ones_ref, qea_ref, qeb_ref, ke_ref, carry_ref, *, n_heads):
    @pl.when(pl.program_id(1) == 0)
    def _():
        carry_ref[...] = jnp.zeros_like(carry_ref)

    x = x_ref[...]
    ts = x.shape[0]
    row = lax.broadcasted_iota(jnp.int32, (ts, ts), 0)
    col = lax.broadcasted_iota(jnp.int32, (ts, ts), 1)
    tri = jnp.where(col <= row, 1.0, 0.0).astype(_BF16)
    hi, mid, lo = _split3(x)
    c = _dot(tri, hi) + _dot(tri, mid) + _dot(tri, lo) + carry_ref[0:1, :]
    carry_ref[0:1, :] = c[ts - 1:ts, :]
    pieces = _packed_pieces(c * LOG2E, n_heads)
    for n, out_ref in enumerate((qea_ref, qeb_ref, ke_ref)):
        out_ref[...] = (_dot(pieces, w_ref[n]) + ones_ref[n]).astype(out_ref.dtype)


def _cumsum_bias(x, batch, seq, d, ts):
    ns = seq // ts
    w, ones = _bias_placement(d)
    row_spec = pl.BlockSpec((ts, d), lambda b, s: (b * ns + s, 0))
    out = jax.ShapeDtypeStruct((batch * seq, d), _BF16)
    return pl.pallas_call(
        functools.partial(_cumsum_kernel, n_heads=d // HEAD_DIM),
        out_shape=(out, out, out),
        grid=(batch, ns),
        in_specs=[pl.BlockSpec((ts, LANES), lambda b, s: (b * ns + s, 0)),
                  pl.BlockSpec(w.shape, lambda b, s: (0, 0, 0)), pl.BlockSpec(ones.shape, lambda b, s: (0, 0, 0))],
        out_specs=(row_spec, row_spec, row_spec),
        scratch_shapes=[pltpu.VMEM((8, LANES), _F32)],
        compiler_params=_params(2),
        name="cumsum",
    )(x, w, ones)


def _flash_items(nq, blk):
    items = []
    for i in range(nq):
        for j in range(i // 2):
            items.append((i, 2 * j * blk, 2 * blk, False))
        if i % 2 == 1:
            items.append((i, (i - 1) * blk, 2 * blk, True))
        else:
            items.append((i, i * blk, blk, True))
    return items


def _causal_flash(query_maps, k_refs, value_maps, finish, nq, blk, scratch):
    s_bufs, m_ref, acc_ref = scratch[:-2], scratch[-2], scratch[-1]
    ahead = len(s_bufs) - 1
    items = _flash_items(nq, blk)

    def produce(item, s_ref):
        i, start, width, _ = item
        parts = [r[start:start + width, :] for r in k_refs]
        kb = parts[0] if len(parts) == 1 else jnp.concatenate(parts, axis=1)
        for e, qm in enumerate(query_maps(i)):
            s_ref[e, :, :width] = _dot_nt(qm, kb)

    def consume(item, s_ref, first):
        i, start, width, masked = item
        vbs = value_maps(start, width)
        for e in range(2):
            s = s_ref[e, :, :width]
            if masked:
                row = lax.broadcasted_iota(jnp.int32, (blk, width), 0)
                col = lax.broadcasted_iota(jnp.int32, (blk, width), 1)
                s = jnp.where(col - row <= i * blk - start, s, -jnp.inf)
            m_new = jnp.max(s, axis=1, keepdims=True)
            if not first:
                m_old = m_ref[e]
                m_new = jnp.maximum(m_old, m_new)
            pv = _dot(jnp.exp2((s - m_new).astype(_BF16)), vbs[e])
            acc_ref[e] = pv if first else acc_ref[e] * jnp.exp2(m_old - m_new) + pv
            m_ref[e] = m_new

    for t in range(ahead):
        produce(items[t], s_bufs[t])
    for t, item in enumerate(items):
        if t + ahead < len(items):
            produce(items[t + ahead], s_bufs[(t + ahead) % len(s_bufs)])
        consume(item, s_bufs[t % len(s_bufs)], first=(t == 0 or items[t - 1][0] != item[0]))
        if t + 1 == len(items) or items[t + 1][0] != item[0]:
            finish(item[0], acc_ref[0], acc_ref[1])


FLASH_SCORE_BUFFERS = 3


def _flash_scratch(blk, n_out):
    return ([pltpu.VMEM((2, blk, 2 * blk), _F32)] * FLASH_SCORE_BUFFERS
            + [pltpu.VMEM((2, blk, 1), _F32), pltpu.VMEM((2, blk, n_out), _F32)])


def _split_maps(q, lo):
    zero = jnp.zeros_like(q)
    return jnp.where(lo, q, zero), jnp.where(lo, zero, q)


def _fox_attn_kernel(q_ref, qea_ref, qeb_ref, k_ref, ke_ref, v_ref, o_ref, *scratch, blk):
    lo = lax.broadcasted_iota(jnp.int32, (blk, LANES), 1) < HEAD_DIM

    def query_maps(i):
        rows = slice(i * blk, (i + 1) * blk)
        qa, qb = _split_maps(q_ref[rows, :], lo)
        return jnp.concatenate([qa, qea_ref[rows, :]], axis=1), jnp.concatenate([qb, qeb_ref[rows, :]], axis=1)

    def value_maps(start, width):
        vb = v_ref[start:start + width, :]
        keep = lax.broadcasted_iota(jnp.int32, (width, LANES), 1) < HEAD_DIM
        one = jnp.ones_like(vb)
        return jnp.where(keep, vb, one), jnp.where(keep, one, vb)

    def finish(i, acc_a, acc_b):
        out_a = acc_a / pltpu.roll(acc_a, HEAD_DIM, axis=1)
        out_b = acc_b / pltpu.roll(acc_b, HEAD_DIM, axis=1)
        o_ref[i * blk:(i + 1) * blk, :] = jnp.where(lo, out_a, out_b).astype(o_ref.dtype)

    _causal_flash(query_maps, (k_ref, ke_ref), value_maps, finish, q_ref.shape[0] // blk, blk, scratch)


def _fox_attn(qkv, qea, qeb, ke, batch, seq, d):
    blk = min(ATTN_BLOCK, seq)
    pairs = d // LANES
    spec = lambda col0: pl.BlockSpec((seq, LANES), lambda b, h: (b, col0 + h))
    return pl.pallas_call(
        functools.partial(_fox_attn_kernel, blk=blk),
        out_shape=jax.ShapeDtypeStruct((batch * seq, d), _BF16),
        grid=(batch, pairs),
        in_specs=[spec(0), spec(0), spec(0), spec(pairs), spec(0), spec(2 * pairs)],
        out_specs=spec(0),
        scratch_shapes=_flash_scratch(blk, LANES),
        compiler_params=_params(2),
        name="fox_attn",
    )(qkv, qea, qeb, qkv, ke, qkv)


def _fox_mixer(h, g, w_in, b_f, w_out, batch, seq):
    t, d = h.shape
    n_heads = d // HEAD_DIM
    w_qkv = w_in[:, :3 * d].astype(_BF16)
    w_f = jnp.pad(w_in[:, 3 * d:], ((0, 0), (0, LANES - n_heads))).astype(_BF16)
    b_pad = jnp.pad(b_f, (0, LANES - n_heads)).reshape(1, LANES)
    qkv, log_f = _fox_proj(h, g, w_qkv, w_f, b_pad)
    qea, qeb, ke = _cumsum_bias(log_f, batch, seq, d, min(ATTN_BLOCK, seq))
    return _fox_attn(qkv, qea, qeb, ke, batch, seq, d), w_out.astype(_BF16)


def _diff_attn_kernel(q_ref, k_ref, v_ref, lam_ref, g_ref, o_ref, *scratch, blk, lam_init):
    lo = lax.broadcasted_iota(jnp.int32, (blk, LANES), 1) < HEAD_DIM
    lp = lam_ref[...]
    lam = (jnp.exp(jnp.sum(lp[0:1] * lp[1:2], axis=1, keepdims=True))
           - jnp.exp(jnp.sum(lp[2:3] * lp[3:4], axis=1, keepdims=True)) + lam_init)

    def query_maps(i):
        return _split_maps(q_ref[i * blk:(i + 1) * blk, :], lo)

    def value_maps(start, width):
        vb = v_ref[start:start + width, :]
        vb = jnp.concatenate([vb, jnp.ones_like(vb)], axis=1)
        return vb, vb

    def finish(i, acc_1, acc_2):
        o = acc_1[:, :LANES] / acc_1[:, LANES:] - lam * (acc_2[:, :LANES] / acc_2[:, LANES:])
        o = _rms_norm(o, g_ref[...], DIFF_SUBLN_EPS) * (1.0 - lam_init)
        o_ref[i * blk:(i + 1) * blk, :] = o.astype(o_ref.dtype)

    _causal_flash(query_maps, (k_ref,), value_maps, finish, q_ref.shape[0] // blk, blk, scratch)


def _diff_attn(proj, lam_params, subln_g, batch, seq, d, lam_init):
    blk = min(ATTN_BLOCK, seq)
    heads = d // LANES
    spec = lambda col0: pl.BlockSpec((seq, LANES), lambda b, h: (b, col0 + h))
    return pl.pallas_call(
        functools.partial(_diff_attn_kernel, blk=blk, lam_init=lam_init),
        out_shape=jax.ShapeDtypeStruct((batch * seq, d), _BF16),
        grid=(batch, heads),
        in_specs=[spec(0), spec(heads), spec(2 * heads),
                  pl.BlockSpec(lam_params.shape, lambda b, h: (0, 0)), pl.BlockSpec((1, LANES), lambda b, h: (0, 0))],
        out_specs=spec(0),
        scratch_shapes=_flash_scratch(blk, 2 * LANES),
        compiler_params=_params(2),
        name="diff_attn",
    )(proj, proj, proj, lam_params, subln_g.reshape(1, LANES))


def _diff_mixer(h, g, w_in, lam_params, subln_g, w_out, rope, batch, seq, layer_idx):
    t, d = h.shape
    lam_init = 0.8 - 0.6 * math.exp(-0.3 * layer_idx)
    proj = _proj(h, g, w_in.astype(_BF16), rope, d, 2 * d, seq)
    return _diff_attn(proj, lam_params, subln_g, batch, seq, d, lam_init), w_out.astype(_BF16)


def _dil_attn_kernel(q_ref, kp_ref, kc_ref, vp_ref, vc_ref, o_ref, lse_ref, kcat_ref, vcat_ref, s_ref, p_ref, *,
                     n_steps, n_pairs):
    band = DIL_BAND
    n_heads = 2 * n_pairs
    kcat_ref[0:band, :] = kp_ref[...]
    kcat_ref[band:, :] = kc_ref[...]
    vcat_ref[0:band, :] = vp_ref[...]
    vcat_ref[band:, :] = vc_ref[...]
    row = lax.broadcasted_iota(jnp.int32, (band, 2 * band), 0)
    col = lax.broadcasted_iota(jnp.int32, (band, 2 * band), 1)
    in_band = jnp.logical_and(col <= row + band, col >= row + band - n_steps)
    first_valid = jnp.logical_and(in_band, jnp.logical_or(col >= band, pl.program_id(1) > 0))
    lane = lax.broadcasted_iota(jnp.int32, (band, LANES), 1)
    lo = lane < HEAD_DIM
    lo_v = lax.broadcasted_iota(jnp.int32, (2 * band, LANES), 1) < HEAD_DIM
    for r in range(q_ref.shape[0] // band):
        rows = slice(r * band, (r + 1) * band)
        win = slice(r * band, (r + 2) * band)
        valid = first_valid if r == 0 else in_band
        for hp in range(n_pairs):
            sl = slice(hp * LANES, (hp + 1) * LANES)
            q = q_ref[rows, sl]
            zero = jnp.zeros_like(q)
            kw = kcat_ref[win, sl]
            for e, qm in enumerate((jnp.where(lo, q, zero), jnp.where(lo, zero, q))):
                s_ref[(2 * hp + e) * band:(2 * hp + e + 1) * band, :] = jnp.where(valid, _dot_nt(qm, kw), -jnp.inf)
        s = s_ref[...]
        m = jnp.max(s, axis=1, keepdims=True)
        p_ref[...] = jnp.exp2((s - m).astype(_BF16))
        m_tile = jnp.zeros((band, LANES), _F32)
        l_tile = jnp.ones((band, LANES), _F32)
        for hp in range(n_pairs):
            sl = slice(hp * LANES, (hp + 1) * LANES)
            vw = vcat_ref[win, sl]
            one = jnp.ones_like(vw)
            ha, hb = slice((2 * hp) * band, (2 * hp + 1) * band), slice((2 * hp + 1) * band, (2 * hp + 2) * band)
            pv_a = _dot(p_ref[ha, :], jnp.where(lo_v, vw, one))
            pv_b = _dot(p_ref[hb, :], jnp.where(lo_v, one, vw))
            l_a, l_b = pltpu.roll(pv_a, HEAD_DIM, axis=1), pltpu.roll(pv_b, HEAD_DIM, axis=1)
            o_ref[rows, sl] = (jnp.where(lo, pv_a, pv_b) / jnp.where(lo, l_a, l_b)).astype(o_ref.dtype)
            m_tile = jnp.where(lane == 2 * hp, m[ha], jnp.where(lane == 2 * hp + 1, m[hb], m_tile))
            l_tile = jnp.where(lane == 2 * hp, l_a, jnp.where(lane == 2 * hp + 1, pv_b, l_tile))
        lse_ref[rows, :] = (m_tile + jnp.log2(l_tile)) * (1.0 / LOG2E)


def _dil_attn(q, q_col, k, k_col, v, v_col, n_seq, n_sub, d, n_steps):
    assert n_steps <= DIL_BAND
    rb = min(ROW_TILE, n_sub)
    nb = n_sub // rb
    bands = rb // DIL_BAND
    n_heads = d // HEAD_DIM
    cur = lambda col: (lambda x, n: (x * nb + n, col))
    prev = lambda col: (lambda x, n: ((x * nb + n) * bands - jnp.where(n > 0, 1, 0), col))
    return pl.pallas_call(
        functools.partial(_dil_attn_kernel, n_steps=n_steps, n_pairs=d // LANES),
        out_shape=(jax.ShapeDtypeStruct((n_seq * n_sub, d), _BF16),
                   jax.ShapeDtypeStruct((n_seq * n_sub, LANES), _F32)),
        grid=(n_seq, nb),
        in_specs=[pl.BlockSpec((rb, d), cur(q_col)),
                  pl.BlockSpec((DIL_BAND, d), prev(k_col)), pl.BlockSpec((rb, d), cur(k_col)),
                  pl.BlockSpec((DIL_BAND, d), prev(v_col)), pl.BlockSpec((rb, d), cur(v_col))],
        out_specs=(pl.BlockSpec((rb, d), cur(0)), pl.BlockSpec((rb, LANES), cur(0))),
        scratch_shapes=[pltpu.VMEM((rb + DIL_BAND, d), _BF16), pltpu.VMEM((rb + DIL_BAND, d), _BF16),
                        pltpu.VMEM((n_heads * DIL_BAND, 2 * DIL_BAND), _F32),
                        pltpu.VMEM((n_heads * DIL_BAND, 2 * DIL_BAND), _BF16)],
        compiler_params=_params(2),
        name="dil_attn",
    )(q, k, k, v, v)


def _dil_proj_kernel(h_ref, g_ref, w_ref, ra_ref, rb_ref, rc_ref, *refs, d_model, dilations):
    out_refs, ybuf, qbuf = refs[:len(dilations)], refs[len(dilations)], refs[len(dilations) + 1]
    n_groups = len(dilations)
    tm = h_ref.shape[0]
    xn = _rms_norm(h_ref[...], g_ref[...], NORM_EPS).astype(_BF16)
    half = ROT_DIM // 2
    n_slab = 0
    step = qbuf.shape[1]
    for c in range(0, w_ref.shape[1], LANES * 2):
        y = _dot(xn, w_ref[:, c:c + 2 * LANES])
        for s in range(0, 2 * LANES, LANES):
            src, col = divmod(c + s, d_model)
            ys = y[:, s:s + LANES]
            if src <= n_groups:
                ys = (ys * ra_ref[...] + pltpu.roll(ys, LANES - half, axis=1) * rb_ref[...]
                      + pltpu.roll(ys, half, axis=1) * rc_ref[...])
            if src < n_groups:
                ys = ys * QK_SCALE
            dests = [(src, 0)] if src < n_groups else [(g, src - n_groups + 1) for g in range(n_groups)]
            slab, quarter = ybuf.at[n_slab % ybuf.shape[0]], qbuf.at[n_slab % qbuf.shape[0]]
            n_slab += 1
            strides = {dilations[g] for g, _ in dests} - {1}
            if strides:
                slab[...] = ys
            if any(dil % step == 0 for dil in strides):
                firsts = [slab[pl.ds(r, tm // step, stride=step), :] for r in range(step)]
                if any(dil > step for dil in strides):
                    for r in range(step):
                        quarter[r] = firsts[r]
            for g, blk in dests:
                dil = dilations[g]
                lanes = slice(blk * d_model + col, blk * d_model + col + LANES)
                for r in range(dil):
                    if dil == 1:
                        rows = ys
                    elif dil == step:
                        rows = firsts[r]
                    elif dil % step == 0:
                        rows = quarter[r % step, pl.ds(r // step, tm // dil, stride=dil // step), :]
                    else:
                        rows = slab[pl.ds(r, tm // dil, stride=dil), :]
                    out_refs[g][r, :, lanes] = rows.astype(_BF16)


def _dil_proj(h, g, w, rope, batch, seq, dilations):
    t, d = h.shape
    tm = min(ROW_TILE, seq)
    nblk = seq // tm
    rope_spec = pl.BlockSpec((tm, LANES), lambda i: (i % nblk, 0))
    out_shapes = tuple(jax.ShapeDtypeStruct((batch, dil, seq // dil, 3 * d), _BF16) for dil in dilations)
    out_specs = tuple(pl.BlockSpec((None, dil, tm // dil, 3 * d), lambda i: (i // nblk, 0, i % nblk, 0))
                      for dil in dilations)
    return pl.pallas_call(
        functools.partial(_dil_proj_kernel, d_model=d, dilations=dilations),
        out_shape=out_shapes,
        grid=(t // tm,),
        in_specs=[_row_spec(tm, d), _const_spec((1, d)), _const_spec(w.shape), rope_spec, rope_spec, rope_spec],
        out_specs=out_specs,
        scratch_shapes=[pltpu.VMEM((4, tm, LANES), _F32), pltpu.VMEM((4, 4, tm // 4, LANES), _F32)],
        compiler_params=_params(1),
        name="dil_proj",
    )(h, g.reshape(1, d), w, *rope)


def _dil_merge_kernel(h_ref, *refs, dilations):
    n = len(dilations)
    o_refs, l_refs = refs[:n], refs[n:2 * n]
    e_ref, w_ref, out_ref, obuf, lbuf, acc_ref = refs[2 * n:]
    tm, d = h_ref.shape

    def by_position(src_ref, dil, lanes, buf):
        if dil == 1:
            return src_ref[0, :, lanes].astype(_F32)
        for r in range(dil):
            buf[pl.ds(r, tm // dil, stride=dil), :] = src_ref[r, :, lanes].astype(_F32)
        return buf[...]

    lses = [by_position(l_refs[gi], dil, slice(0, LANES), lbuf.at[gi]) for gi, dil in enumerate(dilations)]
    m = functools.reduce(jnp.maximum, lses)
    es = [jnp.exp(l - m) for l in lses]
    den = functools.reduce(lambda a, b: a + b, es)
    wides = [_dot(_packed_pieces(e / den, d // HEAD_DIM), e_ref[...]) for e in es]
    for c in range(0, d, LANES):
        lanes = slice(c, c + LANES)
        acc = jnp.zeros((tm, LANES), _F32)
        for gi, dil in enumerate(dilations):
            acc = acc + wides[gi][:, lanes] * by_position(o_refs[gi], dil, lanes, obuf.at[(c // LANES) % 2, gi])
        acc_ref[:, lanes] = acc.astype(_BF16)
    out_ref[...] = h_ref[...] + _dot(acc_ref[...], w_ref[...])


def _dil_merge(h, outs, lses, w_out, batch, seq, dilations):
    t, d = h.shape
    tm = min(ROW_TILE, seq)
    nblk = seq // tm
    n_heads = d // HEAD_DIM
    piece_row = jnp.arange(LANES)
    expand = ((piece_row % n_heads)[:, None] == (jnp.arange(d) // HEAD_DIM)[None, :]) & (piece_row < 3 * n_heads)[:, None]
    res_spec = lambda dil, n: pl.BlockSpec((None, dil, tm // dil, n), lambda i: (i // nblk, 0, i % nblk, 0))
    n = len(dilations)
    return pl.pallas_call(
        functools.partial(_dil_merge_kernel, dilations=dilations),
        out_shape=jax.ShapeDtypeStruct((t, d), _F32),
        grid=(t // tm,),
        in_specs=([_row_spec(tm, d)] + [res_spec(dil, d) for dil in dilations] + [res_spec(dil, LANES) for dil in dilations]
                  + [_const_spec((LANES, d)), _const_spec(w_out.shape)]),
        out_specs=_row_spec(tm, d),
        scratch_shapes=[pltpu.VMEM((2, n, tm, LANES), _F32), pltpu.VMEM((n, tm, LANES), _F32),
                        pltpu.VMEM((tm, d), _BF16)],
        compiler_params=_params(1),
        name="dil_merge",
    )(h, *outs, *lses, expand.astype(_BF16), w_out)


def _dil_mixer(h, g, w_in, w_out, rope, batch, seq):
    t, d = h.shape
    dilations = tuple(dil for _, dil in DIL_CONFIGS)
    projs = _dil_proj(h, g, w_in.astype(_BF16), rope, batch, seq, dilations)
    outs, lses = [], []
    for (window, dil), proj in zip(DIL_CONFIGS, projs):
        assert seq % (dil * DIL_BAND) == 0
        proj = proj.reshape(t, 3 * d)
        o, lse = _dil_attn(proj, 0, proj, 1, proj, 2, batch * dil, seq // dil, d, window // dil)
        outs.append(o.reshape(batch, dil, seq // dil, d))
        lses.append(lse.reshape(batch, dil, seq // dil, LANES))
    return _dil_merge(h, outs, lses, w_out.astype(_BF16), batch, seq, dilations)


def _gelu(x):
    return 0.5 * x * (1.0 + lax.erf(x * math.sqrt(0.5)))


def _sgu_kernel(h_ref, g_ref, win_ref, nv_ref, ws_ref, bs_ref, wout_ref, o_ref, u_ref, v_ref, y_ref, *, chunk):
    x = h_ref[...]
    tm = x.shape[0]
    half = wout_ref.shape[0]
    gdim = half // SGU_GROUPS
    xn = _rms_norm(x, g_ref[...], NORM_EPS).astype(_BF16)
    ssq = jnp.zeros((tm, 1), _F32)
    for c in range(0, half, chunk):
        u_ref[:, c:c + chunk] = _gelu(_dot(xn, win_ref[:, c:c + chunk]))
        vv = _gelu(_dot(xn, win_ref[:, half + c:half + c + chunk]))
        v_ref[:, c:c + chunk] = vv
        ssq = ssq + jnp.sum(vv * vv, axis=1, keepdims=True)
    rstd = lax.rsqrt(ssq / half + NORM_EPS)
    row = lax.broadcasted_iota(jnp.int32, (SGU_CHUNK, SGU_CHUNK), 0)
    col = lax.broadcasted_iota(jnp.int32, (SGU_CHUNK, SGU_CHUNK), 1)
    for gi in range(SGU_GROUPS):
        cols = slice(gi * gdim, (gi + 1) * gdim)
        w = jnp.where(col <= row, ws_ref[gi], jnp.zeros((), _BF16))
        bias = bs_ref[:, gi:gi + 1]
        for r in range(0, tm, SGU_CHUNK):
            rows = slice(r, r + SGU_CHUNK)
            vn = (v_ref[rows, cols] * rstd[rows] * nv_ref[:, cols]).astype(_BF16)
            y_ref[rows, cols] = (u_ref[rows, cols] * (_dot(w, vn) + bias)).astype(_BF16)
    o_ref[...] = x + _dot(y_ref[...], wout_ref[...])


def _sgu_mixer(h, g, w_in, norm_v, w_s, b_s, w_out):
    t, d = h.shape
    half = w_out.shape[0]
    tm = min(ROW_TILE, t)
    assert tm % SGU_CHUNK == 0 and (half // SGU_GROUPS) % LANES == 0
    return pl.pallas_call(
        functools.partial(_sgu_kernel, chunk=_col_chunk(half)),
        out_shape=jax.ShapeDtypeStruct((t, d), _F32),
        grid=(t // tm,),
        in_specs=[_row_spec(tm, d), _const_spec((1, d)), _const_spec(w_in.shape), _const_spec((1, half)),
                  _const_spec(w_s.shape), _const_spec((SGU_CHUNK, SGU_GROUPS)), _const_spec(w_out.shape)],
        out_specs=_row_spec(tm, d),
        scratch_shapes=[pltpu.VMEM((tm, half), _F32), pltpu.VMEM((tm, half), _F32), pltpu.VMEM((tm, half), _BF16)],
        compiler_params=_params(1),
        name="sgu",
    )(h, g.reshape(1, d), w_in.astype(_BF16), norm_v.reshape(1, half), w_s.astype(_BF16), jnp.transpose(b_s),
      w_out.astype(_BF16))


def kernel(x, p, norm_ffn1, w_ffn1_in, w_ffn1_out, norm_mix, norm_ffn2, w_ffn2_in, w_ffn2_out, norm_ple, w_ple_gate, b_ple_gate, w_ple_proj, fox_w_in, fox_b_f, fox_w_out, dil_w_in, dil_w_out, diff_w_in, diff_lambda, diff_subln, diff_w_out, sgu_w_in, sgu_norm_v, sgu_w_s, sgu_b_s, sgu_w_out, norm_final):
    batch, seq, d = x.shape
    depth = p.shape[0]
    t = batch * seq
    h = x.reshape(t, d)
    rope = _rope_tables(seq)
    n_mixers = 4
    ffn1_in, ffn1_out = w_ffn1_in.astype(_BF16), w_ffn1_out.astype(_BF16)
    ffn2_in, ffn2_out = w_ffn2_in.astype(_BF16), w_ffn2_out.astype(_BF16)
    ple_gate, ple_proj = w_ple_gate.astype(_BF16), w_ple_proj.astype(_BF16)
    for i in range(depth):
        kind, j = i % n_mixers, i // n_mixers
        h = _ffn(h, norm_ffn1[i], ffn1_in, ffn1_out, i)
        attn = None
        if kind == 0:
            attn = _fox_mixer(h, norm_mix[i], fox_w_in[j], fox_b_f[j], fox_w_out[j], batch, seq)
        elif kind == 1:
            h = _dil_mixer(h, norm_mix[i], dil_w_in[j], dil_w_out[j], rope, batch, seq)
        elif kind == 2:
            attn = _diff_mixer(h, norm_mix[i], diff_w_in[j], diff_lambda[j], diff_subln[j], diff_w_out[j], rope,
                               batch, seq, i)
        else:
            h = _sgu_mixer(h, norm_mix[i], sgu_w_in[j], sgu_norm_v[j], sgu_w_s[j], sgu_b_s[j], sgu_w_out[j])
        h = _post_mixer(h, attn, norm_ffn2[i], ffn2_in, ffn2_out, p.reshape(depth, t, p.shape[-1]), i, norm_ple[i],
                        ple_gate, b_ple_gate[i], ple_proj, norm_final, final_norm=(i == depth - 1))
    return h.reshape(batch, seq, d)
```
